```python
import math
import jax, jax.numpy as jnp
from jax import lax
import numpy as np

D_MODEL = 2048
BATCH = 4
SEQ = 4096
DEPTH = 2

PL_DIM = 256
D_FF = 4 * D_MODEL
NORM_EPS = 1e-6
N_EVEN = (DEPTH + 1) // 2
N_ODD = DEPTH // 2
S5_WIDTH = D_MODEL // 4
S5_GROUP = 16
S5_GROUPS = S5_WIDTH // S5_GROUP
S5_STATE = 64
SSD_WIDTH = D_MODEL - S5_WIDTH
SSD_HEAD_DIM = 64
SSD_HEADS = SSD_WIDTH // SSD_HEAD_DIM
SSD_GROUPS = 4
SSD_STATE = 128
SSD_CONV = 4
SSD_CHUNK = 128
SSD_CONV_DIM = SSD_WIDTH + 2 * SSD_GROUPS * SSD_STATE
EVEN_IN = S5_WIDTH + SSD_WIDTH + SSD_CONV_DIM + SSD_HEADS
EVEN_MIX = S5_WIDTH + SSD_WIDTH
RWKV_WIDTH = D_MODEL // 2
RWKV_HEAD_DIM = 64
RWKV_HEADS = RWKV_WIDTH // RWKV_HEAD_DIM
RWKV_DECAY_LORA = 96
RWKV_AAA_LORA = 96
RWKV_GATE_LORA = 256
RWKV_GN_EPS = 64e-5
RWKV_IN = 3 * RWKV_WIDTH + RWKV_DECAY_LORA + RWKV_AAA_LORA + RWKV_GATE_LORA
LRU_WIDTH = D_MODEL - RWKV_WIDTH
LRU_BLOCKS = 16
LRU_BLOCK = LRU_WIDTH // LRU_BLOCKS
LRU_CONV = 4
LRU_C = 8.0
ODD_IN = RWKV_IN + 2 * LRU_WIDTH
ODD_MIX = RWKV_WIDTH + LRU_WIDTH

kernel_name = 'hybrid_s5_ssd_rwkv7_rglru_trunk'


def rmsnorm(x, g):
    xf = x.astype(jnp.float32)
    y = xf * lax.rsqrt(jnp.mean(xf * xf, axis=-1, keepdims=True) + NORM_EPS)
    return (y * g.astype(jnp.float32)).astype(x.dtype)


def causal_dwconv(x, w, b):
    k = w.shape[0]
    y = lax.conv_general_dilated(x, w.astype(x.dtype)[:, None, :], window_strides=(1,),
                                 padding=[(k - 1, 0)], dimension_numbers=('NWC', 'WIO', 'NWC'),
                                 feature_group_count=x.shape[-1])
    return y + b.astype(x.dtype)


def token_shift(x):
    return jnp.pad(x, ((0, 0), (1, 0), (0, 0)))[:, :-1]


def s5_mixer(u, lam_re, lam_im, log_step, b_re, b_im, c_re, c_im, d_skip, glu_w, glu_b):
    f32 = jnp.float32
    bsz, seq, _ = u.shape
    uf = u.astype(f32)
    ug = uf.reshape(bsz, seq, S5_GROUPS, S5_GROUP)
    step = jnp.exp(log_step.astype(f32))[:, None]
    lr, li = lam_re.astype(f32), lam_im.astype(f32)
    mag = jnp.exp(lr * step)
    abar_re, abar_im = mag * jnp.cos(li * step), mag * jnp.sin(li * step)
    den = lr * lr + li * li
    nr = abar_re - 1.0
    coef_re = ((nr * lr + abar_im * li) / den)[..., None]
    coef_im = ((abar_im * lr - nr * li) / den)[..., None]
    br, bi = b_re.astype(f32), b_im.astype(f32)
    bbar_re = coef_re * br - coef_im * bi
    bbar_im = coef_re * bi + coef_im * br
    bu_re = jnp.einsum('bsgh,gph->bsgp', ug, bbar_re)
    bu_im = jnp.einsum('bsgh,gph->bsgp', ug, bbar_im)
    a_re = jnp.broadcast_to(abar_re, bu_re.shape)
    a_im = jnp.broadcast_to(abar_im, bu_re.shape)

    def combine(e1, e2):
        a1r, a1i, b1r, b1i = e1
        a2r, a2i, b2r, b2i = e2
        return (a2r * a1r - a2i * a1i, a2r * a1i + a2i * a1r,
                a2r * b1r - a2i * b1i + b2r, a2r * b1i + a2i * b1r + b2i)

    _, _, xr, xi = lax.associative_scan(combine, (a_re, a_im, bu_re, bu_im), axis=1)
    y = (jnp.einsum('ghp,bsgp->bsgh', c_re.astype(f32), xr)
         - jnp.einsum('ghp,bsgp->bsgh', c_im.astype(f32), xi))
    y = y.reshape(bsz, seq, S5_WIDTH) + d_skip.astype(f32) * uf
    act = jax.nn.gelu(y)
    return act * jax.nn.sigmoid(act @ glu_w.astype(f32) + glu_b.astype(f32))


def ssd_chunked(x, da, bm, cm):
    bsz, seq, nh, hd = x.shape
    nc, L, g = seq // SSD_CHUNK, SSD_CHUNK, SSD_GROUPS
    j = nh // g
    x = x.reshape(bsz, nc, L, g, j, hd)
    bm = bm.reshape(bsz, nc, L, g, SSD_STATE)
    cm = cm.reshape(bsz, nc, L, g, SSD_STATE)
    a_cum = jnp.cumsum(da.reshape(bsz, nc, L, g, j).transpose(0, 3, 4, 1, 2), axis=-1)
    mask = jnp.tril(jnp.ones((L, L), dtype=bool))
    seg = a_cum[..., :, None] - a_cum[..., None, :]
    decay = jnp.exp(jnp.where(mask, seg, -jnp.inf))
    scores = jnp.einsum('bclgn,bcsgn->bgcls', cm, bm)
    y_diag = jnp.einsum('bgjcls,bcsgjp->bclgjp', scores[:, :, None] * decay, x)
    decay_states = jnp.exp(a_cum[..., -1:] - a_cum).transpose(0, 3, 4, 1, 2)[..., None]
    states = jnp.einsum('bclgn,bclgjp->bcgjpn', bm, x * decay_states)
    chunk_decay = jnp.exp(a_cum[..., -1])

    def step(carry, inp):
        st, dec = inp
        return carry * dec[..., None, None] + st, carry

    init = jnp.zeros((bsz, g, j, hd, SSD_STATE), x.dtype)
    _, prev = lax.scan(step, init, (jnp.moveaxis(states, 1, 0), jnp.moveaxis(chunk_decay, -1, 0)))
    prev = jnp.moveaxis(prev, 0, 1)
    y_off = (jnp.einsum('bclgn,bcgjpn->bclgjp', cm, prev)
             * jnp.exp(a_cum).transpose(0, 3, 4, 1, 2)[..., None])
    return (y_diag + y_off).reshape(bsz, seq, nh, hd)


def ssd_mixer(z, xbc, dt_raw, conv_w, conv_b, dt_bias, a_log, d_skip, norm_g):
    f32 = jnp.float32
    bsz, seq, _ = z.shape
    xbc = jax.nn.silu(causal_dwconv(xbc, conv_w, conv_b).astype(f32))
    xs, bm, cm = jnp.split(xbc, [SSD_WIDTH, SSD_WIDTH + SSD_GROUPS * SSD_STATE], axis=-1)
    xs = xs.reshape(bsz, seq, SSD_HEADS, SSD_HEAD_DIM)
    bm = bm.reshape(bsz, seq, SSD_GROUPS, SSD_STATE)
    cm = cm.reshape(bsz, seq, SSD_GROUPS, SSD_STATE)
    dt = jax.nn.softplus(dt_raw.astype(f32) + dt_bias.astype(f32))
    da = dt * (-jnp.exp(a_log.astype(f32)))
    y = ssd_chunked(xs * dt[..., None], da, bm, cm) + xs * d_skip.astype(f32)[:, None]
    y = y.reshape(bsz, seq, SSD_WIDTH) * jax.nn.silu(z.astype(f32))
    y = y.reshape(bsz, seq, SSD_GROUPS, SSD_WIDTH // SSD_GROUPS)
    y = y * lax.rsqrt(jnp.mean(y * y, axis=-1, keepdims=True) + NORM_EPS)
    return y.reshape(bsz, seq, SSD_WIDTH) * norm_g.astype(f32)


def rwkv7_mixer(f, mu, w0, w_up, a0, a_up, g_up, k_k, k_a, r_k, ln_g, ln_b):
    f32 = jnp.float32
    f = f.astype(f32)
    f = f + (token_shift(f) - f) * mu.astype(f32)
    W = RWKV_WIDTH
    r, k, v, wl, al, gl = jnp.split(
        f, [W, 2 * W, 3 * W, 3 * W + RWKV_DECAY_LORA, 3 * W + RWKV_DECAY_LORA + RWKV_AAA_LORA], axis=-1)
    w = -jax.nn.softplus(-(w0.astype(f32) + jnp.tanh(wl) @ w_up.astype(f32))) - 0.5
    decay = jnp.exp(-jnp.exp(w))
    a = jax.nn.sigmoid(a0.astype(f32) + al @ a_up.astype(f32))
    g = jax.nn.sigmoid(gl) @ g_up.astype(f32)
    kk = k * k_k.astype(f32)
    k = k * (1.0 + (a - 1.0) * k_a.astype(f32))
    bsz, seq, _ = f.shape
    hs = lambda t: t.reshape(bsz, seq, RWKV_HEADS, RWKV_HEAD_DIM)
    r, k, v, kk, a, decay = hs(r), hs(k), hs(v), hs(kk), hs(a), hs(decay)
    kk = kk * lax.rsqrt(jnp.maximum(jnp.sum(kk * kk, axis=-1, keepdims=True), 1e-24))

    def step(state, inp):
        r_t, w_t, k_t, v_t, kk_t, a_t = inp
        sa = jnp.einsum('bhvk,bhk->bhv', state, -kk_t)
        state = (state * w_t[:, :, None, :] + sa[..., None] * (kk_t * a_t)[:, :, None, :]
                 + v_t[..., None] * k_t[:, :, None, :])
        return state, jnp.einsum('bhvk,bhk->bhv', state, r_t)

    tm = lambda t: jnp.moveaxis(t, 1, 0)
    init = jnp.zeros((bsz, RWKV_HEADS, RWKV_HEAD_DIM, RWKV_HEAD_DIM), f32)
    _, y = lax.scan(step, init, (tm(r), tm(decay), tm(k), tm(v), tm(kk), tm(a)))
    y = jnp.moveaxis(y, 0, 1)
    mean = jnp.mean(y, axis=-1, keepdims=True)
    var = jnp.mean(jnp.square(y - mean), axis=-1, keepdims=True)
    y = ((y - mean) * lax.rsqrt(var + RWKV_GN_EPS)).reshape(bsz, seq, W)
    y = y * ln_g.astype(f32) + ln_b.astype(f32)
    bonus = jnp.sum(r * k * r_k.astype(f32), axis=-1, keepdims=True) * v
    return (y + bonus.reshape(bsz, seq, W)) * g


def rglru_mixer(xl, gl, conv_w, conv_b, w_a, b_a, w_x, b_x, lam):
    f32 = jnp.float32
    bsz, seq, _ = xl.shape
    xc = causal_dwconv(xl, conv_w, conv_b).astype(f32)
    xb = xc.reshape(bsz, seq, LRU_BLOCKS, LRU_BLOCK)
    gate_r = jax.nn.sigmoid(jnp.einsum('bshi,hij->bshj', xb, w_a.astype(f32)) + b_a.astype(f32))
    gate_i = jax.nn.sigmoid(jnp.einsum('bshi,hij->bshj', xb, w_x.astype(f32)) + b_x.astype(f32))
    log_a = -LRU_C * gate_r * jax.nn.softplus(-lam.astype(f32))
    a = jnp.exp(log_a)
    mult = jnp.sqrt(jnp.maximum(-jnp.expm1(2.0 * log_a), 0.0))
    mult = mult.at[:, 0].set(1.0)
    bx = xb * gate_i * mult

    def combine(e1, e2):
        a1, b1 = e1
        a2, b2 = e2
        return a1 * a2, a2 * b1 + b2

    _, hseq = lax.associative_scan(combine, (a, bx), axis=1)
    return hseq.reshape(bsz, seq, LRU_WIDTH) * jax.nn.gelu(gl.astype(f32))


def even_mixer(hn, in_proj, out_proj, lam_re, lam_im, log_step, b_re, b_im, c_re, c_im, s5_d,
               glu_w, glu_b, conv_w, conv_b, dt_bias, a_log, ssd_d, ssd_norm):
    proj = hn @ in_proj
    u, z, xbc, dt_raw = jnp.split(
        proj, [S5_WIDTH, S5_WIDTH + SSD_WIDTH, S5_WIDTH + SSD_WIDTH + SSD_CONV_DIM], axis=-1)
    y_a = s5_mixer(u, lam_re, lam_im, log_step, b_re, b_im, c_re, c_im, s5_d, glu_w, glu_b)
    y_b = ssd_mixer(z, xbc, dt_raw, conv_w, conv_b, dt_bias, a_log, ssd_d, ssd_norm)
    y = jnp.concatenate([y_a, y_b], axis=-1).astype(hn.dtype)
    return y @ out_proj


def odd_mixer(hn, in_proj, out_proj, mu, w0, w_up, a0, a_up, g_up, k_k, k_a, r_k, ln_g, ln_b,
              conv_w, conv_b, w_a, b_a, w_x, b_x, lam):
    proj = hn @ in_proj
    rw, xl, gl = jnp.split(proj, [RWKV_IN, RWKV_IN + LRU_WIDTH], axis=-1)
    y_c = rwkv7_mixer(rw, mu, w0, w_up, a0, a_up, g_up, k_k, k_a, r_k, ln_g, ln_b)
    y_d = rglru_mixer(xl, gl, conv_w, conv_b, w_a, b_a, w_x, b_x, lam)
    y = jnp.concatenate([y_c, y_d], axis=-1).astype(hn.dtype)
    return y @ out_proj


def squared_relu_mlp(h, w1, w2):
    return jnp.square(jax.nn.relu(h @ w1)) @ w2


def setup_inputs(seed: int = 0) -> dict:
    key = jax.random.key(seed)
    ks = iter(jax.random.split(key, 64))
    f32 = jnp.float32

    def nrm(shape, scale):
        return scale * jax.random.normal(next(ks), shape, f32)

    def unif(shape, lo, hi):
        return jax.random.uniform(next(ks), shape, f32, lo, hi)

    D = D_MODEL
    ne, no = N_EVEN, N_ODD
    x = nrm((BATCH, SEQ, D), 1.0)
    p = nrm((DEPTH, BATCH, SEQ, PL_DIM), 1.0)
    norm_mix = 1.0 + nrm((DEPTH, D), 0.02)
    norm_ffn = 1.0 + nrm((DEPTH, D), 0.02)
    norm_pl = 1.0 + nrm((DEPTH, D), 0.02)
    mlp_w1 = nrm((DEPTH, D, D_FF), D ** -0.5)
    mlp_w2 = nrm((DEPTH, D_FF, D), D_FF ** -0.5)
    pl_proj = nrm((DEPTH, PL_DIM, D), PL_DIM ** -0.5)
    pl_gate = nrm((DEPTH, D, D), D ** -0.5)
    e_in_proj = nrm((ne, D, EVEN_IN), D ** -0.5)
    e_out_proj = nrm((ne, EVEN_MIX, D), EVEN_MIX ** -0.5)
    s5_lam_re = -0.5 + nrm((ne, S5_GROUPS, S5_STATE), 0.01)
    s5_lam_im = math.pi * jnp.arange(S5_STATE, dtype=f32) + nrm((ne, S5_GROUPS, S5_STATE), 0.01)
    s5_log_step = unif((ne, S5_GROUPS), math.log(1e-3), math.log(1e-1))
    s5_b_re = nrm((ne, S5_GROUPS, S5_STATE, S5_GROUP), (2 * S5_GROUP) ** -0.5)
    s5_b_im = nrm((ne, S5_GROUPS, S5_STATE, S5_GROUP), (2 * S5_GROUP) ** -0.5)
    s5_c_re = nrm((ne, S5_GROUPS, S5_GROUP, S5_STATE), (2 * S5_STATE) ** -0.5)
    s5_c_im = nrm((ne, S5_GROUPS, S5_GROUP, S5_STATE), (2 * S5_STATE) ** -0.5)
    s5_d = nrm((ne, S5_WIDTH), 0.5)
    s5_glu_w = nrm((ne, S5_WIDTH, S5_WIDTH), S5_WIDTH ** -0.5)
    s5_glu_b = nrm((ne, S5_WIDTH), 0.02)
    ssd_conv_w = nrm((ne, SSD_CONV, SSD_CONV_DIM), SSD_CONV ** -0.5)
    ssd_conv_b = nrm((ne, SSD_CONV_DIM), 0.02)
    dt0 = jnp.exp(unif((ne, SSD_HEADS), math.log(1e-3), math.log(1e-1)))
    ssd_dt_bias = dt0 + jnp.log(-jnp.expm1(-dt0))
    ssd_a_log = jnp.log(unif((ne, SSD_HEADS), 1.0, 16.0))
    ssd_d = 1.0 + nrm((ne, SSD_HEADS), 0.02)
    ssd_norm = 1.0 + nrm((ne, SSD_WIDTH), 0.02)
    o_in_proj = nrm((no, D, ODD_IN), D ** -0.5)
    o_out_proj = nrm((no, ODD_MIX, D), ODD_MIX ** -0.5)
    rwkv_mu = unif((no, RWKV_IN), 0.0, 1.0)
    rwkv_w0 = jnp.linspace(-6.0, -1.0, RWKV_WIDTH, dtype=f32) + nrm((no, RWKV_WIDTH), 0.1)
    rwkv_w_up = nrm((no, RWKV_DECAY_LORA, RWKV_WIDTH), 0.5 * RWKV_DECAY_LORA ** -0.5)
    rwkv_a0 = nrm((no, RWKV_WIDTH), 0.1)
    rwkv_a_up = nrm((no, RWKV_AAA_LORA, RWKV_WIDTH), 0.5 * RWKV_AAA_LORA ** -0.5)
    rwkv_g_up = nrm((no, RWKV_GATE_LORA, RWKV_WIDTH), RWKV_GATE_LORA ** -0.5)
    rwkv_k_k = 0.85 + nrm((no, RWKV_WIDTH), 0.02)
    rwkv_k_a = 1.0 + nrm((no, RWKV_WIDTH), 0.02)
    rwkv_r_k = nrm((no, RWKV_HEADS, RWKV_HEAD_DIM), 0.1)
    rwkv_ln_g = 1.0 + nrm((no, RWKV_WIDTH), 0.02)
    rwkv_ln_b = nrm((no, RWKV_WIDTH), 0.02)
    lru_conv_w = nrm((no, LRU_CONV, LRU_WIDTH), LRU_CONV ** -0.5)
    lru_conv_b = nrm((no, LRU_WIDTH), 0.02)
    lru_w_a = nrm((no, LRU_BLOCKS, LRU_BLOCK, LRU_BLOCK), LRU_BLOCK ** -0.5)
    lru_b_a = nrm((no, LRU_BLOCKS, LRU_BLOCK), 0.02)
    lru_w_x = nrm((no, LRU_BLOCKS, LRU_BLOCK, LRU_BLOCK), LRU_BLOCK ** -0.5)
    lru_b_x = nrm((no, LRU_BLOCKS, LRU_BLOCK), 0.02)
    a_pow = unif((no, LRU_BLOCKS, LRU_BLOCK), 0.9, 0.999)
    a_base = a_pow ** (1.0 / LRU_C)
    lru_lam = jnp.log(a_base) - jnp.log1p(-a_base)
    norm_final = 1.0 + nrm((D,), 0.02)
    return {'x': x, 'p': p, 'norm_mix': norm_mix, 'norm_ffn': norm_ffn, 'norm_pl': norm_pl,
            'mlp_w1': mlp_w1, 'mlp_w2': mlp_w2, 'pl_proj': pl_proj, 'pl_gate': pl_gate,
            'e_in_proj': e_in_proj, 'e_out_proj': e_out_proj,
            's5_lam_re': s5_lam_re, 's5_lam_im': s5_lam_im, 's5_log_step': s5_log_step,
            's5_b_re': s5_b_re, 's5_b_im': s5_b_im, 's5_c_re': s5_c_re, 's5_c_im': s5_c_im,
            's5_d': s5_d, 's5_glu_w': s5_glu_w, 's5_glu_b': s5_glu_b,
            'ssd_conv_w': ssd_conv_w, 'ssd_conv_b': ssd_conv_b, 'ssd_dt_bias': ssd_dt_bias,
            'ssd_a_log': ssd_a_log, 'ssd_d': ssd_d, 'ssd_norm': ssd_norm,
            'o_in_proj': o_in_proj, 'o_out_proj': o_out_proj,
            'rwkv_mu': rwkv_mu, 'rwkv_w0': rwkv_w0, 'rwkv_w_up': rwkv_w_up, 'rwkv_a0': rwkv_a0,
            'rwkv_a_up': rwkv_a_up, 'rwkv_g_up': rwkv_g_up, 'rwkv_k_k': rwkv_k_k, 'rwkv_k_a': rwkv_k_a,
            'rwkv_r_k': rwkv_r_k, 'rwkv_ln_g': rwkv_ln_g, 'rwkv_ln_b': rwkv_ln_b,
            'lru_conv_w': lru_conv_w, 'lru_conv_b': lru_conv_b, 'lru_w_a': lru_w_a, 'lru_b_a': lru_b_a,
            'lru_w_x': lru_w_x, 'lru_b_x': lru_b_x, 'lru_lam': lru_lam, 'norm_final': norm_final}


def reference(x, p, norm_mix, norm_ffn, norm_pl, mlp_w1, mlp_w2, pl_proj, pl_gate,
              e_in_proj, e_out_proj, s5_lam_re, s5_lam_im, s5_log_step, s5_b_re, s5_b_im,
              s5_c_re, s5_c_im, s5_d, s5_glu_w, s5_glu_b, ssd_conv_w, ssd_conv_b, ssd_dt_bias,
              ssd_a_log, ssd_d, ssd_norm, o_in_proj, o_out_proj, rwkv_mu, rwkv_w0, rwkv_w_up,
              rwkv_a0, rwkv_a_up, rwkv_g_up, rwkv_k_k, rwkv_k_a, rwkv_r_k, rwkv_ln_g, rwkv_ln_b,
              lru_conv_w, lru_conv_b, lru_w_a, lru_b_a, lru_w_x, lru_b_x, lru_lam, norm_final):
    h = x
    for i in range(DEPTH):
        hn = rmsnorm(h, norm_mix[i])
        j = i // 2
        if i % 2 == 0:
            mix = even_mixer(hn, e_in_proj[j], e_out_proj[j], s5_lam_re[j], s5_lam_im[j],
                             s5_log_step[j], s5_b_re[j], s5_b_im[j], s5_c_re[j], s5_c_im[j],
                             s5_d[j], s5_glu_w[j], s5_glu_b[j], ssd_conv_w[j], ssd_conv_b[j],
                             ssd_dt_bias[j], ssd_a_log[j], ssd_d[j], ssd_norm[j])
        else:
            mix = odd_mixer(hn, o_in_proj[j], o_out_proj[j], rwkv_mu[j], rwkv_w0[j], rwkv_w_up[j],
                            rwkv_a0[j], rwkv_a_up[j], rwkv_g_up[j], rwkv_k_k[j], rwkv_k_a[j],
                            rwkv_r_k[j], rwkv_ln_g[j], rwkv_ln_b[j], lru_conv_w[j], lru_conv_b[j],
                            lru_w_a[j], lru_b_a[j], lru_w_x[j], lru_b_x[j], lru_lam[j])
        h = h + mix
        h = h + squared_relu_mlp(rmsnorm(h, norm_ffn[i]), mlp_w1[i], mlp_w2[i])
        gate = jax.nn.sigmoid(rmsnorm(h, norm_pl[i]) @ pl_gate[i])
        h = h + gate * (p[i] @ pl_proj[i])
    return rmsnorm(h, norm_final)
```

```python
import functools
import math

import jax
import jax.numpy as jnp
from jax import lax
from jax.experimental import pallas as pl
from jax.experimental.pallas import tpu as pltpu

F32 = jnp.float32
BF16 = jnp.bfloat16

V7X_LANES = 128
V7X_SUBLANES = 8
V7X_VMEM_BYTES = 64 * 1024 * 1024
VMEM_LIMIT_BYTES = V7X_VMEM_BYTES - 8 * 1024 * 1024

NORM_EPS = 1e-6
S5_GROUP = 16
S5_STATE = 64
SSD_HEAD_DIM = 64
SSD_HEADS = 24
SSD_GROUPS = 4
SSD_STATE = 128
SSD_CHUNK = 128
CONV_K = 4
RWKV_HEAD_DIM = 64
RWKV_GN_EPS = 64e-5
RWKV_CHUNK = 64
RWKV_INV_BLOCK = 16
LRU_BLOCK = 64
LRU_C = 8.0


def _cparams(*sem):
    return pltpu.CompilerParams(dimension_semantics=sem, vmem_limit_bytes=VMEM_LIMIT_BYTES)


def _bdot(a, b):
    return jnp.dot(a.astype(BF16), b.astype(BF16), preferred_element_type=F32)


def _bdot_nt(a, b):
    return lax.dot_general(a.astype(BF16), b.astype(BF16), (((1,), (1,)), ((), ())),
                           preferred_element_type=F32)


def _bdot_tn(a, b):
    return lax.dot_general(a.astype(BF16), b.astype(BF16), (((0,), (0,)), ((), ())),
                           preferred_element_type=F32)


def _split2(x):
    hi = x.astype(BF16)
    lo = (x - hi.astype(F32)).astype(BF16)
    return hi, lo


def _split3(x):
    x1 = x.astype(BF16)
    r1 = x - x1.astype(F32)
    x2 = r1.astype(BF16)
    x3 = (r1 - x2.astype(F32)).astype(BF16)
    return x1, x2, x3


def _dot_lhs01(m01, x):
    m = m01.astype(BF16)
    x1, x2, x3 = _split3(x)
    d = lambda v: jnp.dot(m, v, preferred_element_type=F32)
    return d(x1) + d(x2) + d(x3)


def _dot_rhs01(x, m01):
    m = m01.astype(BF16)
    x1, x2, x3 = _split3(x)
    d = lambda v: jnp.dot(v, m, preferred_element_type=F32)
    return d(x1) + d(x2) + d(x3)


def _dot3(a, b):
    a1, a2 = _split2(a)
    b1, b2 = _split2(b)
    d = lambda u, v: jnp.dot(u, v, preferred_element_type=F32)
    return d(a1, b1) + (d(a1, b2) + d(a2, b1))


def _softplus(x):
    return jnp.maximum(x, 0.0) + jnp.log1p(jnp.exp(-jnp.abs(x)))


def _sigmoid(x):
    return jax.nn.sigmoid(x)


def _gelu(x):
    return jax.nn.gelu(x, approximate=True)


def _silu(x):
    return x * _sigmoid(x)


def _row_iota(shape):
    return lax.broadcasted_iota(jnp.int32, shape, 0)


def _causal_conv(x, tail_ref, w_ref, b_ref):
    rows = x.shape[0]
    xe = jnp.concatenate([tail_ref[...], x], axis=0)
    acc = b_ref[...] + w_ref[CONV_K - 1:CONV_K, :] * x
    for k in range(CONV_K - 1):
        sh = pltpu.roll(xe, CONV_K - 1 - k, 0)
        acc = acc + w_ref[k:k + 1, :] * sh[V7X_SUBLANES:, :]
    tail_ref[...] = x[rows - V7X_SUBLANES:, :]
    return acc


def _rmsnorm_rows(h, g):
    ms = jnp.mean(h * h, axis=-1, keepdims=True)
    return h * lax.rsqrt(ms + NORM_EPS) * g


def _norm_mm_kernel(h_ref, g_ref, w_ref, o_ref, hn_ref, *, act):
    @pl.when(pl.program_id(1) == 0)
    def _():
        hn_ref[...] = _rmsnorm_rows(h_ref[...], g_ref[...]).astype(BF16)

    y = jnp.dot(hn_ref[...], w_ref[...], preferred_element_type=F32)
    if act == "relu2":
        y = jnp.square(jnp.maximum(y, 0.0))
    o_ref[...] = y.astype(o_ref.dtype)


def _norm_matmul(h, g, w, *, act=None, out_dtype=F32, tm=512, tn=512):
    m, k = h.shape
    n = w.shape[1]
    tm, tn = min(tm, m), min(tn, n)
    assert m % tm == 0 and n % tn == 0, (m, n, tm, tn)
    return pl.pallas_call(
        functools.partial(_norm_mm_kernel, act=act),
        grid=(m // tm, n // tn),
        in_specs=[pl.BlockSpec((tm, k), lambda i, j: (i, 0)),
                  pl.BlockSpec((1, k), lambda i, j: (0, 0)),
                  pl.BlockSpec((k, tn), lambda i, j: (0, j))],
        out_specs=pl.BlockSpec((tm, tn), lambda i, j: (i, j)),
        out_shape=jax.ShapeDtypeStruct((m, n), out_dtype),
        scratch_shapes=[pltpu.VMEM((tm, k), BF16)],
        compiler_params=_cparams("parallel", "arbitrary"),
        name="norm_matmul",
    )(h, g.reshape(1, k), w)


def _mm_res_kernel(a_ref, w_ref, r_ref, o_ref, acc_ref, *, nk):
    kk = pl.program_id(2)

    @pl.when(kk == 0)
    def _():
        acc_ref[...] = jnp.zeros_like(acc_ref)

    acc_ref[...] += jnp.dot(a_ref[...], w_ref[...], preferred_element_type=F32)

    @pl.when(kk == nk - 1)
    def _():
        o_ref[...] = r_ref[...] + acc_ref[...]


def _matmul_residual(a, w, res, *, tm=512, tn=512, tk=2048):
    m, k = a.shape
    n = w.shape[1]
    tm, tn, tk = min(tm, m), min(tn, n), min(tk, k)
    assert m % tm == 0 and n % tn == 0 and k % tk == 0
    nk = k // tk
    return pl.pallas_call(
        functools.partial(_mm_res_kernel, nk=nk),
        grid=(m // tm, n // tn, nk),
        in_specs=[pl.BlockSpec((tm, tk), lambda i, j, kk: (i, kk)),
                  pl.BlockSpec((tk, tn), lambda i, j, kk: (kk, j)),
                  pl.BlockSpec((tm, tn), lambda i, j, kk: (i, j))],
        out_specs=pl.BlockSpec((tm, tn), lambda i, j, kk: (i, j)),
        out_shape=jax.ShapeDtypeStruct((m, n), F32),
        scratch_shapes=[pltpu.VMEM((tm, tn), F32)],
        compiler_params=_cparams("parallel", "parallel", "arbitrary"),
        name="matmul_residual",
    )(a, w, res)


def _gate_kernel(h_ref, g_ref, wg_ref, p_ref, wp_ref, hres_ref, o_ref, hn_ref):
    @pl.when(pl.program_id(1) == 0)
    def _():
        hn_ref[...] = _rmsnorm_rows(h_ref[...], g_ref[...]).astype(BF16)

    gate = _sigmoid(jnp.dot(hn_ref[...], wg_ref[...], preferred_element_type=F32))
    pp = jnp.dot(p_ref[...].astype(BF16), wp_ref[...], preferred_element_type=F32)
    o_ref[...] = hres_ref[...] + gate * pp


def _gated_embed(h, g, wg, p, wp, *, tm=512, tn=512):
    m, k = h.shape
    n = wg.shape[1]
    kp = p.shape[1]
    tm, tn = min(tm, m), min(tn, n)
    assert m % tm == 0 and n % tn == 0
    return pl.pallas_call(
        _gate_kernel,
        grid=(m // tm, n // tn),
        in_specs=[pl.BlockSpec((tm, k), lambda i, j: (i, 0)),
                  pl.BlockSpec((1, k), lambda i, j: (0, 0)),
                  pl.BlockSpec((k, tn), lambda i, j: (0, j)),
                  pl.BlockSpec((tm, kp), lambda i, j: (i, 0)),
                  pl.BlockSpec((kp, tn), lambda i, j: (0, j)),
                  pl.BlockSpec((tm, tn), lambda i, j: (i, j))],
        out_specs=pl.BlockSpec((tm, tn), lambda i, j: (i, j)),
        out_shape=jax.ShapeDtypeStruct((m, n), F32),
        scratch_shapes=[pltpu.VMEM((tm, k), BF16)],
        compiler_params=_cparams("parallel", "arbitrary"),
        name="gated_embed",
    )(h, g.reshape(1, k), wg, p, wp, h)


def _final_norm_kernel(h_ref, g_ref, o_ref):
    o_ref[...] = _rmsnorm_rows(h_ref[...], g_ref[...])


def _final_norm(h, g, *, tm=512):
    m, k = h.shape
    tm = min(tm, m)
    return pl.pallas_call(
        _final_norm_kernel,
        grid=(m // tm,),
        in_specs=[pl.BlockSpec((tm, k), lambda i: (i, 0)), pl.BlockSpec((1, k), lambda i: (0, 0))],
        out_specs=pl.BlockSpec((tm, k), lambda i: (i, 0)),
        out_shape=jax.ShapeDtypeStruct((m, k), F32),
        compiler_params=_cparams("parallel"),
        name="final_norm",
    )(h, g.reshape(1, k))


def _s5_prep_kernel(lr_ref, li_ref, st_ref, btr_ref, bti_ref,
                    bbd_re_ref, bbd_im_ref, apw_re_ref, apw_im_ref, acar_re_ref, acar_im_ref):
    lr, li, st = lr_ref[...], li_ref[...], jnp.exp(st_ref[...])
    lrs, lis = lr * st, li * st

    def apow(n):
        mag = jnp.exp(n * lrs)
        return mag * jnp.cos(n * lis), mag * jnp.sin(n * lis)

    mag = jnp.exp(lrs)
    abar_re, abar_im = mag * jnp.cos(lis), mag * jnp.sin(lis)
    den = lr * lr + li * li
    nr = abar_re - 1.0
    coef_re = (nr * lr + abar_im * li) / den
    coef_im = (abar_im * lr - nr * li) / den
    btr, bti = btr_ref[...], bti_ref[...]
    bbar_re = coef_re * btr - coef_im * bti
    bbar_im = coef_re * bti + coef_im * btr
    rows, cols = bbd_re_ref.shape
    reps = rows // S5_GROUP
    rg = lax.broadcasted_iota(jnp.int32, (rows, cols), 0) // S5_GROUP
    cg = lax.broadcasted_iota(jnp.int32, (rows, cols), 1) // S5_STATE
    same = rg == cg
    bbd_re_ref[...] = jnp.where(same, jnp.concatenate([bbar_re] * reps, axis=0), 0.0).astype(BF16)
    bbd_im_ref[...] = jnp.where(same, jnp.concatenate([bbar_im] * reps, axis=0), 0.0).astype(BF16)
    nlev = apw_re_ref.shape[0]
    lev = lax.broadcasted_iota(jnp.int32, (nlev, 1), 0)
    pw = jnp.left_shift(jnp.ones_like(lev), lev).astype(F32)
    apw_re_ref[...], apw_im_ref[...] = apow(pw)
    tt = acar_re_ref.shape[0]
    steps = (lax.broadcasted_iota(jnp.int32, (tt, 1), 0) + 1).astype(F32)
    acar_re_ref[...], acar_im_ref[...] = apow(steps)


def _s5_kernel(u_ref, bre_ref, bim_ref, cre_ref, cim_ref, apr_ref, api_ref, acr_ref, aci_ref,
               d_ref, gw_ref, gb_ref, o_ref, car_re, car_im, *, nlev, lane_blk):
    @pl.when(pl.program_id(1) == 0)
    def _():
        car_re[...] = jnp.zeros_like(car_re)
        car_im[...] = jnp.zeros_like(car_im)

    u = u_ref[...]
    tt, width = u.shape
    nstate = bre_ref.shape[1]
    row = _row_iota((tt, lane_blk))
    ch_blk = lane_blk * S5_GROUP // S5_STATE
    ys = []
    for j in range(nstate // lane_blk):
        ls = slice(j * lane_blk, (j + 1) * lane_blk)
        cs = slice(j * ch_blk, (j + 1) * ch_blk)
        uj = u[:, cs]
        xr = _bdot(uj, bre_ref[cs, ls])
        xi = _bdot(uj, bim_ref[cs, ls])
        for lv in range(nlev):
            s = 1 << lv
            pr, pi = apr_ref[lv:lv + 1, ls], api_ref[lv:lv + 1, ls]
            sr, si = pltpu.roll(xr, s, 0), pltpu.roll(xi, s, 0)
            keep = row >= s
            xr, xi = (xr + jnp.where(keep, pr * sr - pi * si, 0.0),
                      xi + jnp.where(keep, pr * si + pi * sr, 0.0))
        cr, ci = car_re[0:1, ls], car_im[0:1, ls]
        ar, ai = acr_ref[:, ls], aci_ref[:, ls]
        xr, xi = xr + (ar * cr - ai * ci), xi + (ar * ci + ai * cr)
        car_re[0:1, ls] = xr[tt - 1:tt, :]
        car_im[0:1, ls] = xi[tt - 1:tt, :]
        ys.append(_bdot(xr, cre_ref[ls, cs]) - _bdot(xi, cim_ref[ls, cs]))
    y = jnp.concatenate(ys, axis=1) + d_ref[...] * u
    act = _gelu(y)
    o_ref[...] = (act * _sigmoid(_bdot(act, gw_ref[...]) + gb_ref[...])).astype(o_ref.dtype)


def _s5_mixer(proj, col_blk, bsz, seq, lam_re, lam_im, log_step, b_re, b_im, c_re, c_im,
              d_skip, glu_w, glu_b, *, tt=256, lane_blk=512):
    ng, ns = lam_re.shape
    width = ng * S5_GROUP
    nstate = ng * ns
    tt = min(tt, seq)
    nlev = int(math.log2(tt))
    assert 1 << nlev == tt and seq % tt == 0
    nlev_pad = -(-nlev // V7X_SUBLANES) * V7X_SUBLANES
    flat = lambda a: a.astype(F32).reshape(1, nstate)
    step = jnp.broadcast_to(log_step.astype(F32)[:, None], (ng, ns))
    bt = lambda a: a.astype(F32).transpose(2, 0, 1).reshape(S5_GROUP, nstate)
    vm = pl.BlockSpec(memory_space=pltpu.VMEM)
    bbd_re, bbd_im, apw_re, apw_im, acar_re, acar_im = pl.pallas_call(
        _s5_prep_kernel,
        in_specs=[vm] * 5,
        out_specs=[vm] * 6,
        out_shape=[jax.ShapeDtypeStruct((width, nstate), BF16)] * 2
        + [jax.ShapeDtypeStruct((nlev_pad, nstate), F32)] * 2
        + [jax.ShapeDtypeStruct((tt, nstate), F32)] * 2,
        compiler_params=pltpu.CompilerParams(vmem_limit_bytes=VMEM_LIMIT_BYTES),
        name="s5_prep",
    )(flat(lam_re), flat(lam_im), flat(step), bt(b_re), bt(b_im))
    eye = jnp.eye(ng, dtype=F32)
    cbd = lambda c: (c.astype(F32).transpose(0, 2, 1)[:, :, None, :] * eye[:, None, :, None]
                     ).reshape(nstate, width).astype(BF16)
    nt = seq // tt
    const = lambda shape: pl.BlockSpec(shape, lambda b, t: (0, 0))
    return pl.pallas_call(
        functools.partial(_s5_kernel, nlev=nlev, lane_blk=lane_blk),
        grid=(bsz, nt),
        in_specs=[pl.BlockSpec((tt, width), lambda b, t: (b * nt + t, col_blk)),
                  const((width, nstate)), const((width, nstate)),
                  const((nstate, width)), const((nstate, width)),
                  const((nlev_pad, nstate)), const((nlev_pad, nstate)),
                  const((tt, nstate)), const((tt, nstate)),
                  const((1, width)), const((width, width)), const((1, width))],
        out_specs=pl.BlockSpec((tt, width), lambda b, t: (b * nt + t, 0)),
        out_shape=jax.ShapeDtypeStruct((bsz * seq, width), BF16),
        scratch_shapes=[pltpu.VMEM((V7X_SUBLANES, nstate), F32)] * 2,
        compiler_params=_cparams("parallel", "arbitrary"),
        name="s5_mixer",
    )(proj, bbd_re, bbd_im, cbd(c_re), cbd(c_im), apw_re, apw_im, acar_re, acar_im,
      d_skip.astype(F32).reshape(1, width), glu_w.astype(BF16), glu_b.astype(F32).reshape(1, width))


def _ssd_kernel(z_ref, xs_ref, bc_ref, dt_ref, cwx_ref, cbx_ref, cwb_ref, cbb_ref, e64_ref,
                dtb_ref, alog_ref, dtbx_ref, alogx_ref, dskx_ref, ng_ref, o_ref,
                tailx, tailb, st_ref):
    @pl.when(pl.program_id(1) == 0)
    def _():
        tailx[...] = jnp.zeros_like(tailx)
        tailb[...] = jnp.zeros_like(tailb)
        st_ref[...] = jnp.zeros_like(st_ref)

    L = SSD_CHUNK
    tt = xs_ref.shape[0]
    width = xs_ref.shape[1]
    gstate = SSD_GROUPS * SSD_STATE
    xs = _silu(_causal_conv(xs_ref[...], tailx, cwx_ref, cbx_ref))
    bc = _silu(_causal_conv(bc_ref[...], tailb, cwb_ref, cbb_ref))
    dt_raw = dt_ref[...]
    da = _softplus(dt_raw + dtb_ref[...]) * (-jnp.exp(alog_ref[...]))
    dtx = _softplus(_dot_rhs01(dt_raw, e64_ref[...]) + dtbx_ref[...])
    dax = dtx * (-jnp.exp(alogx_ref[...]))
    xdt = xs * dtx
    ri = lax.broadcasted_iota(jnp.int32, (L, L), 0)
    ci = lax.broadcasted_iota(jnp.int32, (L, L), 1)
    tril = ri >= ci
    tri01 = tril.astype(BF16)
    lane_lo = lax.broadcasted_iota(jnp.int32, (1, V7X_LANES), 1) < SSD_HEAD_DIM
    pairs_per_group = SSD_HEADS // SSD_GROUPS // 2
    ys = []
    for c in range(tt // L):
        rs = slice(c * L, (c + 1) * L)
        acum = _dot_lhs01(tri01, da[rs, :])
        acum_t = acum.T
        acx = _dot_lhs01(tri01, dax[rs, :])
        ycols = []
        for g in range(SSD_GROUPS):
            bg = bc[rs, g * SSD_STATE:(g + 1) * SSD_STATE]
            cg = bc[rs, gstate + g * SSD_STATE:gstate + (g + 1) * SSD_STATE]
            scores = _bdot_nt(cg, bg)
            bg_t = bg.T
            for pr in range(pairs_per_group):
                q = g * pairs_per_group + pr
                ls = slice(q * V7X_LANES, (q + 1) * V7X_LANES)
                xp = xdt[rs, ls]
                acp = acx[:, ls]
                yd = []
                for h in (2 * q, 2 * q + 1):
                    seg = acum[:, h:h + 1] - acum_t[h:h + 1, :]
                    yd.append(_bdot(scores * jnp.exp(jnp.where(tril, seg, -jnp.inf)), xp))
                y = jnp.where(lane_lo, yd[0], yd[1])
                prev_t = st_ref[q]
                y = y + _bdot(cg, prev_t) * jnp.exp(acp)
                last = acp[L - 1:L, :]
                st_ref[q] = prev_t * jnp.exp(last) + _bdot(bg_t, xp * jnp.exp(last - acp))
                ycols.append(y)
        ys.append(jnp.concatenate(ycols, axis=1))
    y = jnp.concatenate(ys, axis=0) if len(ys) > 1 else ys[0]
    y = (y + xs * dskx_ref[...]) * _silu(z_ref[...])
    gw = width // SSD_GROUPS
    outs = []
    for g in range(SSD_GROUPS):
        yg = y[:, g * gw:(g + 1) * gw]
        outs.append(yg * lax.rsqrt(jnp.mean(yg * yg, axis=-1, keepdims=True) + NORM_EPS))
    o_ref[...] = (jnp.concatenate(outs, axis=1) * ng_ref[...]).astype(o_ref.dtype)


def _ssd_mixer(proj, blk_z, blk_xs, blk_bc, blk_dt, bsz, seq, conv_w, conv_b, dt_bias, a_log,
               d_skip, norm_g, *, tt=256):
    width = SSD_HEADS * SSD_HEAD_DIM
    gstate2 = 2 * SSD_GROUPS * SSD_STATE
    tt = min(tt, seq)
    assert seq % tt == 0 and tt % SSD_CHUNK == 0
    nt = seq // tt
    f = lambda a: a.astype(F32)
    pad_h = lambda a: jnp.pad(f(a), (0, V7X_LANES - SSD_HEADS)).reshape(1, V7X_LANES)
    exp_h = lambda a: jnp.repeat(f(a), SSD_HEAD_DIM).reshape(1, width)
    e64 = (jnp.arange(V7X_LANES)[:, None] == (jnp.arange(width)[None, :] // SSD_HEAD_DIM)).astype(BF16)
    const = lambda shape: pl.BlockSpec(shape, lambda b, t: (0,) * len(shape))
    rowblk = lambda w, cb: pl.BlockSpec((tt, w), lambda b, t: (b * nt + t, cb))
    return pl.pallas_call(
        _ssd_kernel,
        grid=(bsz, nt),
        in_specs=[rowblk(width, blk_z), rowblk(width, blk_xs), rowblk(gstate2, blk_bc),
                  rowblk(V7X_LANES, blk_dt),
                  const((CONV_K, width)), const((1, width)), const((CONV_K, gstate2)), const((1, gstate2)),
                  const((V7X_LANES, width)),
                  const((1, V7X_LANES)), const((1, V7X_LANES)),
                  const((1, width)), const((1, width)), const((1, width)), const((1, width))],
        out_specs=pl.BlockSpec((tt, width), lambda b, t: (b * nt + t, 0)),
        out_shape=jax.ShapeDtypeStruct((bsz * seq, width), BF16),
        scratch_shapes=[pltpu.VMEM((V7X_SUBLANES, width), F32),
                        pltpu.VMEM((V7X_SUBLANES, gstate2), F32),
                        pltpu.VMEM((SSD_HEADS // 2, SSD_STATE, V7X_LANES), F32)],
        compiler_params=_cparams("parallel", "arbitrary"),
        name="ssd_mixer",
    )(proj, proj, proj, proj,
      f(conv_w[:, :width]), f(conv_b[:width]).reshape(1, width),
      f(conv_w[:, width:]), f(conv_b[width:]).reshape(1, gstate2),
      e64, pad_h(dt_bias), pad_h(a_log), exp_h(dt_bias), exp_h(a_log), exp_h(d_skip),
      f(norm_g).reshape(1, width))


def _stack_heads(x, lane_lo):
    return jnp.concatenate([jnp.where(lane_lo, x, 0.0), jnp.where(lane_lo, 0.0, x)], axis=0)


def _unit_lower_inverse(a, same_blk, eye):
    ad = jnp.where(same_blk, a, 0.0)
    ao = a - ad
    a2 = _dot3(ad, ad)
    a4 = _dot3(a2, a2)
    a8 = _dot3(a4, a4)
    td = _dot3(_dot3(_dot3(eye + ad, eye + a2), eye + a4), eye + a8)
    n = _dot3(td, ao)
    n2 = _dot3(n, n)
    return _dot3(_dot3(eye + n, eye + n2), td)


def _rwkv_kernel(rkv_ref, lo_ref, mu_rkv_ref, mu_lo_ref, w0_ref, wup_ref, a0_ref, aup_ref, gup_ref,
                 kk_ref, ka_ref, rk_ref, lng_ref, lnb_ref, ones_ref, o_ref,
                 prev_rkv, prev_lo, state_ref):
    @pl.when(pl.program_id(1) == 0)
    def _():
        prev_rkv[...] = jnp.zeros_like(prev_rkv)
        prev_lo[...] = jnp.zeros_like(prev_lo)
        state_ref[...] = jnp.zeros_like(state_ref)

    L = RWKV_CHUNK
    tb = rkv_ref.shape[0]
    width = rkv_ref.shape[1] // 3

    def shift_mix(ref, prev, mu_ref):
        f = ref[...]
        fs = jnp.where(_row_iota(f.shape) == 0, prev[0:1, :], pltpu.roll(f, 1, 0))
        prev[0:1, :] = f[tb - 1:tb, :]
        return f + (fs - f) * mu_ref[...]

    rkv = shift_mix(rkv_ref, prev_rkv, mu_rkv_ref)
    lo = shift_mix(lo_ref, prev_lo, mu_lo_ref)
    r, k, v = rkv[:, :width], rkv[:, width:2 * width], rkv[:, 2 * width:]
    nl = wup_ref.shape[0]
    wl, al, gl = lo[:, :nl], lo[:, nl:2 * nl], lo[:, 2 * nl:]
    w = -_softplus(-(w0_ref[...] + _bdot(jnp.tanh(wl), wup_ref[...]))) - 0.5
    logw = -jnp.exp(w)
    a_sig = _sigmoid(a0_ref[...] + _bdot(al, aup_ref[...]))
    g = _bdot(_sigmoid(gl), gup_ref[...])
    kk = k * kk_ref[...]
    k = k * (1.0 + (a_sig - 1.0) * ka_ref[...])
    ones_bd = ones_ref[...]

    def head_sum(x):
        hi, lo_ = _split2(x)
        return (jnp.dot(hi, ones_bd, preferred_element_type=F32)
                + jnp.dot(lo_, ones_bd, preferred_element_type=F32))

    P2 = 2 * L
    ri = lax.broadcasted_iota(jnp.int32, (P2, P2), 0)
    ci = lax.broadcasted_iota(jnp.int32, (P2, P2), 1)
    same_head = (ri // L) == (ci // L)
    strict = same_head & (ri > ci)
    incl = same_head & (ri >= ci)
    same_blk = (ri // RWKV_INV_BLOCK) == (ci // RWKV_INV_BLOCK)
    eye = (ri == ci).astype(F32)
    tri01 = (lax.broadcasted_iota(jnp.int32, (L, L), 0) >= lax.broadcasted_iota(jnp.int32, (L, L), 1)).astype(BF16)
    lane_lo = lax.broadcasted_iota(jnp.int32, (1, V7X_LANES), 1) < RWKV_HEAD_DIM

    ycols = []
    for q in range(width // V7X_LANES):
        ls = slice(q * V7X_LANES, (q + 1) * V7X_LANES)
        kkq = kk[:, ls]
        kkq = kkq * lax.rsqrt(jnp.maximum(head_sum(kkq * kkq), 1e-24))
        bq = kkq * a_sig[:, ls]
        yrows = []
        for c in range(tb // L):
            rs = slice(c * L, (c + 1) * L)
            lw = logw[rs, ls]
            cum = _dot_lhs01(tri01, lw)
            g_in, g_prev, g_inv = jnp.exp(cum), jnp.exp(cum - lw), jnp.exp(-cum)
            g_last = g_in[L - 1:L, :]
            at = _stack_heads(-kkq[rs] * g_prev, lane_lo)
            rt = _stack_heads(r[rs, ls] * g_in, lane_lo)
            bt = _stack_heads(bq[rs] * g_inv, lane_lo)
            kt = _stack_heads(k[rs, ls] * g_inv, lane_lo)
            vs = _stack_heads(v[rs, ls], lane_lo)
            ar = jnp.concatenate([at, rt], axis=0)
            bk = jnp.concatenate([bt, kt], axis=0)
            p = _bdot_nt(ar, bk)
            a_ab = jnp.where(strict, p[:P2, :P2], 0.0)
            a_ak = jnp.where(strict, p[:P2, P2:], 0.0)
            a_rb = jnp.where(incl, p[P2:, :P2], 0.0)
            a_rk = jnp.where(incl, p[P2:, P2:], 0.0)
            tinv = _unit_lower_inverse(a_ab, same_blk, eye)
            ht = state_ref[q]
            arh = _bdot_nt(ar, ht)
            u = _bdot(tinv, arh[:P2] + _bdot(a_ak, vs))
            ys = arh[P2:] + _bdot(a_rb, u) + _bdot(a_rk, vs)
            state_ref[q] = ht * g_last + _bdot_tn(jnp.concatenate([u, vs], axis=0),
                                                   jnp.concatenate([bt, kt], axis=0) * g_last)
            yrows.append(ys[:L] + ys[L:])
        yq = jnp.concatenate(yrows, axis=0) if len(yrows) > 1 else yrows[0]
        inv_n = 1.0 / RWKV_HEAD_DIM
        mean = head_sum(yq) * inv_n
        yc = yq - mean
        var = head_sum(yc * yc) * inv_n
        yn = yc * lax.rsqrt(var + RWKV_GN_EPS) * lng_ref[:, ls] + lnb_ref[:, ls]
        bonus = head_sum(r[:, ls] * k[:, ls] * rk_ref[:, ls]) * v[:, ls]
        ycols.append((yn + bonus) * g[:, ls])
    o_ref[...] = jnp.concatenate(ycols, axis=1).astype(o_ref.dtype)


def _rwkv_mixer(proj, blk_rkv, blk_lo, bsz, seq, mu_rkv, mu_lo, w0, w_up, a0, a_up, g_up,
                k_k, k_a, r_k, ln_g, ln_b, *, tb=64):
    width = w0.shape[0]
    nl = w_up.shape[0]
    tb = min(tb, seq)
    assert seq % tb == 0 and tb % RWKV_CHUNK == 0
    nt = seq // tb
    f = lambda a: a.astype(F32).reshape(1, -1)
    lane = jnp.arange(V7X_LANES) // RWKV_HEAD_DIM
    ones_bd = (lane[:, None] == lane[None, :]).astype(BF16)
    const = lambda shape: pl.BlockSpec(shape, lambda b, t: (0,) * len(shape))
    return pl.pallas_call(
        _rwkv_kernel,
        grid=(bsz, nt),
        in_specs=[pl.BlockSpec((tb, 3 * width), lambda b, t: (b * nt + t, blk_rkv)),
                  pl.BlockSpec((tb, 4 * nl), lambda b, t: (b * nt + t, blk_lo)),
                  const((1, 3 * width)), const((1, 4 * nl)),
                  const((1, width)), const((nl, width)), const((1, width)), const((nl, width)),
                  const((2 * nl, width)),
                  const((1, width)), const((1, width)), const((1, width)), const((1, width)),
                  const((1, width)), const((V7X_LANES, V7X_LANES))],
        out_specs=pl.BlockSpec((tb, width), lambda b, t: (b * nt + t, 0)),
        out_shape=jax.ShapeDtypeStruct((bsz * seq, width), BF16),
        scratch_shapes=[pltpu.VMEM((V7X_SUBLANES, 3 * width), F32),
                        pltpu.VMEM((V7X_SUBLANES, 4 * nl), F32),
                        pltpu.VMEM((width // V7X_LANES, V7X_LANES, V7X_LANES), F32)],
        compiler_params=_cparams("parallel", "arbitrary"),
        name="rwkv_mixer",
    )(proj, proj, mu_rkv, mu_lo, f(w0), w_up.astype(BF16), f(a0), a_up.astype(BF16),
      g_up.astype(BF16), f(k_k), f(k_a), f(r_k), f(ln_g), f(ln_b), ones_bd)


def _lru_kernel(xl_ref, gl_ref, cw_ref, cb_ref, wa_ref, ba_ref, wx_ref, bx_ref, lam_ref, o_ref,
                tail, carry, *, nlev):
    @pl.when(pl.program_id(1) == 0)
    def _():
        tail[...] = jnp.zeros_like(tail)
        carry[...] = jnp.zeros_like(carry)

    tt, width = xl_ref.shape
    row = _row_iota((tt, width))
    xc = _causal_conv(xl_ref[...], tail, cw_ref, cb_ref)
    gr, gi = [], []
    for j in range(width // V7X_LANES):
        xj = xc[:, j * V7X_LANES:(j + 1) * V7X_LANES]
        gr.append(_bdot(xj, wa_ref[j]))
        gi.append(_bdot(xj, wx_ref[j]))
    gate_r = _sigmoid(jnp.concatenate(gr, axis=1) + ba_ref[...])
    gate_i = _sigmoid(jnp.concatenate(gi, axis=1) + bx_ref[...])
    log_a = -LRU_C * gate_r * _softplus(-lam_ref[...])
    a = jnp.exp(log_a)
    mult = jnp.sqrt(jnp.maximum(-(jnp.tanh(log_a) * (a * a + 1.0)), 0.0))
    mult = jnp.where(row + pl.program_id(1) * tt == 0, 1.0, mult)
    b = xc * gate_i * mult
    for lv in range(nlev):
        s = 1 << lv
        keep = row >= s
        sa, sb = pltpu.roll(a, s, 0), pltpu.roll(b, s, 0)
        b = b + jnp.where(keep, a * sb, 0.0)
        a = jnp.where(keep, a * sa, a)
    h = b + a * carry[0:1, :]
    carry[0:1, :] = h[tt - 1:tt, :]
    o_ref[...] = (h * _gelu(gl_ref[...])).astype(o_ref.dtype)


def _lru_mixer(proj, blk_xl, blk_gl, bsz, seq, conv_w, conv_b, w_a, b_a, w_x, b_x, lam, *, tt=256):
    nb, blk, _ = w_a.shape
    width = nb * blk
    tt = min(tt, seq)
    nlev = int(math.log2(tt))
    assert 1 << nlev == tt and seq % tt == 0
    nt = seq // tt
    per = V7X_LANES // blk
    eye = jnp.eye(per, dtype=F32)
    bd = lambda w: (w.astype(F32).reshape(nb // per, per, blk, 1, blk) * eye[None, :, None, :, None]
                    ).reshape(nb // per, V7X_LANES, V7X_LANES).astype(BF16)
    f = lambda a: a.astype(F32).reshape(1, width)
    const = lambda shape: pl.BlockSpec(shape, lambda b, t: (0,) * len(shape))
    return pl.pallas_call(
        functools.partial(_lru_kernel, nlev=nlev),
        grid=(bsz, nt),
        in_specs=[pl.BlockSpec((tt, width), lambda b, t: (b * nt + t, blk_xl)),
                  pl.BlockSpec((tt, width), lambda b, t: (b * nt + t, blk_gl)),
                  const((CONV_K, width)), const((1, width)),
                  const((nb // per, V7X_LANES, V7X_LANES)), const((1, width)),
                  const((nb // per, V7X_LANES, V7X_LANES)), const((1, width)), const((1, width))],
        out_specs=pl.BlockSpec((tt, width), lambda b, t: (b * nt + t, 0)),
        out_shape=jax.ShapeDtypeStruct((bsz * seq, width), BF16),
        scratch_shapes=[pltpu.VMEM((V7X_SUBLANES, width), F32), pltpu.VMEM((V7X_SUBLANES, width), F32)],
        compiler_params=_cparams("parallel", "arbitrary"),
        name="lru_mixer",
    )(proj, proj, conv_w.astype(F32), f(conv_b), bd(w_a), f(b_a), bd(w_x), f(b_x), f(lam))


def _pad_cols(w, n):
    return jnp.pad(w, ((0, 0), (0, n - w.shape[1])))


def _even_mixer(h, bsz, seq, norm_g, in_proj, out_proj, lam_re, lam_im, log_step, b_re, b_im, c_re, c_im,
                s5_d, glu_w, glu_b, conv_w, conv_b, dt_bias, a_log, ssd_d, ssd_norm):
    s5w = lam_re.shape[0] * S5_GROUP
    ssdw = SSD_HEADS * SSD_HEAD_DIM
    gstate2 = 2 * SSD_GROUPS * SSD_STATE
    u_w, z_w, xbc_w, dt_w = jnp.split(in_proj, [s5w, s5w + ssdw, s5w + 2 * ssdw + gstate2], axis=1)
    w_all = jnp.concatenate([z_w, xbc_w, u_w, dt_w], axis=1)
    w_all = _pad_cols(w_all, -(-w_all.shape[1] // 512) * 512).astype(BF16)
    proj = _norm_matmul(h, norm_g, w_all, tn=512)
    off_bc, off_u, off_dt = 2 * ssdw, 2 * ssdw + gstate2, 2 * ssdw + gstate2 + s5w
    y_a = _s5_mixer(proj, off_u // s5w, bsz, seq, lam_re, lam_im, log_step, b_re, b_im, c_re, c_im,
                    s5_d, glu_w, glu_b)
    y_b = _ssd_mixer(proj, 0, 1, off_bc // gstate2, off_dt // V7X_LANES, bsz, seq,
                     conv_w, conv_b, dt_bias, a_log, ssd_d, ssd_norm)
    y = jnp.concatenate([y_a, y_b], axis=1)
    return _matmul_residual(y, out_proj.astype(BF16), h)


def _odd_mixer(h, bsz, seq, norm_g, in_proj, out_proj, mu, w0, w_up, a0, a_up, g_up, k_k, k_a, r_k,
               ln_g, ln_b, conv_w, conv_b, w_a, b_a, w_x, b_x, lam):
    rw = w0.shape[0]
    nl = w_up.shape[0]
    ngl = g_up.shape[0]
    lw = lam.shape[0] * lam.shape[1]
    nlp = V7X_LANES
    assert nl <= nlp and ngl == 2 * nlp
    rkv_w, wl_w, al_w, gl_w, xl_w, g2_w = jnp.split(
        in_proj, [3 * rw, 3 * rw + nl, 3 * rw + 2 * nl, 3 * rw + 2 * nl + ngl, 3 * rw + 2 * nl + ngl + lw], axis=1)
    w_all = jnp.concatenate([rkv_w, xl_w, g2_w, _pad_cols(wl_w, nlp), _pad_cols(al_w, nlp), gl_w],
                            axis=1).astype(BF16)
    mu = mu.astype(F32)
    mu_rkv = mu[:3 * rw].reshape(1, -1)
    padv = lambda v: jnp.pad(v, (0, nlp - v.shape[0]))
    mu_lo = jnp.concatenate([padv(mu[3 * rw:3 * rw + nl]), padv(mu[3 * rw + nl:3 * rw + 2 * nl]),
                             mu[3 * rw + 2 * nl:]]).reshape(1, -1)
    padr = lambda w: jnp.pad(w, ((0, nlp - w.shape[0]), (0, 0)))
    proj = _norm_matmul(h, norm_g, w_all, tn=512)
    off_xl, off_gl, off_lo = 3 * rw, 3 * rw + lw, 3 * rw + 2 * lw
    y_c = _rwkv_mixer(proj, 0, off_lo // (4 * nlp), bsz, seq, mu_rkv, mu_lo, w0, padr(w_up), a0, padr(a_up),
                      g_up, k_k, k_a, r_k.reshape(-1), ln_g, ln_b)
    y_d = _lru_mixer(proj, off_xl // lw, off_gl // lw, bsz, seq, conv_w, conv_b, w_a, b_a, w_x, b_x, lam)
    y = jnp.concatenate([y_c, y_d], axis=1)
    return _matmul_residual(y, out_proj.astype(BF16), h)


def kernel(x, p, norm_mix, norm_ffn, norm_pl, mlp_w1, mlp_w2, pl_proj, pl_gate, e_in_proj, e_out_proj, s5_lam_re, s5_lam_im, s5_log_step, s5_b_re, s5_b_im, s5_c_re, s5_c_im, s5_d, s5_glu_w, s5_glu_b, ssd_conv_w, ssd_conv_b, ssd_dt_bias, ssd_a_log, ssd_d, ssd_norm, o_in_proj, o_out_proj, rwkv_mu, rwkv_w0, rwkv_w_up, rwkv_a0, rwkv_a_up, rwkv_g_up, rwkv_k_k, rwkv_k_a, rwkv_r_k, rwkv_ln_g, rwkv_ln_b, lru_conv_w, lru_conv_b, lru_w_a, lru_b_a, lru_w_x, lru_b_x, lru_lam, norm_final):
    bsz, seq, d = x.shape
    depth = p.shape[0]
    h = x.astype(F32).reshape(bsz * seq, d)
    for i in range(depth):
        j = i // 2
        if i % 2 == 0:
            h = _even_mixer(h, bsz, seq, norm_mix[i], e_in_proj[j], e_out_proj[j], s5_lam_re[j], s5_lam_im[j],
                            s5_log_step[j], s5_b_re[j], s5_b_im[j], s5_c_re[j], s5_c_im[j], s5_d[j],
                            s5_glu_w[j], s5_glu_b[j], ssd_conv_w[j], ssd_conv_b[j], ssd_dt_bias[j],
                            ssd_a_log[j], ssd_d[j], ssd_norm[j])
        else:
            h = _odd_mixer(h, bsz, seq, norm_mix[i], o_in_proj[j], o_out_proj[j], rwkv_mu[j], rwkv_w0[j],
                           rwkv_w_up[j], rwkv_a0[j], rwkv_a_up[j], rwkv_g_up[j], rwkv_k_k[j], rwkv_k_a[j],
                           rwkv_r_k[j], rwkv_ln_g[j], rwkv_ln_b[j], lru_conv_w[j], lru_conv_b[j],
                           lru_w_a[j], lru_b_a[j], lru_w_x[j], lru_b_x[j], lru_lam[j])
        mid = _norm_matmul(h, norm_ffn[i], mlp_w1[i].astype(BF16), act="relu2", out_dtype=BF16)
        h = _matmul_residual(mid, mlp_w2[i].astype(BF16), h)
        h = _gated_embed(h, norm_pl[i], pl_gate[i].astype(BF16), p[i].reshape(bsz * seq, -1),
                         pl_proj[i].astype(BF16))
    return _final_norm(h, norm_final).reshape(bsz, seq, d)
```

```python
import functools
import math

import jax
import jax.numpy as jnp
from jax import lax
from jax.experimental import pallas as pl
from jax.experimental.pallas import tpu as pltpu

F32 = jnp.float32
BF16 = jnp.bfloat16

V7X_LANES = 128
V7X_SUBLANES = 8
V7X_VMEM_BYTES = 64 * 1024 * 1024
VMEM_LIMIT_BYTES = V7X_VMEM_BYTES - 8 * 1024 * 1024

NORM_EPS = 1e-6
S5_GROUP = 16
S5_STATE = 64
SSD_HEAD_DIM = 64
SSD_HEADS = 24
SSD_GROUPS = 4
SSD_STATE = 128
SSD_CHUNK = 128
CONV_K = 4
RWKV_HEAD_DIM = 64
RWKV_GN_EPS = 64e-5
RWKV_CHUNK = 64
RWKV_INV_BLOCK = 16
LRU_BLOCK = 64
LRU_C = 8.0


def _cparams(*sem):
    return pltpu.CompilerParams(dimension_semantics=sem, vmem_limit_bytes=VMEM_LIMIT_BYTES)


def _bdot(a, b):
    return jnp.dot(a.astype(BF16), b.astype(BF16), preferred_element_type=F32)


def _bdot_nt(a, b):
    return lax.dot_general(a.astype(BF16), b.astype(BF16), (((1,), (1,)), ((), ())),
                           preferred_element_type=F32)


def _bdot_tn(a, b):
    return lax.dot_general(a.astype(BF16), b.astype(BF16), (((0,), (0,)), ((), ())),
                           preferred_element_type=F32)


def _split2(x):
    hi = x.astype(BF16)
    lo = (x - hi.astype(F32)).astype(BF16)
    return hi, lo


def _split3(x):
    x1 = x.astype(BF16)
    r1 = x - x1.astype(F32)
    x2 = r1.astype(BF16)
    x3 = (r1 - x2.astype(F32)).astype(BF16)
    return x1, x2, x3


def _dot_lhs01(m01, x):
    m = m01.astype(BF16)
    x1, x2, x3 = _split3(x)
    d = lambda v: jnp.dot(m, v, preferred_element_type=F32)
    return d(x1) + d(x2) + d(x3)


def _dot_rhs01(x, m01):
    m = m01.astype(BF16)
    x1, x2, x3 = _split3(x)
    d = lambda v: jnp.dot(v, m, preferred_element_type=F32)
    return d(x1) + d(x2) + d(x3)


def _softplus(x):
    return jnp.maximum(x, 0.0) + jnp.log1p(jnp.exp(-jnp.abs(x)))


def _sigmoid(x):
    return jax.nn.sigmoid(x)


def _gelu(x):
    return jax.nn.gelu(x, approximate=True)


def _silu(x):
    return x * _sigmoid(x)


def _row_iota(shape):
    return lax.broadcasted_iota(jnp.int32, shape, 0)


def _causal_conv(x, tail_ref, w_ref, b_ref):
    rows = x.shape[0]
    xe = jnp.concatenate([tail_ref[...], x], axis=0)
    acc = b_ref[...] + w_ref[CONV_K - 1:CONV_K, :] * x
    for k in range(CONV_K - 1):
        sh = pltpu.roll(xe, CONV_K - 1 - k, 0)
        acc = acc + w_ref[k:k + 1, :] * sh[V7X_SUBLANES:, :]
    tail_ref[...] = x[rows - V7X_SUBLANES:, :]
    return acc


def _rmsnorm_rows(h, g):
    ms = jnp.mean(h * h, axis=-1, keepdims=True)
    return h * lax.rsqrt(ms + NORM_EPS) * g


def _norm_mm_kernel(h_ref, g_ref, w_ref, o_ref, hn_ref, *, act):
    @pl.when(pl.program_id(1) == 0)
    def _():
        hn_ref[...] = _rmsnorm_rows(h_ref[...], g_ref[...]).astype(BF16)

    y = jnp.dot(hn_ref[...], w_ref[...], preferred_element_type=F32)
    if act == "relu2":
        y = jnp.square(jnp.maximum(y, 0.0))
    o_ref[...] = y.astype(o_ref.dtype)


def _col_tile(n, target):
    return max(t for t in range(V7X_LANES, max(target, V7X_LANES) + 1, V7X_LANES) if n % t == 0)


def _norm_matmul(h, g, w, *, act=None, out_dtype=F32, tm=1024, tn=1536):
    m, k = h.shape
    n = w.shape[1]
    tm, tn = min(tm, m), _col_tile(n, tn)
    assert m % tm == 0 and n % tn == 0, (m, n, tm, tn)
    return pl.pallas_call(
        functools.partial(_norm_mm_kernel, act=act),
        grid=(m // tm, n // tn),
        in_specs=[pl.BlockSpec((tm, k), lambda i, j: (i, 0)),
                  pl.BlockSpec((1, k), lambda i, j: (0, 0)),
                  pl.BlockSpec((k, tn), lambda i, j: (0, j))],
        out_specs=pl.BlockSpec((tm, tn), lambda i, j: (i, j)),
        out_shape=jax.ShapeDtypeStruct((m, n), out_dtype),
        scratch_shapes=[pltpu.VMEM((tm, k), BF16)],
        compiler_params=_cparams("parallel", "arbitrary"),
        name="norm_matmul",
    )(h, g.reshape(1, k), w)


def _out_proj_kernel(a1_ref, a2_ref, w1_ref, w2_ref, r_ref, o_ref):
    o_ref[...] = (r_ref[...] + jnp.dot(a1_ref[...], w1_ref[...], preferred_element_type=F32)
                  + jnp.dot(a2_ref[...], w2_ref[...], preferred_element_type=F32))


def _out_proj_residual(a1, a2, w, res, *, tm=1024, tn=1024):
    m, k1 = a1.shape
    k2 = a2.shape[1]
    n = w.shape[1]
    tm, tn = min(tm, m), min(tn, n)
    assert m % tm == 0 and n % tn == 0
    return pl.pallas_call(
        _out_proj_kernel,
        grid=(m // tm, n // tn),
        in_specs=[pl.BlockSpec((tm, k1), lambda i, j: (i, 0)),
                  pl.BlockSpec((tm, k2), lambda i, j: (i, 0)),
                  pl.BlockSpec((k1, tn), lambda i, j: (0, j)),
                  pl.BlockSpec((k2, tn), lambda i, j: (0, j)),
                  pl.BlockSpec((tm, tn), lambda i, j: (i, j))],
        out_specs=pl.BlockSpec((tm, tn), lambda i, j: (i, j)),
        out_shape=jax.ShapeDtypeStruct((m, n), F32),
        compiler_params=_cparams("parallel", "parallel"),
        name="out_proj",
    )(a1, a2, w[:k1], w[k1:], res)


def _mlp_kernel(h_ref, g_ref, w1_ref, w2_ref, o_ref, hn_ref, acc_ref, *, nf):
    j = pl.program_id(1)

    @pl.when(j == 0)
    def _():
        hn_ref[...] = _rmsnorm_rows(h_ref[...], g_ref[...]).astype(BF16)
        acc_ref[...] = jnp.zeros_like(acc_ref)

    mid = jnp.dot(hn_ref[...], w1_ref[...], preferred_element_type=F32)
    mid = jnp.square(jnp.maximum(mid, 0.0)).astype(BF16)
    acc_ref[...] += jnp.dot(mid, w2_ref[...], preferred_element_type=F32)

    @pl.when(j == nf - 1)
    def _():
        o_ref[...] = h_ref[...] + acc_ref[...]


def _mlp_residual(h, g, w1, w2, *, tm=512, tf=1024):
    m, d = h.shape
    f = w1.shape[1]
    tm, tf = min(tm, m), min(tf, f)
    assert m % tm == 0 and f % tf == 0
    nf = f // tf
    return pl.pallas_call(
        functools.partial(_mlp_kernel, nf=nf),
        grid=(m // tm, nf),
        in_specs=[pl.BlockSpec((tm, d), lambda i, j: (i, 0)),
                  pl.BlockSpec((1, d), lambda i, j: (0, 0)),
                  pl.BlockSpec((d, tf), lambda i, j: (0, j)),
                  pl.BlockSpec((tf, d), lambda i, j: (j, 0))],
        out_specs=pl.BlockSpec((tm, d), lambda i, j: (i, 0)),
        out_shape=jax.ShapeDtypeStruct((m, d), F32),
        scratch_shapes=[pltpu.VMEM((tm, d), BF16), pltpu.VMEM((tm, d), F32)],
        compiler_params=_cparams("parallel", "arbitrary"),
        name="mlp",
    )(h, g.reshape(1, d), w1, w2)


def _gate_kernel(h_ref, g_ref, wg_ref, p_ref, wp_ref, hres_ref, o_ref, hn_ref):
    @pl.when(pl.program_id(1) == 0)
    def _():
        hn_ref[...] = _rmsnorm_rows(h_ref[...], g_ref[...]).astype(BF16)

    gate = _sigmoid(jnp.dot(hn_ref[...], wg_ref[...], preferred_element_type=F32))
    pp = jnp.dot(p_ref[...].astype(BF16), wp_ref[...], preferred_element_type=F32)
    o_ref[...] = hres_ref[...] + gate * pp


def _gate_final_kernel(h_ref, g_ref, wg_ref, p_ref, wp_ref, gf_ref, o_ref):
    h = h_ref[...]
    hn = _rmsnorm_rows(h, g_ref[...]).astype(BF16)
    gate = _sigmoid(jnp.dot(hn, wg_ref[...], preferred_element_type=F32))
    pp = jnp.dot(p_ref[...].astype(BF16), wp_ref[...], preferred_element_type=F32)
    o_ref[...] = _rmsnorm_rows(h + gate * pp, gf_ref[...])


def _gated_embed_final(h, g, wg, p, wp, g_final, *, tm=512):
    m, k = h.shape
    kp = p.shape[1]
    tm = min(tm, m)
    assert m % tm == 0 and wg.shape == (k, k)
    const = lambda shape: pl.BlockSpec(shape, lambda i: (0, 0))
    return pl.pallas_call(
        _gate_final_kernel,
        grid=(m // tm,),
        in_specs=[pl.BlockSpec((tm, k), lambda i: (i, 0)), const((1, k)), const((k, k)),
                  pl.BlockSpec((tm, kp), lambda i: (i, 0)), const((kp, k)), const((1, k))],
        out_specs=pl.BlockSpec((tm, k), lambda i: (i, 0)),
        out_shape=jax.ShapeDtypeStruct((m, k), F32),
        compiler_params=_cparams("parallel"),
        name="gated_embed_final",
    )(h, g.reshape(1, k), wg, p, wp, g_final.reshape(1, k))


def _gated_embed(h, g, wg, p, wp, *, tm=1024, tn=1024):
    m, k = h.shape
    n = wg.shape[1]
    kp = p.shape[1]
    tm, tn = min(tm, m), min(tn, n)
    assert m % tm == 0 and n % tn == 0
    return pl.pallas_call(
        _gate_kernel,
        grid=(m // tm, n // tn),
        in_specs=[pl.BlockSpec((tm, k), lambda i, j: (i, 0)),
                  pl.BlockSpec((1, k), lambda i, j: (0, 0)),
                  pl.BlockSpec((k, tn), lambda i, j: (0, j)),
                  pl.BlockSpec((tm, kp), lambda i, j: (i, 0)),
                  pl.BlockSpec((kp, tn), lambda i, j: (0, j)),
                  pl.BlockSpec((tm, tn), lambda i, j: (i, j))],
        out_specs=pl.BlockSpec((tm, tn), lambda i, j: (i, j)),
        out_shape=jax.ShapeDtypeStruct((m, n), F32),
        scratch_shapes=[pltpu.VMEM((tm, k), BF16)],
        compiler_params=_cparams("parallel", "arbitrary"),
        name="gated_embed",
    )(h, g.reshape(1, k), wg, p, wp, h)


def _s5_prep_kernel(lr_ref, li_ref, st_ref, btr_ref, bti_ref,
                    bbd_re_ref, bbd_im_ref, apw_re_ref, apw_im_ref, acar_re_ref, acar_im_ref):
    lr, li, st = lr_ref[...], li_ref[...], jnp.exp(st_ref[...])
    lrs, lis = lr * st, li * st

    def apow(n):
        mag = jnp.exp(n * lrs)
        return mag * jnp.cos(n * lis), mag * jnp.sin(n * lis)

    mag = jnp.exp(lrs)
    abar_re, abar_im = mag * jnp.cos(lis), mag * jnp.sin(lis)
    den = lr * lr + li * li
    nr = abar_re - 1.0
    coef_re = (nr * lr + abar_im * li) / den
    coef_im = (abar_im * lr - nr * li) / den
    btr, bti = btr_ref[...], bti_ref[...]
    bbar_re = coef_re * btr - coef_im * bti
    bbar_im = coef_re * bti + coef_im * btr
    rows, cols = bbd_re_ref.shape
    reps = rows // S5_GROUP
    rg = lax.broadcasted_iota(jnp.int32, (rows, cols), 0) // S5_GROUP
    cg = lax.broadcasted_iota(jnp.int32, (rows, cols), 1) // S5_STATE
    same = rg == cg
    bbd_re_ref[...] = jnp.where(same, jnp.concatenate([bbar_re] * reps, axis=0), 0.0).astype(BF16)
    bbd_im_ref[...] = jnp.where(same, jnp.concatenate([bbar_im] * reps, axis=0), 0.0).astype(BF16)
    nlev = apw_re_ref.shape[0]
    lev = lax.broadcasted_iota(jnp.int32, (nlev, 1), 0)
    pw = jnp.left_shift(jnp.ones_like(lev), lev).astype(F32)
    apw_re_ref[...], apw_im_ref[...] = apow(pw)
    tt = acar_re_ref.shape[0]
    steps = (lax.broadcasted_iota(jnp.int32, (tt, 1), 0) + 1).astype(F32)
    acar_re_ref[...], acar_im_ref[...] = apow(steps)


def _s5_kernel(u_ref, bre_ref, bim_ref, cre_ref, cim_ref, apr_ref, api_ref, acr_ref, aci_ref,
               d_ref, gw_ref, gb_ref, o_ref, car_re, car_im, *, nlev, lane_blk):
    @pl.when(pl.program_id(1) == 0)
    def _():
        car_re[...] = jnp.zeros_like(car_re)
        car_im[...] = jnp.zeros_like(car_im)

    u = u_ref[...]
    tt, width = u.shape
    nstate = bre_ref.shape[1]
    row = _row_iota((tt, lane_blk))
    ch_blk = lane_blk * S5_GROUP // S5_STATE
    ys = []
    for j in range(nstate // lane_blk):
        ls = slice(j * lane_blk, (j + 1) * lane_blk)
        cs = slice(j * ch_blk, (j + 1) * ch_blk)
        uj = u[:, cs]
        xr = _bdot(uj, bre_ref[cs, ls])
        xi = _bdot(uj, bim_ref[cs, ls])
        for lv in range(nlev):
            s = 1 << lv
            pr, pi = apr_ref[lv:lv + 1, ls], api_ref[lv:lv + 1, ls]
            sr, si = pltpu.roll(xr, s, 0), pltpu.roll(xi, s, 0)
            keep = row >= s
            xr, xi = (xr + jnp.where(keep, pr * sr - pi * si, 0.0),
                      xi + jnp.where(keep, pr * si + pi * sr, 0.0))
        cr, ci = car_re[0:1, ls], car_im[0:1, ls]
        ar, ai = acr_ref[:, ls], aci_ref[:, ls]
        xr, xi = xr + (ar * cr - ai * ci), xi + (ar * ci + ai * cr)
        car_re[0:1, ls] = xr[tt - 1:tt, :]
        car_im[0:1, ls] = xi[tt - 1:tt, :]
        ys.append(_bdot(xr, cre_ref[ls, cs]) - _bdot(xi, cim_ref[ls, cs]))
    y = jnp.concatenate(ys, axis=1) + d_ref[...] * u
    act = _gelu(y)
    o_ref[...] = (act * _sigmoid(_bdot(act, gw_ref[...]) + gb_ref[...])).astype(o_ref.dtype)


def _s5_mixer(proj, col_blk, bsz, seq, lam_re, lam_im, log_step, b_re, b_im, c_re, c_im,
              d_skip, glu_w, glu_b, *, tt=256, lane_blk=512):
    ng, ns = lam_re.shape
    width = ng * S5_GROUP
    nstate = ng * ns
    tt = min(tt, seq)
    nlev = int(math.log2(tt))
    assert 1 << nlev == tt and seq % tt == 0
    nlev_pad = -(-nlev // V7X_SUBLANES) * V7X_SUBLANES
    flat = lambda a: a.astype(F32).reshape(1, nstate)
    step = jnp.broadcast_to(log_step.astype(F32)[:, None], (ng, ns))
    bt = lambda a: a.astype(F32).transpose(2, 0, 1).reshape(S5_GROUP, nstate)
    vm = pl.BlockSpec(memory_space=pltpu.VMEM)
    bbd_re, bbd_im, apw_re, apw_im, acar_re, acar_im = pl.pallas_call(
        _s5_prep_kernel,
        in_specs=[vm] * 5,
        out_specs=[vm] * 6,
        out_shape=[jax.ShapeDtypeStruct((width, nstate), BF16)] * 2
        + [jax.ShapeDtypeStruct((nlev_pad, nstate), F32)] * 2
        + [jax.ShapeDtypeStruct((tt, nstate), F32)] * 2,
        compiler_params=pltpu.CompilerParams(vmem_limit_bytes=VMEM_LIMIT_BYTES),
        name="s5_prep",
    )(flat(lam_re), flat(lam_im), flat(step), bt(b_re), bt(b_im))
    eye = jnp.eye(ng, dtype=F32)
    cbd = lambda c: (c.astype(F32).transpose(0, 2, 1)[:, :, None, :] * eye[:, None, :, None]
                     ).reshape(nstate, width).astype(BF16)
    nt = seq // tt
    const = lambda shape: pl.BlockSpec(shape, lambda b, t: (0, 0))
    return pl.pallas_call(
        functools.partial(_s5_kernel, nlev=nlev, lane_blk=lane_blk),
        grid=(bsz, nt),
        in_specs=[pl.BlockSpec((tt, width), lambda b, t: (b * nt + t, col_blk)),
                  const((width, nstate)), const((width, nstate)),
                  const((nstate, width)), const((nstate, width)),
                  const((nlev_pad, nstate)), const((nlev_pad, nstate)),
                  const((tt, nstate)), const((tt, nstate)),
                  const((1, width)), const((width, width)), const((1, width))],
        out_specs=pl.BlockSpec((tt, width), lambda b, t: (b * nt + t, 0)),
        out_shape=jax.ShapeDtypeStruct((bsz * seq, width), BF16),
        scratch_shapes=[pltpu.VMEM((V7X_SUBLANES, nstate), F32)] * 2,
        compiler_params=_cparams("parallel", "arbitrary"),
        name="s5_mixer",
    )(proj, bbd_re, bbd_im, cbd(c_re), cbd(c_im), apw_re, apw_im, acar_re, acar_im,
      d_skip.astype(F32).reshape(1, width), glu_w.astype(BF16), glu_b.astype(F32).reshape(1, width))


def _ssd_kernel(z_ref, xs_ref, bc_ref, dt_ref, cwx_ref, cbx_ref, cwb_ref, cbb_ref, e64_ref,
                dtb_ref, alog_ref, dtbx_ref, alogx_ref, dskx_ref, ng_ref, o_ref,
                tailx, tailb, st_ref):
    @pl.when(pl.program_id(1) == 0)
    def _():
        tailx[...] = jnp.zeros_like(tailx)
        tailb[...] = jnp.zeros_like(tailb)
        st_ref[...] = jnp.zeros_like(st_ref)

    L = SSD_CHUNK
    tt = xs_ref.shape[0]
    width = xs_ref.shape[1]
    gstate = SSD_GROUPS * SSD_STATE
    xs = _silu(_causal_conv(xs_ref[...], tailx, cwx_ref, cbx_ref))
    bc = _silu(_causal_conv(bc_ref[...], tailb, cwb_ref, cbb_ref))
    dt_raw = dt_ref[...]
    da = _softplus(dt_raw + dtb_ref[...]) * (-jnp.exp(alog_ref[...]))
    dtx = _softplus(_dot_rhs01(dt_raw, e64_ref[...]) + dtbx_ref[...])
    dax = dtx * (-jnp.exp(alogx_ref[...]))
    xdt = xs * dtx
    ri = lax.broadcasted_iota(jnp.int32, (L, L), 0)
    ci = lax.broadcasted_iota(jnp.int32, (L, L), 1)
    tril = ri >= ci
    tri01 = tril.astype(BF16)
    lane_lo = lax.broadcasted_iota(jnp.int32, (1, V7X_LANES), 1) < SSD_HEAD_DIM
    pairs_per_group = SSD_HEADS // SSD_GROUPS // 2
    ys = []
    for c in range(tt // L):
        rs = slice(c * L, (c + 1) * L)
        acum = _dot_lhs01(tri01, da[rs, :])
        acum_t = acum.T
        acx = _dot_lhs01(tri01, dax[rs, :])
        ycols = []
        for g in range(SSD_GROUPS):
            bg = bc[rs, g * SSD_STATE:(g + 1) * SSD_STATE]
            cg = bc[rs, gstate + g * SSD_STATE:gstate + (g + 1) * SSD_STATE]
            scores = _bdot_nt(cg, bg)
            bg_t = bg.T
            for pr in range(pairs_per_group):
                q = g * pairs_per_group + pr
                ls = slice(q * V7X_LANES, (q + 1) * V7X_LANES)
                xp = xdt[rs, ls]
                acp = acx[:, ls]
                yd = []
                for h in (2 * q, 2 * q + 1):
                    seg = acum[:, h:h + 1] - acum_t[h:h + 1, :]
                    yd.append(_bdot(scores * jnp.exp(jnp.where(tril, seg, -jnp.inf)), xp))
                y = jnp.where(lane_lo, yd[0], yd[1])
                prev_t = st_ref[q]
                y = y + _bdot(cg, prev_t) * jnp.exp(acp)
                last = acp[L - 1:L, :]
                st_ref[q] = prev_t * jnp.exp(last) + _bdot(bg_t, xp * jnp.exp(last - acp))
                ycols.append(y)
        ys.append(jnp.concatenate(ycols, axis=1))
    y = jnp.concatenate(ys, axis=0) if len(ys) > 1 else ys[0]
    y = (y + xs * dskx_ref[...]) * _silu(z_ref[...])
    gw = width // SSD_GROUPS
    outs = []
    for g in range(SSD_GROUPS):
        yg = y[:, g * gw:(g + 1) * gw]
        outs.append(yg * lax.rsqrt(jnp.mean(yg * yg, axis=-1, keepdims=True) + NORM_EPS))
    o_ref[...] = (jnp.concatenate(outs, axis=1) * ng_ref[...]).astype(o_ref.dtype)


def _ssd_mixer(proj, blk_z, blk_xs, blk_bc, blk_dt, bsz, seq, conv_w, conv_b, dt_bias, a_log,
               d_skip, norm_g, *, tt=256):
    width = SSD_HEADS * SSD_HEAD_DIM
    gstate2 = 2 * SSD_GROUPS * SSD_STATE
    tt = min(tt, seq)
    assert seq % tt == 0 and tt % SSD_CHUNK == 0
    nt = seq // tt
    f = lambda a: a.astype(F32)
    pad_h = lambda a: jnp.pad(f(a), (0, V7X_LANES - SSD_HEADS)).reshape(1, V7X_LANES)
    exp_h = lambda a: jnp.repeat(f(a), SSD_HEAD_DIM).reshape(1, width)
    e64 = (jnp.arange(V7X_LANES)[:, None] == (jnp.arange(width)[None, :] // SSD_HEAD_DIM)).astype(BF16)
    const = lambda shape: pl.BlockSpec(shape, lambda b, t: (0,) * len(shape))
    rowblk = lambda w, cb: pl.BlockSpec((tt, w), lambda b, t: (b * nt + t, cb))
    return pl.pallas_call(
        _ssd_kernel,
        grid=(bsz, nt),
        in_specs=[rowblk(width, blk_z), rowblk(width, blk_xs), rowblk(gstate2, blk_bc),
                  rowblk(V7X_LANES, blk_dt),
                  const((CONV_K, width)), const((1, width)), const((CONV_K, gstate2)), const((1, gstate2)),
                  const((V7X_LANES, width)),
                  const((1, V7X_LANES)), const((1, V7X_LANES)),
                  const((1, width)), const((1, width)), const((1, width)), const((1, width))],
        out_specs=pl.BlockSpec((tt, width), lambda b, t: (b * nt + t, 0)),
        out_shape=jax.ShapeDtypeStruct((bsz * seq, width), BF16),
        scratch_shapes=[pltpu.VMEM((V7X_SUBLANES, width), F32),
                        pltpu.VMEM((V7X_SUBLANES, gstate2), F32),
                        pltpu.VMEM((SSD_HEADS // 2, SSD_STATE, V7X_LANES), F32)],
        compiler_params=_cparams("parallel", "arbitrary"),
        name="ssd_mixer",
    )(proj, proj, proj, proj,
      f(conv_w[:, :width]), f(conv_b[:width]).reshape(1, width),
      f(conv_w[:, width:]), f(conv_b[width:]).reshape(1, gstate2),
      e64, pad_h(dt_bias), pad_h(a_log), exp_h(dt_bias), exp_h(a_log), exp_h(d_skip),
      f(norm_g).reshape(1, width))


def _stack_heads(x, lane_lo):
    return jnp.concatenate([jnp.where(lane_lo, x, 0.0), jnp.where(lane_lo, 0.0, x)], axis=0)


def _unit_lower_inverses(mats, same_blk, eye):
    ad = [jnp.where(same_blk, a, 0.0) for a in mats]
    ao = [a - d for a, d in zip(mats, ad)]
    a2 = [_bdot(d, d) for d in ad]
    a4 = [_bdot(x, x) for x in a2]
    a8 = [_bdot(x, x) for x in a4]
    t1 = [_bdot(eye + d, eye + x) for d, x in zip(ad, a2)]
    t2 = [_bdot(eye + x, eye + y) for x, y in zip(a4, a8)]
    td = [_bdot(x, y) for x, y in zip(t1, t2)]
    n = [_bdot(t, o) for t, o in zip(td, ao)]
    n2 = [_bdot(x, x) for x in n]
    tn = [_bdot(eye + x, eye + y) for x, y in zip(n, n2)]
    return [_bdot(x, t) for x, t in zip(tn, td)]


def _rwkv_kernel(rkv_ref, lo_ref, mu_rkv_ref, mu_lo_ref, w0_ref, wup_ref, a0_ref, aup_ref, gup_ref,
                 kk_ref, ka_ref, rk_ref, lng_ref, lnb_ref, ones_ref, o_ref,
                 prev_rkv, prev_lo, state_ref):
    @pl.when(pl.program_id(1) == 0)
    def _():
        prev_rkv[...] = jnp.zeros_like(prev_rkv)
        prev_lo[...] = jnp.zeros_like(prev_lo)
        state_ref[...] = jnp.zeros_like(state_ref)

    L = RWKV_CHUNK
    tb = rkv_ref.shape[0]
    width = rkv_ref.shape[1] // 3

    def shift_mix(ref, prev, mu_ref):
        f = ref[...]
        fs = jnp.where(_row_iota(f.shape) == 0, prev[0:1, :], pltpu.roll(f, 1, 0))
        prev[0:1, :] = f[tb - 1:tb, :]
        return f + (fs - f) * mu_ref[...]

    rkv = shift_mix(rkv_ref, prev_rkv, mu_rkv_ref)
    lo = shift_mix(lo_ref, prev_lo, mu_lo_ref)
    r, k, v = rkv[:, :width], rkv[:, width:2 * width], rkv[:, 2 * width:]
    nl = wup_ref.shape[0]
    wl, al, gl = lo[:, :nl], lo[:, nl:2 * nl], lo[:, 2 * nl:]
    w = -_softplus(-(w0_ref[...] + _bdot(jnp.tanh(wl), wup_ref[...]))) - 0.5
    logw = -jnp.exp(w)
    a_sig = _sigmoid(a0_ref[...] + _bdot(al, aup_ref[...]))
    g = _bdot(_sigmoid(gl), gup_ref[...])
    kk = k * kk_ref[...]
    k = k * (1.0 + (a_sig - 1.0) * ka_ref[...])
    ones_bd = ones_ref[...]

    def head_sum(x):
        hi, lo_ = _split2(x)
        return (jnp.dot(hi, ones_bd, preferred_element_type=F32)
                + jnp.dot(lo_, ones_bd, preferred_element_type=F32))

    P2 = 2 * L
    ri = lax.broadcasted_iota(jnp.int32, (P2, P2), 0)
    ci = lax.broadcasted_iota(jnp.int32, (P2, P2), 1)
    same_head = (ri // L) == (ci // L)
    strict = same_head & (ri > ci)
    incl = same_head & (ri >= ci)
    same_blk = (ri // RWKV_INV_BLOCK) == (ci // RWKV_INV_BLOCK)
    eye = (ri == ci).astype(F32)
    tri01 = (lax.broadcasted_iota(jnp.int32, (L, L), 0) >= lax.broadcasted_iota(jnp.int32, (L, L), 1)).astype(BF16)
    lane_lo = lax.broadcasted_iota(jnp.int32, (1, V7X_LANES), 1) < RWKV_HEAD_DIM

    nq = width // V7X_LANES
    lss = [slice(q * V7X_LANES, (q + 1) * V7X_LANES) for q in range(nq)]
    kks = [kk[:, ls] for ls in lss]
    kkn = [x * lax.rsqrt(jnp.maximum(head_sum(x * x), 1e-24)) for x in kks]
    yrows = []
    for c in range(tb // L):
        rs = slice(c * L, (c + 1) * L)
        cum = _dot_lhs01(tri01, logw[rs, :])
        g_in, g_prev, g_inv = jnp.exp(cum), jnp.exp(cum - logw[rs, :]), jnp.exp(-cum)
        ats = [_stack_heads(-kkn[q][rs] * g_prev[:, ls], lane_lo) for q, ls in enumerate(lss)]
        rts = [_stack_heads(r[rs, ls] * g_in[:, ls], lane_lo) for ls in lss]
        bts = [_stack_heads(kkn[q][rs] * a_sig[rs, ls] * g_inv[:, ls], lane_lo) for q, ls in enumerate(lss)]
        kts = [_stack_heads(k[rs, ls] * g_inv[:, ls], lane_lo) for ls in lss]
        vss = [_stack_heads(v[rs, ls], lane_lo) for ls in lss]
        ars = [jnp.concatenate([a, b], axis=0).astype(BF16) for a, b in zip(ats, rts)]
        ps = [_bdot_nt(ar, jnp.concatenate([b, kq], axis=0)) for ar, b, kq in zip(ars, bts, kts)]
        tinv = _unit_lower_inverses([jnp.where(strict, p[:P2, :P2], 0.0) for p in ps], same_blk, eye)
        akv = [_bdot(jnp.where(strict, p[:P2, P2:], 0.0), vs) for p, vs in zip(ps, vss)]
        a_r = [jnp.concatenate([jnp.where(incl, p[P2:, :P2], 0.0), jnp.where(incl, p[P2:, P2:], 0.0)],
                               axis=1).astype(BF16) for p in ps]
        bkl = [(jnp.concatenate([b, kq], axis=0) * g_in[L - 1:L, ls]).astype(BF16)
               for b, kq, ls in zip(bts, kts, lss)]
        hts = [state_ref[q] for q in range(nq)]
        arh = [_bdot_nt(ar, ht) for ar, ht in zip(ars, hts)]
        us = [_bdot(t, x[:P2] + y) for t, x, y in zip(tinv, arh, akv)]
        uv = [jnp.concatenate([u, vs], axis=0).astype(BF16) for u, vs in zip(us, vss)]
        for q in range(nq):
            state_ref[q] = hts[q] * g_in[L - 1:L, lss[q]] + _bdot_tn(uv[q], bkl[q])
        ys = [x[P2:] + jnp.dot(m, w, preferred_element_type=F32) for x, m, w in zip(arh, a_r, uv)]
        yrows.append(jnp.concatenate([y[:L] + y[L:] for y in ys], axis=1))
    y = jnp.concatenate(yrows, axis=0) if len(yrows) > 1 else yrows[0]
    inv_n = 1.0 / RWKV_HEAD_DIM
    outs = []
    for q, ls in enumerate(lss):
        yq = y[:, ls]
        mean = head_sum(yq) * inv_n
        yc = yq - mean
        var = head_sum(yc * yc) * inv_n
        yn = yc * lax.rsqrt(var + RWKV_GN_EPS) * lng_ref[:, ls] + lnb_ref[:, ls]
        bonus = head_sum(r[:, ls] * k[:, ls] * rk_ref[:, ls]) * v[:, ls]
        outs.append((yn + bonus) * g[:, ls])
    o_ref[...] = jnp.concatenate(outs, axis=1).astype(o_ref.dtype)


def _rwkv_mixer(proj, blk_rkv, blk_lo, bsz, seq, mu_rkv, mu_lo, w0, w_up, a0, a_up, g_up,
                k_k, k_a, r_k, ln_g, ln_b, *, tb=64):
    width = w0.shape[0]
    nl = w_up.shape[0]
    tb = min(tb, seq)
    assert seq % tb == 0 and tb % RWKV_CHUNK == 0
    nt = seq // tb
    f = lambda a: a.astype(F32).reshape(1, -1)
    lane = jnp.arange(V7X_LANES) // RWKV_HEAD_DIM
    ones_bd = (lane[:, None] == lane[None, :]).astype(BF16)
    const = lambda shape: pl.BlockSpec(shape, lambda b, t: (0,) * len(shape))
    return pl.pallas_call(
        _rwkv_kernel,
        grid=(bsz, nt),
        in_specs=[pl.BlockSpec((tb, 3 * width), lambda b, t: (b * nt + t, blk_rkv)),
                  pl.BlockSpec((tb, 4 * nl), lambda b, t: (b * nt + t, blk_lo)),
                  const((1, 3 * width)), const((1, 4 * nl)),
                  const((1, width)), const((nl, width)), const((1, width)), const((nl, width)),
                  const((2 * nl, width)),
                  const((1, width)), const((1, width)), const((1, width)), const((1, width)),
                  const((1, width)), const((V7X_LANES, V7X_LANES))],
        out_specs=pl.BlockSpec((tb, width), lambda b, t: (b * nt + t, 0)),
        out_shape=jax.ShapeDtypeStruct((bsz * seq, width), BF16),
        scratch_shapes=[pltpu.VMEM((V7X_SUBLANES, 3 * width), F32),
                        pltpu.VMEM((V7X_SUBLANES, 4 * nl), F32),
                        pltpu.VMEM((width // V7X_LANES, V7X_LANES, V7X_LANES), F32)],
        compiler_params=_cparams("parallel", "arbitrary"),
        name="rwkv_mixer",
    )(proj, proj, mu_rkv, mu_lo, f(w0), w_up.astype(BF16), f(a0), a_up.astype(BF16),
      g_up.astype(BF16), f(k_k), f(k_a), f(r_k), f(ln_g), f(ln_b), ones_bd)


def _lru_kernel(xl_ref, gl_ref, cw_ref, cb_ref, wa_ref, ba_ref, wx_ref, bx_ref, lam_ref, o_ref,
                tail, carry, *, nlev):
    @pl.when(pl.program_id(1) == 0)
    def _():
        tail[...] = jnp.zeros_like(tail)
        carry[...] = jnp.zeros_like(carry)

    tt, width = xl_ref.shape
    row = _row_iota((tt, width))
    xc = _causal_conv(xl_ref[...], tail, cw_ref, cb_ref)
    gr, gi = [], []
    for j in range(width // V7X_LANES):
        xj = xc[:, j * V7X_LANES:(j + 1) * V7X_LANES]
        gr.append(_bdot(xj, wa_ref[j]))
        gi.append(_bdot(xj, wx_ref[j]))
    gate_r = _sigmoid(jnp.concatenate(gr, axis=1) + ba_ref[...])
    gate_i = _sigmoid(jnp.concatenate(gi, axis=1) + bx_ref[...])
    log_a = -LRU_C * gate_r * _softplus(-lam_ref[...])
    a = jnp.exp(log_a)
    mult = jnp.sqrt(jnp.maximum(-(jnp.tanh(log_a) * (a * a + 1.0)), 0.0))
    mult = jnp.where(row + pl.program_id(1) * tt == 0, 1.0, mult)
    b = xc * gate_i * mult
    for lv in range(nlev):
        s = 1 << lv
        keep = row >= s
        sa, sb = pltpu.roll(a, s, 0), pltpu.roll(b, s, 0)
        b = b + jnp.where(keep, a * sb, 0.0)
        a = jnp.where(keep, a * sa, a)
    h = b + a * carry[0:1, :]
    carry[0:1, :] = h[tt - 1:tt, :]
    o_ref[...] = (h * _gelu(gl_ref[...])).astype(o_ref.dtype)


def _lru_mixer(proj, blk_xl, blk_gl, bsz, seq, conv_w, conv_b, w_a, b_a, w_x, b_x, lam, *, tt=256):
    nb, blk, _ = w_a.shape
    width = nb * blk
    tt = min(tt, seq)
    nlev = int(math.log2(tt))
    assert 1 << nlev == tt and seq % tt == 0
    nt = seq // tt
    per = V7X_LANES // blk
    eye = jnp.eye(per, dtype=F32)
    bd = lambda w: (w.astype(F32).reshape(nb // per, per, blk, 1, blk) * eye[None, :, None, :, None]
                    ).reshape(nb // per, V7X_LANES, V7X_LANES).astype(BF16)
    f = lambda a: a.astype(F32).reshape(1, width)
    const = lambda shape: pl.BlockSpec(shape, lambda b, t: (0,) * len(shape))
    return pl.pallas_call(
        functools.partial(_lru_kernel, nlev=nlev),
        grid=(bsz, nt),
        in_specs=[pl.BlockSpec((tt, width), lambda b, t: (b * nt + t, blk_xl)),
                  pl.BlockSpec((tt, width), lambda b, t: (b * nt + t, blk_gl)),
                  const((CONV_K, width)), const((1, width)),
                  const((nb // per, V7X_LANES, V7X_LANES)), const((1, width)),
                  const((nb // per, V7X_LANES, V7X_LANES)), const((1, width)), const((1, width))],
        out_specs=pl.BlockSpec((tt, width), lambda b, t: (b * nt + t, 0)),
        out_shape=jax.ShapeDtypeStruct((bsz * seq, width), BF16),
        scratch_shapes=[pltpu.VMEM((V7X_SUBLANES, width), F32), pltpu.VMEM((V7X_SUBLANES, width), F32)],
        compiler_params=_cparams("parallel", "arbitrary"),
        name="lru_mixer",
    )(proj, proj, conv_w.astype(F32), f(conv_b), bd(w_a), f(b_a), bd(w_x), f(b_x), f(lam))


def _pad_cols(w, n):
    return jnp.pad(w, ((0, 0), (0, n - w.shape[1])))


def _even_mixer(h, bsz, seq, norm_g, in_proj, out_proj, lam_re, lam_im, log_step, b_re, b_im, c_re, c_im,
                s5_d, glu_w, glu_b, conv_w, conv_b, dt_bias, a_log, ssd_d, ssd_norm):
    s5w = lam_re.shape[0] * S5_GROUP
    ssdw = SSD_HEADS * SSD_HEAD_DIM
    gstate2 = 2 * SSD_GROUPS * SSD_STATE
    u_w, z_w, xbc_w, dt_w = jnp.split(in_proj, [s5w, s5w + ssdw, s5w + 2 * ssdw + gstate2], axis=1)
    w_all = jnp.concatenate([z_w, xbc_w, u_w, dt_w], axis=1)
    w_all = _pad_cols(w_all, -(-w_all.shape[1] // 512) * 512).astype(BF16)
    proj = _norm_matmul(h, norm_g, w_all)
    off_bc, off_u, off_dt = 2 * ssdw, 2 * ssdw + gstate2, 2 * ssdw + gstate2 + s5w
    y_a = _s5_mixer(proj, off_u // s5w, bsz, seq, lam_re, lam_im, log_step, b_re, b_im, c_re, c_im,
                    s5_d, glu_w, glu_b)
    y_b = _ssd_mixer(proj, 0, 1, off_bc // gstate2, off_dt // V7X_LANES, bsz, seq,
                     conv_w, conv_b, dt_bias, a_log, ssd_d, ssd_norm)
    return _out_proj_residual(y_a, y_b, out_proj.astype(BF16), h)


def _odd_mixer(h, bsz, seq, norm_g, in_proj, out_proj, mu, w0, w_up, a0, a_up, g_up, k_k, k_a, r_k,
               ln_g, ln_b, conv_w, conv_b, w_a, b_a, w_x, b_x, lam):
    rw = w0.shape[0]
    nl = w_up.shape[0]
    ngl = g_up.shape[0]
    lw = lam.shape[0] * lam.shape[1]
    nlp = V7X_LANES
    assert nl <= nlp and ngl == 2 * nlp
    rkv_w, wl_w, al_w, gl_w, xl_w, g2_w = jnp.split(
        in_proj, [3 * rw, 3 * rw + nl, 3 * rw + 2 * nl, 3 * rw + 2 * nl + ngl, 3 * rw + 2 * nl + ngl + lw], axis=1)
    w_all = jnp.concatenate([rkv_w, xl_w, g2_w, _pad_cols(wl_w, nlp), _pad_cols(al_w, nlp), gl_w],
                            axis=1).astype(BF16)
    mu = mu.astype(F32)
    mu_rkv = mu[:3 * rw].reshape(1, -1)
    padv = lambda v: jnp.pad(v, (0, nlp - v.shape[0]))
    mu_lo = jnp.concatenate([padv(mu[3 * rw:3 * rw + nl]), padv(mu[3 * rw + nl:3 * rw + 2 * nl]),
                             mu[3 * rw + 2 * nl:]]).reshape(1, -1)
    padr = lambda w: jnp.pad(w, ((0, nlp - w.shape[0]), (0, 0)))
    proj = _norm_matmul(h, norm_g, w_all)
    off_xl, off_gl, off_lo = 3 * rw, 3 * rw + lw, 3 * rw + 2 * lw
    y_c = _rwkv_mixer(proj, 0, off_lo // (4 * nlp), bsz, seq, mu_rkv, mu_lo, w0, padr(w_up), a0, padr(a_up),
                      g_up, k_k, k_a, r_k.reshape(-1), ln_g, ln_b)
    y_d = _lru_mixer(proj, off_xl // lw, off_gl // lw, bsz, seq, conv_w, conv_b, w_a, b_a, w_x, b_x, lam)
    return _out_proj_residual(y_c, y_d, out_proj.astype(BF16), h)


def kernel(x, p, norm_mix, norm_ffn, norm_pl, mlp_w1, mlp_w2, pl_proj, pl_gate, e_in_proj, e_out_proj, s5_lam_re, s5_lam_im, s5_log_step, s5_b_re, s5_b_im, s5_c_re, s5_c_im, s5_d, s5_glu_w, s5_glu_b, ssd_conv_w, ssd_conv_b, ssd_dt_bias, ssd_a_log, ssd_d, ssd_norm, o_in_proj, o_out_proj, rwkv_mu, rwkv_w0, rwkv_w_up, rwkv_a0, rwkv_a_up, rwkv_g_up, rwkv_k_k, rwkv_k_a, rwkv_r_k, rwkv_ln_g, rwkv_ln_b, lru_conv_w, lru_conv_b, lru_w_a, lru_b_a, lru_w_x, lru_b_x, lru_lam, norm_final):
    bsz, seq, d = x.shape
    depth = p.shape[0]
    h = x.astype(F32).reshape(bsz * seq, d)
    for i in range(depth):
        j = i // 2
        if i % 2 == 0:
            h = _even_mixer(h, bsz, seq, norm_mix[i], e_in_proj[j], e_out_proj[j], s5_lam_re[j], s5_lam_im[j],
                            s5_log_step[j], s5_b_re[j], s5_b_im[j], s5_c_re[j], s5_c_im[j], s5_d[j],
                            s5_glu_w[j], s5_glu_b[j], ssd_conv_w[j], ssd_conv_b[j], ssd_dt_bias[j],
                            ssd_a_log[j], ssd_d[j], ssd_norm[j])
        else:
            h = _odd_mixer(h, bsz, seq, norm_mix[i], o_in_proj[j], o_out_proj[j], rwkv_mu[j], rwkv_w0[j],
                           rwkv_w_up[j], rwkv_a0[j], rwkv_a_up[j], rwkv_g_up[j], rwkv_k_k[j], rwkv_k_a[j],
                           rwkv_r_k[j], rwkv_ln_g[j], rwkv_ln_b[j], lru_conv_w[j], lru_conv_b[j],
                           lru_w_a[j], lru_b_a[j], lru_w_x[j], lru_b_x[j], lru_lam[j])
        h = _mlp_residual(h, norm_ffn[i], mlp_w1[i].astype(BF16), mlp_w2[i].astype(BF16))
        gate_args = (h, norm_pl[i], pl_gate[i].astype(BF16), p[i].reshape(bsz * seq, -1),
                     pl_proj[i].astype(BF16))
        if i + 1 < depth:
            h = _gated_embed(*gate_args)
        else:
            h = _gated_embed_final(*gate_args, norm_final)
    return h.reshape(bsz, seq, d)
```

```python
import functools

import jax
import jax.numpy as jnp
from jax import lax
from jax.experimental import pallas as pl
from jax.experimental.pallas import tpu as pltpu

F32 = jnp.float32
BF16 = jnp.bfloat16

V7X_LANES = 128
V7X_SUBLANES = 8
V7X_VMEM_BYTES = 64 * 1024 * 1024
VMEM_LIMIT_BYTES = V7X_VMEM_BYTES - 8 * 1024 * 1024

NORM_EPS = 1e-6
S5_GROUP = 16
S5_STATE = 64
SSD_HEAD_DIM = 64
SSD_HEADS = 24
SSD_GROUPS = 4
SSD_STATE = 128
SSD_CHUNK = 128
CONV_K = 4
RWKV_HEAD_DIM = 64
RWKV_GN_EPS = 64e-5
RWKV_CHUNK = 64
RWKV_INV_BLOCK = 16
LRU_BLOCK = 64
LRU_C = 8.0


def _cparams(*sem):
    return pltpu.CompilerParams(dimension_semantics=sem, vmem_limit_bytes=VMEM_LIMIT_BYTES)


def _bdot(a, b):
    return jnp.dot(a.astype(BF16), b.astype(BF16), preferred_element_type=F32)


def _bdot_nt(a, b):
    return lax.dot_general(a.astype(BF16), b.astype(BF16), (((1,), (1,)), ((), ())),
                           preferred_element_type=F32)


def _bdot_tn(a, b):
    return lax.dot_general(a.astype(BF16), b.astype(BF16), (((0,), (0,)), ((), ())),
                           preferred_element_type=F32)


def _split3(x):
    x1 = x.astype(BF16)
    r1 = x - x1.astype(F32)
    x2 = r1.astype(BF16)
    x3 = (r1 - x2.astype(F32)).astype(BF16)
    return x1, x2, x3


def _dot_lhs01(m01, x):
    m = m01.astype(BF16)
    x1, x2, x3 = _split3(x)
    d = lambda v: jnp.dot(m, v, preferred_element_type=F32)
    return d(x1) + d(x2) + d(x3)


def _dot_rhs01(x, m01):
    m = m01.astype(BF16)
    x1, x2, x3 = _split3(x)
    d = lambda v: jnp.dot(v, m, preferred_element_type=F32)
    return d(x1) + d(x2) + d(x3)


def _softplus(x):
    return jnp.maximum(x, 0.0) + jnp.log1p(jnp.exp(-jnp.abs(x)))


def _sigmoid(x):
    return jax.nn.sigmoid(x)


def _gelu(x):
    return jax.nn.gelu(x, approximate=True)


def _silu(x):
    return x * _sigmoid(x)


def _row_iota(shape):
    return lax.broadcasted_iota(jnp.int32, shape, 0)


def _delayed(x, ext_ref, delays):
    rows = x.shape[0]
    ext_ref[V7X_SUBLANES:, :] = x
    out = [ext_ref[V7X_SUBLANES - d:V7X_SUBLANES - d + rows, :] for d in delays]
    ext_ref[:V7X_SUBLANES, :] = x[rows - V7X_SUBLANES:, :]
    return out


def _causal_conv(x, ext_ref, w_ref, b_ref):
    acc = b_ref[...] + w_ref[CONV_K - 1:CONV_K, :] * x
    for k, xd in enumerate(_delayed(x, ext_ref, range(CONV_K - 1, 0, -1))):
        acc = acc + w_ref[k:k + 1, :] * xd
    return acc


def _rmsnorm_rows(h, g):
    ms = jnp.mean(h * h, axis=-1, keepdims=True)
    return h * lax.rsqrt(ms + NORM_EPS) * g


def _norm_mm_kernel(h_ref, g_ref, w_ref, o_ref, hn_ref, *, act):
    @pl.when(pl.program_id(1) == 0)
    def _():
        hn_ref[...] = _rmsnorm_rows(h_ref[...], g_ref[...]).astype(BF16)

    y = jnp.dot(hn_ref[...], w_ref[...], preferred_element_type=F32)
    if act == "relu2":
        y = jnp.square(jnp.maximum(y, 0.0))
    o_ref[...] = y.astype(o_ref.dtype)


def _col_tile(n, target):
    return max(t for t in range(V7X_LANES, max(target, V7X_LANES) + 1, V7X_LANES) if n % t == 0)


def _norm_matmul(h, g, w, *, act=None, out_dtype=F32, tm=1024, tn=1536):
    m, k = h.shape
    n = w.shape[1]
    tm, tn = min(tm, m), _col_tile(n, tn)
    assert m % tm == 0 and n % tn == 0, (m, n, tm, tn)
    return pl.pallas_call(
        functools.partial(_norm_mm_kernel, act=act),
        grid=(m // tm, n // tn),
        in_specs=[pl.BlockSpec((tm, k), lambda i, j: (i, 0)),
                  pl.BlockSpec((1, k), lambda i, j: (0, 0)),
                  pl.BlockSpec((k, tn), lambda i, j: (0, j))],
        out_specs=pl.BlockSpec((tm, tn), lambda i, j: (i, j)),
        out_shape=jax.ShapeDtypeStruct((m, n), out_dtype),
        scratch_shapes=[pltpu.VMEM((tm, k), BF16)],
        compiler_params=_cparams("parallel", "arbitrary"),
        name="norm_matmul",
    )(h, g.reshape(1, k), w)


def _out_proj_kernel(a1_ref, a2_ref, w1_ref, w2_ref, r_ref, o_ref):
    o_ref[...] = (r_ref[...] + jnp.dot(a1_ref[...], w1_ref[...], preferred_element_type=F32)
                  + jnp.dot(a2_ref[...], w2_ref[...], preferred_element_type=F32))


def _out_proj_residual(a1, a2, w, res, *, tm=1024, tn=1024):
    m, k1 = a1.shape
    k2 = a2.shape[1]
    n = w.shape[1]
    tm, tn = min(tm, m), min(tn, n)
    assert m % tm == 0 and n % tn == 0
    return pl.pallas_call(
        _out_proj_kernel,
        grid=(m // tm, n // tn),
        in_specs=[pl.BlockSpec((tm, k1), lambda i, j: (i, 0)),
                  pl.BlockSpec((tm, k2), lambda i, j: (i, 0)),
                  pl.BlockSpec((k1, tn), lambda i, j: (0, j)),
                  pl.BlockSpec((k2, tn), lambda i, j: (0, j)),
                  pl.BlockSpec((tm, tn), lambda i, j: (i, j))],
        out_specs=pl.BlockSpec((tm, tn), lambda i, j: (i, j)),
        out_shape=jax.ShapeDtypeStruct((m, n), F32),
        compiler_params=_cparams("parallel", "parallel"),
        name="out_proj",
    )(a1, a2, w[:k1], w[k1:], res)


def _mlp_kernel(h_ref, g_ref, w1_ref, w2_ref, o_ref, hn_ref, acc_ref, *, nf):
    j = pl.program_id(1)

    @pl.when(j == 0)
    def _():
        hn_ref[...] = _rmsnorm_rows(h_ref[...], g_ref[...]).astype(BF16)
        acc_ref[...] = jnp.zeros_like(acc_ref)

    mid = jnp.dot(hn_ref[...], w1_ref[...], preferred_element_type=F32)
    mid = jnp.square(jnp.maximum(mid, 0.0)).astype(BF16)
    acc_ref[...] += jnp.dot(mid, w2_ref[...], preferred_element_type=F32)

    @pl.when(j == nf - 1)
    def _():
        o_ref[...] = h_ref[...] + acc_ref[...]


def _mlp_residual(h, g, w1, w2, *, tm=512, tf=1024):
    m, d = h.shape
    f = w1.shape[1]
    tm, tf = min(tm, m), min(tf, f)
    assert m % tm == 0 and f % tf == 0
    nf = f // tf
    return pl.pallas_call(
        functools.partial(_mlp_kernel, nf=nf),
        grid=(m // tm, nf),
        in_specs=[pl.BlockSpec((tm, d), lambda i, j: (i, 0)),
                  pl.BlockSpec((1, d), lambda i, j: (0, 0)),
                  pl.BlockSpec((d, tf), lambda i, j: (0, j)),
                  pl.BlockSpec((tf, d), lambda i, j: (j, 0))],
        out_specs=pl.BlockSpec((tm, d), lambda i, j: (i, 0)),
        out_shape=jax.ShapeDtypeStruct((m, d), F32),
        scratch_shapes=[pltpu.VMEM((tm, d), BF16), pltpu.VMEM((tm, d), F32)],
        compiler_params=_cparams("parallel", "arbitrary"),
        name="mlp",
    )(h, g.reshape(1, d), w1, w2)


def _gate_kernel(h_ref, g_ref, wg_ref, p_ref, wp_ref, hres_ref, o_ref, hn_ref):
    @pl.when(pl.program_id(1) == 0)
    def _():
        hn_ref[...] = _rmsnorm_rows(h_ref[...], g_ref[...]).astype(BF16)

    gate = _sigmoid(jnp.dot(hn_ref[...], wg_ref[...], preferred_element_type=F32))
    pp = jnp.dot(p_ref[...].astype(BF16), wp_ref[...], preferred_element_type=F32)
    o_ref[...] = hres_ref[...] + gate * pp


def _gate_final_kernel(h_ref, g_ref, wg_ref, p_ref, wp_ref, gf_ref, o_ref):
    h = h_ref[...]
    hn = _rmsnorm_rows(h, g_ref[...]).astype(BF16)
    gate = _sigmoid(jnp.dot(hn, wg_ref[...], preferred_element_type=F32))
    pp = jnp.dot(p_ref[...].astype(BF16), wp_ref[...], preferred_element_type=F32)
    o_ref[...] = _rmsnorm_rows(h + gate * pp, gf_ref[...])


def _gated_embed_final(h, g, wg, p, layer, wp, g_final, *, tm=512):
    m, k = h.shape
    kp = p.shape[2]
    tm = min(tm, m)
    assert m % tm == 0 and wg.shape == (k, k)
    const = lambda shape: pl.BlockSpec(shape, lambda i: (0, 0))
    return pl.pallas_call(
        _gate_final_kernel,
        grid=(m // tm,),
        in_specs=[pl.BlockSpec((tm, k), lambda i: (i, 0)), const((1, k)), const((k, k)),
                  pl.BlockSpec((None, tm, kp), lambda i: (layer, i, 0)), const((kp, k)), const((1, k))],
        out_specs=pl.BlockSpec((tm, k), lambda i: (i, 0)),
        out_shape=jax.ShapeDtypeStruct((m, k), F32),
        compiler_params=_cparams("parallel"),
        name="gated_embed_final",
    )(h, g.reshape(1, k), wg, p, wp, g_final.reshape(1, k))


def _gated_embed(h, g, wg, p, layer, wp, *, tm=1024, tn=1024):
    m, k = h.shape
    n = wg.shape[1]
    kp = p.shape[2]
    tm, tn = min(tm, m), min(tn, n)
    assert m % tm == 0 and n % tn == 0
    return pl.pallas_call(
        _gate_kernel,
        grid=(m // tm, n // tn),
        in_specs=[pl.BlockSpec((tm, k), lambda i, j: (i, 0)),
                  pl.BlockSpec((1, k), lambda i, j: (0, 0)),
                  pl.BlockSpec((k, tn), lambda i, j: (0, j)),
                  pl.BlockSpec((None, tm, kp), lambda i, j: (layer, i, 0)),
                  pl.BlockSpec((kp, tn), lambda i, j: (0, j)),
                  pl.BlockSpec((tm, tn), lambda i, j: (i, j))],
        out_specs=pl.BlockSpec((tm, tn), lambda i, j: (i, j)),
        out_shape=jax.ShapeDtypeStruct((m, n), F32),
        scratch_shapes=[pltpu.VMEM((tm, k), BF16)],
        compiler_params=_cparams("parallel", "arbitrary"),
        name="gated_embed",
    )(h, g.reshape(1, k), wg, p, wp, h)


def _s5_prep_kernel(lr_ref, li_ref, st_ref, btr_ref, bti_ref, b8_ref, a8_re_ref, a8_im_ref):
    lr, li, st = lr_ref[...], li_ref[...], jnp.exp(st_ref[...])
    lrs, lis = lr * st, li * st

    def apow(n):
        mag = jnp.exp(n * lrs)
        return mag * jnp.cos(n * lis), mag * jnp.sin(n * lis)

    mag = jnp.exp(lrs)
    abar_re, abar_im = mag * jnp.cos(lis), mag * jnp.sin(lis)
    den = lr * lr + li * li
    nr = abar_re - 1.0
    coef_re = (nr * lr + abar_im * li) / den
    coef_im = (abar_im * lr - nr * li) / den
    btr, bti = btr_ref[...], bti_ref[...]
    bbar_re = coef_re * btr - coef_im * bti
    bbar_im = coef_re * bti + coef_im * btr
    nblk, taps_ch, lanes2 = b8_ref.shape
    lane_blk = lanes2 // 2
    ch_blk = taps_ch // V7X_SUBLANES
    rg = lax.broadcasted_iota(jnp.int32, (ch_blk, lane_blk), 0) // S5_GROUP
    cg = lax.broadcasted_iota(jnp.int32, (ch_blk, lane_blk), 1) // S5_STATE
    same = rg == cg
    for s in range(V7X_SUBLANES):
        if s == 0:
            br, bi = bbar_re, bbar_im
        else:
            pr, pi = apow(float(s))
            br, bi = pr * bbar_re - pi * bbar_im, pr * bbar_im + pi * bbar_re
        for j in range(nblk):
            ls = slice(j * lane_blk, (j + 1) * lane_blk)
            rs = slice(s * ch_blk, (s + 1) * ch_blk)
            tile = lambda m: jnp.where(same, jnp.concatenate([m[:, ls]] * (ch_blk // S5_GROUP), axis=0), 0.0)
            b8_ref[j, rs, :lane_blk] = tile(br).astype(BF16)
            b8_ref[j, rs, lane_blk:] = tile(bi).astype(BF16)
    a8r, a8i = apow(float(V7X_SUBLANES))
    a8_re_ref[...] = jnp.broadcast_to(a8r, a8_re_ref.shape)
    a8_im_ref[...] = jnp.broadcast_to(a8i, a8_im_ref.shape)


def _s5_kernel(u_ref, b8_ref, cre_ref, cim_ref, a8r_ref, a8i_ref, d_ref, gw_ref, gb_ref, o_ref,
               ue_ref, car_re, car_im):
    @pl.when(pl.program_id(1) == 0)
    def _():
        ue_ref[:V7X_SUBLANES, :] = jnp.zeros((V7X_SUBLANES, ue_ref.shape[1]), F32)
        car_re[...] = jnp.zeros_like(car_re)
        car_im[...] = jnp.zeros_like(car_im)

    u = u_ref[...]
    tt = u.shape[0]
    nblk, taps_ch, lanes2 = b8_ref.shape
    lane_blk = lanes2 // 2
    ch_blk = taps_ch // V7X_SUBLANES
    taps = [u] + _delayed(u, ue_ref, range(1, V7X_SUBLANES))
    ys = []
    for j in range(nblk):
        ls = slice(j * lane_blk, (j + 1) * lane_blk)
        cs = slice(j * ch_blk, (j + 1) * ch_blk)
        u8 = jnp.concatenate([t[:, cs].astype(BF16) for t in taps], axis=1)
        z = jnp.dot(u8, b8_ref[j], preferred_element_type=F32)
        pr, pi = a8r_ref[:, ls], a8i_ref[:, ls]
        cr, ci = car_re[:, ls], car_im[:, ls]
        xr, xi = [], []
        for i in range(tt // V7X_SUBLANES):
            rs = slice(i * V7X_SUBLANES, (i + 1) * V7X_SUBLANES)
            cr, ci = (z[rs, :lane_blk] + (pr * cr - pi * ci), z[rs, lane_blk:] + (pr * ci + pi * cr))
            xr.append(cr)
            xi.append(ci)
        car_re[:, ls] = cr
        car_im[:, ls] = ci
        ys.append(_bdot(jnp.concatenate(xr, axis=0), cre_ref[j]) - _bdot(jnp.concatenate(xi, axis=0), cim_ref[j]))
    y = jnp.concatenate(ys, axis=1) + d_ref[...] * u
    act = _gelu(y)
    o_ref[...] = (act * _sigmoid(_bdot(act, gw_ref[...]) + gb_ref[...])).astype(o_ref.dtype)


def _s5_mixer(proj, col_blk, bsz, seq, lam_re, lam_im, log_step, b_re, b_im, c_re, c_im,
              d_skip, glu_w, glu_b, *, tt=256, lane_blk=512):
    ng, ns = lam_re.shape
    width = ng * S5_GROUP
    nstate = ng * ns
    tt = min(tt, seq)
    assert seq % tt == 0 and tt % V7X_SUBLANES == 0 and nstate % lane_blk == 0 and lane_blk % ns == 0
    nblk = nstate // lane_blk
    gpb = lane_blk // ns
    ch_blk = gpb * S5_GROUP
    flat = lambda a: a.astype(F32).reshape(1, nstate)
    step = jnp.broadcast_to(log_step.astype(F32)[:, None], (ng, ns))
    bt = lambda a: a.astype(F32).transpose(2, 0, 1).reshape(S5_GROUP, nstate)
    vm = pl.BlockSpec(memory_space=pltpu.VMEM)
    b8, a8_re, a8_im = pl.pallas_call(
        _s5_prep_kernel,
        in_specs=[vm] * 5,
        out_specs=[vm] * 3,
        out_shape=[jax.ShapeDtypeStruct((nblk, V7X_SUBLANES * ch_blk, 2 * lane_blk), BF16)]
        + [jax.ShapeDtypeStruct((V7X_SUBLANES, nstate), F32)] * 2,
        compiler_params=pltpu.CompilerParams(vmem_limit_bytes=VMEM_LIMIT_BYTES),
        name="s5_prep",
    )(flat(lam_re), flat(lam_im), flat(step), bt(b_re), bt(b_im))
    eye = jnp.eye(gpb, dtype=F32)
    cbd = lambda c: (c.astype(F32).reshape(nblk, gpb, S5_GROUP, ns).transpose(0, 1, 3, 2)[:, :, :, None, :]
                     * eye[None, :, None, :, None]).reshape(nblk, lane_blk, ch_blk).astype(BF16)
    nt = seq // tt
    const = lambda shape: pl.BlockSpec(shape, lambda b, t: (0,) * len(shape))
    return pl.pallas_call(
        _s5_kernel,
        grid=(bsz, nt),
        in_specs=[pl.BlockSpec((tt, width), lambda b, t: (b * nt + t, col_blk)),
                  const((nblk, V7X_SUBLANES * ch_blk, 2 * lane_blk)),
                  const((nblk, lane_blk, ch_blk)), const((nblk, lane_blk, ch_blk)),
                  const((V7X_SUBLANES, nstate)), const((V7X_SUBLANES, nstate)),
                  const((1, width)), const((width, width)), const((1, width))],
        out_specs=pl.BlockSpec((tt, width), lambda b, t: (b * nt + t, 0)),
        out_shape=jax.ShapeDtypeStruct((bsz * seq, width), BF16),
        scratch_shapes=[pltpu.VMEM((tt + V7X_SUBLANES, width), F32)]
        + [pltpu.VMEM((V7X_SUBLANES, nstate), F32)] * 2,
        compiler_params=_cparams("parallel", "arbitrary"),
        name="s5_mixer",
    )(proj, b8, cbd(c_re), cbd(c_im), a8_re, a8_im,
      d_skip.astype(F32).reshape(1, width), glu_w.astype(BF16), glu_b.astype(F32).reshape(1, width))


def _ssd_kernel(z_ref, xs_ref, bc_ref, dt_ref, cwx_ref, cbx_ref, cwb_ref, cbb_ref, e64_ref,
                dtb_ref, alog_ref, alogx_ref, dskx_ref, ng_ref, o_ref,
                tailx, tailb, st_ref):
    @pl.when(pl.program_id(1) == 0)
    def _():
        tailx[:V7X_SUBLANES, :] = jnp.zeros((V7X_SUBLANES, tailx.shape[1]), F32)
        tailb[:V7X_SUBLANES, :] = jnp.zeros((V7X_SUBLANES, tailb.shape[1]), F32)
        st_ref[...] = jnp.zeros_like(st_ref)

    L = SSD_CHUNK
    tt = xs_ref.shape[0]
    width = xs_ref.shape[1]
    gstate = SSD_GROUPS * SSD_STATE
    xs = _silu(_causal_conv(xs_ref[...], tailx, cwx_ref, cbx_ref))
    bc = _silu(_causal_conv(bc_ref[...], tailb, cwb_ref, cbb_ref))
    dt = _softplus(dt_ref[...] + dtb_ref[...])
    da = dt * (-jnp.exp(alog_ref[...]))
    dtx = _dot_rhs01(dt, e64_ref[...])
    dax = dtx * (-jnp.exp(alogx_ref[...]))
    xdt = xs * dtx
    ri = lax.broadcasted_iota(jnp.int32, (L, L), 0)
    ci = lax.broadcasted_iota(jnp.int32, (L, L), 1)
    tril = ri >= ci
    tri01 = tril.astype(BF16)
    lane_lo = lax.broadcasted_iota(jnp.int32, (1, V7X_LANES), 1) < SSD_HEAD_DIM
    pairs_per_group = SSD_HEADS // SSD_GROUPS // 2
    ys = []
    for c in range(tt // L):
        rs = slice(c * L, (c + 1) * L)
        acum = _dot_lhs01(tri01, da[rs, :])
        acum_t = acum.T
        acx = _dot_lhs01(tri01, dax[rs, :])
        ycols = []
        for g in range(SSD_GROUPS):
            bg = bc[rs, g * SSD_STATE:(g + 1) * SSD_STATE]
            cg = bc[rs, gstate + g * SSD_STATE:gstate + (g + 1) * SSD_STATE]
            scores = _bdot_nt(cg, bg)
            bg_t = bg.T
            for pr in range(pairs_per_group):
                q = g * pairs_per_group + pr
                ls = slice(q * V7X_LANES, (q + 1) * V7X_LANES)
                xp = xdt[rs, ls]
                acp = acx[:, ls]
                yd = []
                for h in (2 * q, 2 * q + 1):
                    seg = acum[:, h:h + 1] - acum_t[h:h + 1, :]
                    yd.append(_bdot(scores * jnp.exp(jnp.where(tril, seg, -jnp.inf)), xp))
                y = jnp.where(lane_lo, yd[0], yd[1])
                prev_t = st_ref[q]
                y = y + _bdot(cg, prev_t) * jnp.exp(acp)
                last = acp[L - 1:L, :]
                st_ref[q] = prev_t * jnp.exp(last) + _bdot(bg_t, xp * jnp.exp(last - acp))
                ycols.append(y)
        ys.append(jnp.concatenate(ycols, axis=1))
    y = jnp.concatenate(ys, axis=0) if len(ys) > 1 else ys[0]
    y = (y + xs * dskx_ref[...]) * _silu(z_ref[...])
    gw = width // SSD_GROUPS
    outs = []
    for g in range(SSD_GROUPS):
        yg = y[:, g * gw:(g + 1) * gw]
        outs.append(yg * lax.rsqrt(jnp.mean(yg * yg, axis=-1, keepdims=True) + NORM_EPS))
    o_ref[...] = (jnp.concatenate(outs, axis=1) * ng_ref[...]).astype(o_ref.dtype)


def _ssd_mixer(proj, blk_z, blk_xs, blk_bc, blk_dt, bsz, seq, conv_w, conv_b, dt_bias, a_log,
               d_skip, norm_g, *, tt=256):
    width = SSD_HEADS * SSD_HEAD_DIM
    gstate2 = 2 * SSD_GROUPS * SSD_STATE
    tt = min(tt, seq)
    assert seq % tt == 0 and tt % SSD_CHUNK == 0
    nt = seq // tt
    f = lambda a: a.astype(F32)
    pad_h = lambda a: jnp.pad(f(a), (0, V7X_LANES - SSD_HEADS)).reshape(1, V7X_LANES)
    exp_h = lambda a: jnp.repeat(f(a), SSD_HEAD_DIM).reshape(1, width)
    e64 = (jnp.arange(V7X_LANES)[:, None] == (jnp.arange(width)[None, :] // SSD_HEAD_DIM)).astype(BF16)
    const = lambda shape: pl.BlockSpec(shape, lambda b, t: (0,) * len(shape))
    rowblk = lambda w, cb: pl.BlockSpec((tt, w), lambda b, t: (b * nt + t, cb))
    return pl.pallas_call(
        _ssd_kernel,
        grid=(bsz, nt),
        in_specs=[rowblk(width, blk_z), rowblk(width, blk_xs), rowblk(gstate2, blk_bc),
                  rowblk(V7X_LANES, blk_dt),
                  const((CONV_K, width)), const((1, width)), const((CONV_K, gstate2)), const((1, gstate2)),
                  const((V7X_LANES, width)),
                  const((1, V7X_LANES)), const((1, V7X_LANES)),
                  const((1, width)), const((1, width)), const((1, width))],
        out_specs=pl.BlockSpec((tt, width), lambda b, t: (b * nt + t, 0)),
        out_shape=jax.ShapeDtypeStruct((bsz * seq, width), BF16),
        scratch_shapes=[pltpu.VMEM((tt + V7X_SUBLANES, width), F32),
                        pltpu.VMEM((tt + V7X_SUBLANES, gstate2), F32),
                        pltpu.VMEM((SSD_HEADS // 2, SSD_STATE, V7X_LANES), F32)],
        compiler_params=_cparams("parallel", "arbitrary"),
        name="ssd_mixer",
    )(proj, proj, proj, proj,
      f(conv_w[:, :width]), f(conv_b[:width]).reshape(1, width),
      f(conv_w[:, width:]), f(conv_b[width:]).reshape(1, gstate2),
      e64, pad_h(dt_bias), pad_h(a_log), exp_h(a_log), exp_h(d_skip),
      f(norm_g).reshape(1, width))


def _stack_heads(x, lane_lo):
    return jnp.concatenate([jnp.where(lane_lo, x, 0.0), jnp.where(lane_lo, 0.0, x)], axis=0)


def _unit_lower_inverses(mats, same_blk, eye):
    ad = [jnp.where(same_blk, a, 0.0) for a in mats]
    ao = [a - d for a, d in zip(mats, ad)]
    a2 = [_bdot(d, d) for d in ad]
    a4 = [_bdot(x, x) for x in a2]
    a8 = [_bdot(x, x) for x in a4]
    t1 = [_bdot(eye + d, eye + x) for d, x in zip(ad, a2)]
    t2 = [_bdot(eye + x, eye + y) for x, y in zip(a4, a8)]
    td = [_bdot(x, y) for x, y in zip(t1, t2)]
    n = [_bdot(t, o) for t, o in zip(td, ao)]
    n2 = [_bdot(x, x) for x in n]
    tn = [_bdot(eye + x, eye + y) for x, y in zip(n, n2)]
    return [_bdot(x, t) for x, t in zip(tn, td)]


def _rwkv_kernel(rkv_ref, lo_ref, mu_rkv_ref, mu_lo_ref, w0_ref, wup_ref, a0_ref, aup_ref, gup_ref,
                 kk_ref, ka_ref, rk_ref, lng_ref, lnb_ref, ones_ref, o_ref,
                 prev_rkv, prev_lo, state_ref):
    @pl.when(pl.program_id(1) == 0)
    def _():
        prev_rkv[...] = jnp.zeros_like(prev_rkv)
        prev_lo[...] = jnp.zeros_like(prev_lo)
        state_ref[...] = jnp.zeros_like(state_ref)

    L = RWKV_CHUNK
    tb = rkv_ref.shape[0]
    width = rkv_ref.shape[1] // 3

    def shift_mix(ref, prev, mu_ref):
        f = ref[...]
        fs = jnp.where(_row_iota(f.shape) == 0, prev[0:1, :], pltpu.roll(f, 1, 0))
        prev[0:1, :] = f[tb - 1:tb, :]
        return f + (fs - f) * mu_ref[...]

    rkv = shift_mix(rkv_ref, prev_rkv, mu_rkv_ref)
    lo = shift_mix(lo_ref, prev_lo, mu_lo_ref)
    r, k, v = rkv[:, :width], rkv[:, width:2 * width], rkv[:, 2 * width:]
    nl = wup_ref.shape[0]
    wl, al, gl = lo[:, :nl], lo[:, nl:2 * nl], lo[:, 2 * nl:]
    w = -_softplus(-(w0_ref[...] + _bdot(jnp.tanh(wl), wup_ref[...]))) - 0.5
    logw = -jnp.exp(w)
    a_sig = _sigmoid(a0_ref[...] + _bdot(al, aup_ref[...]))
    g = _bdot(_sigmoid(gl), gup_ref[...])
    kk = k * kk_ref[...]
    k = k * (1.0 + (a_sig - 1.0) * ka_ref[...])
    ones_bd = ones_ref[...]

    def head_sum(x):
        return jnp.dot(x.astype(BF16), ones_bd, preferred_element_type=F32)

    P2 = 2 * L
    ri = lax.broadcasted_iota(jnp.int32, (P2, P2), 0)
    ci = lax.broadcasted_iota(jnp.int32, (P2, P2), 1)
    same_head = (ri // L) == (ci // L)
    strict = same_head & (ri > ci)
    incl = same_head & (ri >= ci)
    same_blk = (ri // RWKV_INV_BLOCK) == (ci // RWKV_INV_BLOCK)
    eye = (ri == ci).astype(F32)
    tri01 = (lax.broadcasted_iota(jnp.int32, (L, L), 0) >= lax.broadcasted_iota(jnp.int32, (L, L), 1)).astype(BF16)
    lane_lo = lax.broadcasted_iota(jnp.int32, (1, V7X_LANES), 1) < RWKV_HEAD_DIM

    nq = width // V7X_LANES
    lss = [slice(q * V7X_LANES, (q + 1) * V7X_LANES) for q in range(nq)]
    kks = [kk[:, ls] for ls in lss]
    kkn = [x * lax.rsqrt(jnp.maximum(head_sum(x * x), 1e-24)) for x in kks]
    nc = tb // L
    inst = [(slice(c * L, (c + 1) * L), q, ls) for c in range(nc) for q, ls in enumerate(lss)]
    cums = [_dot_lhs01(tri01, logw[c * L:(c + 1) * L, :]) for c in range(nc)]
    g_in = jnp.concatenate([jnp.exp(cm) for cm in cums], axis=0)
    g_prev = jnp.concatenate([jnp.exp(cm - logw[c * L:(c + 1) * L, :]) for c, cm in enumerate(cums)], axis=0)
    g_inv = jnp.concatenate([jnp.exp(-cm) for cm in cums], axis=0)
    ats = [_stack_heads(-kkn[q][rs] * g_prev[rs, ls], lane_lo) for rs, q, ls in inst]
    rts = [_stack_heads(r[rs, ls] * g_in[rs, ls], lane_lo) for rs, q, ls in inst]
    bts = [_stack_heads(kkn[q][rs] * a_sig[rs, ls] * g_inv[rs, ls], lane_lo) for rs, q, ls in inst]
    kts = [_stack_heads(k[rs, ls] * g_inv[rs, ls], lane_lo) for rs, q, ls in inst]
    vss = [_stack_heads(v[rs, ls], lane_lo) for rs, q, ls in inst]
    ars = [jnp.concatenate([a, b], axis=0).astype(BF16) for a, b in zip(ats, rts)]
    ps = [_bdot_nt(ar, jnp.concatenate([b, kq], axis=0)) for ar, b, kq in zip(ars, bts, kts)]
    tinv = _unit_lower_inverses([jnp.where(strict, p[:P2, :P2], 0.0) for p in ps], same_blk, eye)
    akv = [_bdot(jnp.where(strict, p[:P2, P2:], 0.0), vs) for p, vs in zip(ps, vss)]
    a_r = [jnp.concatenate([jnp.where(incl, p[P2:, :P2], 0.0), jnp.where(incl, p[P2:, P2:], 0.0)],
                           axis=1).astype(BF16) for p in ps]
    g_last = [g_in[rs, ls][L - 1:L, :] for rs, q, ls in inst]
    bkl = [(jnp.concatenate([b, kq], axis=0) * gl_).astype(BF16) for b, kq, gl_ in zip(bts, kts, g_last)]
    hts = [state_ref[q] for q in range(nq)]
    yrows = []
    for c in range(nc):
        sel = lambda xs: xs[c * nq:(c + 1) * nq]
        arh = [_bdot_nt(ar, ht) for ar, ht in zip(sel(ars), hts)]
        us = [_bdot(t, x[:P2] + y) for t, x, y in zip(sel(tinv), arh, sel(akv))]
        uv = [jnp.concatenate([u, vs], axis=0).astype(BF16) for u, vs in zip(us, sel(vss))]
        hts = [ht * gl_ + _bdot_tn(w, b) for ht, gl_, w, b in zip(hts, sel(g_last), uv, sel(bkl))]
        ys = [x[P2:] + jnp.dot(m, w, preferred_element_type=F32) for x, m, w in zip(arh, sel(a_r), uv)]
        yrows.append(jnp.concatenate([y[:L] + y[L:] for y in ys], axis=1))
    for q in range(nq):
        state_ref[q] = hts[q]
    y = jnp.concatenate(yrows, axis=0) if len(yrows) > 1 else yrows[0]
    inv_n = 1.0 / RWKV_HEAD_DIM
    outs = []
    for q, ls in enumerate(lss):
        yq = y[:, ls]
        mean = head_sum(yq) * inv_n
        yc = yq - mean
        var = head_sum(yc * yc) * inv_n
        yn = yc * lax.rsqrt(var + RWKV_GN_EPS) * lng_ref[:, ls] + lnb_ref[:, ls]
        bonus = head_sum(r[:, ls] * k[:, ls] * rk_ref[:, ls]) * v[:, ls]
        outs.append((yn + bonus) * g[:, ls])
    o_ref[...] = jnp.concatenate(outs, axis=1).astype(o_ref.dtype)


def _rwkv_mixer(proj, blk_rkv, blk_lo, bsz, seq, mu_rkv, mu_lo, w0, w_up, a0, a_up, g_up,
                k_k, k_a, r_k, ln_g, ln_b, *, tb=256):
    width = w0.shape[0]
    nl = w_up.shape[0]
    tb = min(tb, seq)
    assert seq % tb == 0 and tb % RWKV_CHUNK == 0
    nt = seq // tb
    f = lambda a: a.astype(F32).reshape(1, -1)
    lane = jnp.arange(V7X_LANES) // RWKV_HEAD_DIM
    ones_bd = (lane[:, None] == lane[None, :]).astype(BF16)
    const = lambda shape: pl.BlockSpec(shape, lambda b, t: (0,) * len(shape))
    return pl.pallas_call(
        _rwkv_kernel,
        grid=(bsz, nt),
        in_specs=[pl.BlockSpec((tb, 3 * width), lambda b, t: (b * nt + t, blk_rkv)),
                  pl.BlockSpec((tb, 4 * nl), lambda b, t: (b * nt + t, blk_lo)),
                  const((1, 3 * width)), const((1, 4 * nl)),
                  const((1, width)), const((nl, width)), const((1, width)), const((nl, width)),
                  const((2 * nl, width)),
                  const((1, width)), const((1, width)), const((1, width)), const((1, width)),
                  const((1, width)), const((V7X_LANES, V7X_LANES))],
        out_specs=pl.BlockSpec((tb, width), lambda b, t: (b * nt + t, 0)),
        out_shape=jax.ShapeDtypeStruct((bsz * seq, width), BF16),
        scratch_shapes=[pltpu.VMEM((V7X_SUBLANES, 3 * width), F32),
                        pltpu.VMEM((V7X_SUBLANES, 4 * nl), F32),
                        pltpu.VMEM((width // V7X_LANES, V7X_LANES, V7X_LANES), F32)],
        compiler_params=_cparams("parallel", "arbitrary"),
        name="rwkv_mixer",
    )(proj, proj, mu_rkv, mu_lo, f(w0), w_up.astype(BF16), f(a0), a_up.astype(BF16),
      g_up.astype(BF16), f(k_k), f(k_a), f(r_k), f(ln_g), f(ln_b), ones_bd)


def _lru_kernel(xl_ref, gl_ref, cw_ref, cb_ref, wa_ref, ba_ref, wx_ref, bx_ref, lam_ref, o_ref,
                tail, carry):
    @pl.when(pl.program_id(1) == 0)
    def _():
        tail[:V7X_SUBLANES, :] = jnp.zeros((V7X_SUBLANES, tail.shape[1]), F32)
        carry[...] = jnp.zeros_like(carry)

    tt, width = xl_ref.shape
    row = _row_iota((tt, width))
    xc = _causal_conv(xl_ref[...], tail, cw_ref, cb_ref)
    gr, gi = [], []
    for j in range(width // V7X_LANES):
        xj = xc[:, j * V7X_LANES:(j + 1) * V7X_LANES]
        gr.append(_bdot(xj, wa_ref[j]))
        gi.append(_bdot(xj, wx_ref[j]))
    gate_r = _sigmoid(jnp.concatenate(gr, axis=1) + ba_ref[...])
    gate_i = _sigmoid(jnp.concatenate(gi, axis=1) + bx_ref[...])
    log_a = -LRU_C * gate_r * _softplus(-lam_ref[...])
    a = jnp.exp(log_a)
    mult = jnp.sqrt(jnp.maximum(-(jnp.tanh(log_a) * (a * a + 1.0)), 0.0))
    mult = jnp.where(row + pl.program_id(1) * tt == 0, 1.0, mult)
    b = xc * gate_i * mult
    sub = row & (V7X_SUBLANES - 1)
    for s in (1, 2, 4):
        keep = sub >= s
        sa, sb = pltpu.roll(a, s, 0), pltpu.roll(b, s, 0)
        b = b + jnp.where(keep, a * sb, 0.0)
        a = jnp.where(keep, a * sa, a)
    c = carry[0:1, :]
    hs = []
    for i in range(tt // V7X_SUBLANES):
        rs = slice(i * V7X_SUBLANES, (i + 1) * V7X_SUBLANES)
        hi = b[rs, :] + a[rs, :] * c
        hs.append(hi)
        c = hi[V7X_SUBLANES - 1:, :]
    carry[0:1, :] = c
    h = jnp.concatenate(hs, axis=0)
    o_ref[...] = (h * _gelu(gl_ref[...])).astype(o_ref.dtype)


def _lru_mixer(proj, blk_xl, blk_gl, bsz, seq, conv_w, conv_b, w_a, b_a, w_x, b_x, lam, *, tt=256):
    nb, blk, _ = w_a.shape
    width = nb * blk
    tt = min(tt, seq)
    assert seq % tt == 0 and tt % V7X_SUBLANES == 0
    nt = seq // tt
    per = V7X_LANES // blk
    eye = jnp.eye(per, dtype=F32)
    bd = lambda w: (w.astype(F32).reshape(nb // per, per, blk, 1, blk) * eye[None, :, None, :, None]
                    ).reshape(nb // per, V7X_LANES, V7X_LANES).astype(BF16)
    f = lambda a: a.astype(F32).reshape(1, width)
    const = lambda shape: pl.BlockSpec(shape, lambda b, t: (0,) * len(shape))
    return pl.pallas_call(
        _lru_kernel,
        grid=(bsz, nt),
        in_specs=[pl.BlockSpec((tt, width), lambda b, t: (b * nt + t, blk_xl)),
                  pl.BlockSpec((tt, width), lambda b, t: (b * nt + t, blk_gl)),
                  const((CONV_K, width)), const((1, width)),
                  const((nb // per, V7X_LANES, V7X_LANES)), const((1, width)),
                  const((nb // per, V7X_LANES, V7X_LANES)), const((1, width)), const((1, width))],
        out_specs=pl.BlockSpec((tt, width), lambda b, t: (b * nt + t, 0)),
        out_shape=jax.ShapeDtypeStruct((bsz * seq, width), BF16),
        scratch_shapes=[pltpu.VMEM((tt + V7X_SUBLANES, width), F32), pltpu.VMEM((V7X_SUBLANES, width), F32)],
        compiler_params=_cparams("parallel", "arbitrary"),
        name="lru_mixer",
    )(proj, proj, conv_w.astype(F32), f(conv_b), bd(w_a), f(b_a), bd(w_x), f(b_x), f(lam))


def _pad_cols(w, n):
    return jnp.pad(w, ((0, 0), (0, n - w.shape[1])))


def _even_mixer(h, bsz, seq, norm_g, in_proj, out_proj, lam_re, lam_im, log_step, b_re, b_im, c_re, c_im,
                s5_d, glu_w, glu_b, conv_w, conv_b, dt_bias, a_log, ssd_d, ssd_norm):
    s5w = lam_re.shape[0] * S5_GROUP
    ssdw = SSD_HEADS * SSD_HEAD_DIM
    gstate2 = 2 * SSD_GROUPS * SSD_STATE
    u_w, z_w, xbc_w, dt_w = jnp.split(in_proj, [s5w, s5w + ssdw, s5w + 2 * ssdw + gstate2], axis=1)
    w_all = jnp.concatenate([z_w, xbc_w, u_w, dt_w], axis=1)
    w_all = _pad_cols(w_all, -(-w_all.shape[1] // 512) * 512).astype(BF16)
    proj = _norm_matmul(h, norm_g, w_all)
    off_bc, off_u, off_dt = 2 * ssdw, 2 * ssdw + gstate2, 2 * ssdw + gstate2 + s5w
    y_a = _s5_mixer(proj, off_u // s5w, bsz, seq, lam_re, lam_im, log_step, b_re, b_im, c_re, c_im,
                    s5_d, glu_w, glu_b)
    y_b = _ssd_mixer(proj, 0, 1, off_bc // gstate2, off_dt // V7X_LANES, bsz, seq,
                     conv_w, conv_b, dt_bias, a_log, ssd_d, ssd_norm)
    return _out_proj_residual(y_a, y_b, out_proj.astype(BF16), h)


def _odd_mixer(h, bsz, seq, norm_g, in_proj, out_proj, mu, w0, w_up, a0, a_up, g_up, k_k, k_a, r_k,
               ln_g, ln_b, conv_w, conv_b, w_a, b_a, w_x, b_x, lam):
    rw = w0.shape[0]
    nl = w_up.shape[0]
    ngl = g_up.shape[0]
    lw = lam.shape[0] * lam.shape[1]
    nlp = V7X_LANES
    assert nl <= nlp and ngl == 2 * nlp
    rkv_w, wl_w, al_w, gl_w, xl_w, g2_w = jnp.split(
        in_proj, [3 * rw, 3 * rw + nl, 3 * rw + 2 * nl, 3 * rw + 2 * nl + ngl, 3 * rw + 2 * nl + ngl + lw], axis=1)
    w_all = jnp.concatenate([rkv_w, xl_w, g2_w, _pad_cols(wl_w, nlp), _pad_cols(al_w, nlp), gl_w],
                            axis=1).astype(BF16)
    mu = mu.astype(F32)
    mu_rkv = mu[:3 * rw].reshape(1, -1)
    padv = lambda v: jnp.pad(v, (0, nlp - v.shape[0]))
    mu_lo = jnp.concatenate([padv(mu[3 * rw:3 * rw + nl]), padv(mu[3 * rw + nl:3 * rw + 2 * nl]),
                             mu[3 * rw + 2 * nl:]]).reshape(1, -1)
    padr = lambda w: jnp.pad(w, ((0, nlp - w.shape[0]), (0, 0)))
    proj = _norm_matmul(h, norm_g, w_all)
    off_xl, off_gl, off_lo = 3 * rw, 3 * rw + lw, 3 * rw + 2 * lw
    y_c = _rwkv_mixer(proj, 0, off_lo // (4 * nlp), bsz, seq, mu_rkv, mu_lo, w0, padr(w_up), a0, padr(a_up),
                      g_up, k_k, k_a, r_k.reshape(-1), ln_g, ln_b)
    y_d = _lru_mixer(proj, off_xl // lw, off_gl // lw, bsz, seq, conv_w, conv_b, w_a, b_a, w_x, b_x, lam)
    return _out_proj_residual(y_c, y_d, out_proj.astype(BF16), h)


def kernel(x, p, norm_mix, norm_ffn, norm_pl, mlp_w1, mlp_w2, pl_proj, pl_gate, e_in_proj, e_out_proj, s5_lam_re, s5_lam_im, s5_log_step, s5_b_re, s5_b_im, s5_c_re, s5_c_im, s5_d, s5_glu_w, s5_glu_b, ssd_conv_w, ssd_conv_b, ssd_dt_bias, ssd_a_log, ssd_d, ssd_norm, o_in_proj, o_out_proj, rwkv_mu, rwkv_w0, rwkv_w_up, rwkv_a0, rwkv_a_up, rwkv_g_up, rwkv_k_k, rwkv_k_a, rwkv_r_k, rwkv_ln_g, rwkv_ln_b, lru_conv_w, lru_conv_b, lru_w_a, lru_b_a, lru_w_x, lru_b_x, lru_lam, norm_final):
    bsz, seq, d = x.shape
    depth = p.shape[0]
    h = x.astype(F32).reshape(bsz * seq, d)
    for i in range(depth):
        j = i // 2
        if i % 2 == 0:
            h = _even_mixer(h, bsz, seq, norm_mix[i], e_in_proj[j], e_out_proj[j], s5_lam_re[j], s5_lam_im[j],
                            s5_log_step[j], s5_b_re[j], s5_b_im[j], s5_c_re[j], s5_c_im[j], s5_d[j],
                            s5_glu_w[j], s5_glu_b[j], ssd_conv_w[j], ssd_conv_b[j], ssd_dt_bias[j],
                            ssd_a_log[j], ssd_d[j], ssd_norm[j])
        else:
            h = _odd_mixer(h, bsz, seq, norm_mix[i], o_in_proj[j], o_out_proj[j], rwkv_mu[j], rwkv_w0[j],
                           rwkv_w_up[j], rwkv_a0[j], rwkv_a_up[j], rwkv_g_up[j], rwkv_k_k[j], rwkv_k_a[j],
                           rwkv_r_k[j], rwkv_ln_g[j], rwkv_ln_b[j], lru_conv_w[j], lru_conv_b[j],
                           lru_w_a[j], lru_b_a[j], lru_w_x[j], lru_b_x[j], lru_lam[j])
        h = _mlp_residual(h, norm_ffn[i], mlp_w1[i].astype(BF16), mlp_w2[i].astype(BF16))
        gate_args = (h, norm_pl[i], pl_gate[i].astype(BF16), p.reshape(depth, bsz * seq, -1), i,
                     pl_proj[i].astype(BF16))
        if i + 1 < depth:
            h = _gated_embed(*gate_args)
        else:
            h = _gated_embed_final(*gate_args, norm_final)
    return h.reshape(bsz, seq, d)
```

```python
import functools

import jax
import jax.numpy as jnp
from jax import lax
from jax.experimental import pallas as pl
from jax.experimental.pallas import tpu as pltpu

F32 = jnp.float32
BF16 = jnp.bfloat16

V7X_LANES = 128
V7X_SUBLANES = 8
V7X_VMEM_BYTES = 64 * 1024 * 1024
VMEM_LIMIT_BYTES = V7X_VMEM_BYTES - 8 * 1024 * 1024

NORM_EPS = 1e-6
S5_GROUP = 16
S5_STATE = 64
SSD_HEAD_DIM = 64
SSD_HEADS = 24
SSD_GROUPS = 4
SSD_STATE = 128
SSD_CHUNK = 128
CONV_K = 4
RWKV_HEAD_DIM = 64
RWKV_GN_EPS = 64e-5
RWKV_CHUNK = 64
RWKV_INV_BLOCK = 16
LRU_BLOCK = 64
LRU_C = 8.0


def _cparams(*sem):
    return pltpu.CompilerParams(dimension_semantics=sem, vmem_limit_bytes=VMEM_LIMIT_BYTES)


def _bdot(a, b):
    return jnp.dot(a.astype(BF16), b.astype(BF16), preferred_element_type=F32)


def _bdot_nt(a, b):
    return lax.dot_general(a.astype(BF16), b.astype(BF16), (((1,), (1,)), ((), ())),
                           preferred_element_type=F32)


def _bdot_tn(a, b):
    return lax.dot_general(a.astype(BF16), b.astype(BF16), (((0,), (0,)), ((), ())),
                           preferred_element_type=F32)


def _split3(x):
    x1 = x.astype(BF16)
    r1 = x - x1.astype(F32)
    x2 = r1.astype(BF16)
    x3 = (r1 - x2.astype(F32)).astype(BF16)
    return x1, x2, x3


def _dot_lhs01(m01, x):
    m = m01.astype(BF16)
    x1, x2, x3 = _split3(x)
    d = lambda v: jnp.dot(m, v, preferred_element_type=F32)
    return d(x1) + d(x2) + d(x3)


def _dot_rhs01(x, m01):
    m = m01.astype(BF16)
    x1, x2, x3 = _split3(x)
    d = lambda v: jnp.dot(v, m, preferred_element_type=F32)
    return d(x1) + d(x2) + d(x3)


def _softplus(x):
    return jnp.maximum(x, 0.0) + jnp.log1p(jnp.exp(-jnp.abs(x)))


def _sigmoid(x):
    return jax.nn.sigmoid(x)


def _gelu(x):
    return jax.nn.gelu(x, approximate=True)


def _silu(x):
    return x * _sigmoid(x)


def _row_iota(shape):
    return lax.broadcasted_iota(jnp.int32, shape, 0)


def _delayed(x, tail_ref, delays):
    rows, c = x.shape
    nt = rows // V7X_SUBLANES
    x3 = x.reshape(nt, V7X_SUBLANES, c)
    tail = tail_ref[...].reshape(1, V7X_SUBLANES, c)
    sub = lax.broadcasted_iota(jnp.int32, (nt, V7X_SUBLANES, c), 1)
    out = []
    for d in delays:
        cur = pltpu.roll(x3, d, 1)
        prev = jnp.concatenate([pltpu.roll(tail, d, 1), cur[:nt - 1]], axis=0)
        out.append(jnp.where(sub >= d, cur, prev).reshape(rows, c))
    tail_ref[...] = x[rows - V7X_SUBLANES:, :]
    return out


def _causal_conv(x, ext_ref, w_ref, b_ref):
    acc = b_ref[...] + w_ref[CONV_K - 1:CONV_K, :] * x
    for k, xd in enumerate(_delayed(x, ext_ref, range(CONV_K - 1, 0, -1))):
        acc = acc + w_ref[k:k + 1, :] * xd
    return acc


def _rmsnorm_rows(h, g):
    ms = jnp.mean(h * h, axis=-1, keepdims=True)
    return h * lax.rsqrt(ms + NORM_EPS) * g


def _norm_mm_kernel(h_ref, g_ref, w_ref, o_ref):
    hn = _rmsnorm_rows(h_ref[...], g_ref[...]).astype(BF16)
    o_ref[...] = jnp.dot(hn, w_ref[...], preferred_element_type=F32)


def _col_tile(n, target):
    return max(t for t in range(V7X_LANES, max(target, V7X_LANES) + 1, V7X_LANES) if n % t == 0)


def _norm_matmul(h, g, w, *, tm, tn):
    m, k = h.shape
    n = w.shape[1]
    tm, tn = min(tm, m), _col_tile(n, tn)
    assert m % tm == 0 and n % tn == 0, (m, n, tm, tn)
    return pl.pallas_call(
        _norm_mm_kernel,
        grid=(m // tm, n // tn),
        in_specs=[pl.BlockSpec((tm, k), lambda i, j: (i, 0)),
                  pl.BlockSpec((1, k), lambda i, j: (0, 0)),
                  pl.BlockSpec((k, tn), lambda i, j: (0, j))],
        out_specs=pl.BlockSpec((tm, tn), lambda i, j: (i, j)),
        out_shape=jax.ShapeDtypeStruct((m, n), F32),
        compiler_params=_cparams("parallel", "parallel"),
        name="norm_matmul",
    )(h, g.reshape(1, k), w)


def _mm_kernel(a_ref, w_ref, o_ref):
    o_ref[...] = jnp.dot(a_ref[...], w_ref[...], preferred_element_type=F32)


def _in_proj(h, hn, g, w, *, tm=1024, tn=1536):
    if hn is None:
        return _norm_matmul(h, g, w, tm=tm, tn=tn)
    m, k = hn.shape
    n = w.shape[1]
    tm, tn = min(tm, m), _col_tile(n, tn)
    assert m % tm == 0
    return pl.pallas_call(
        _mm_kernel,
        grid=(m // tm, n // tn),
        in_specs=[pl.BlockSpec((tm, k), lambda i, j: (i, 0)), pl.BlockSpec((k, tn), lambda i, j: (0, j))],
        out_specs=pl.BlockSpec((tm, tn), lambda i, j: (i, j)),
        out_shape=jax.ShapeDtypeStruct((m, n), F32),
        compiler_params=_cparams("parallel", "parallel"),
        name="in_proj",
    )(hn, w)


def _out_proj_kernel(a1_ref, a2_ref, w1_ref, w2_ref, r_ref, o_ref):
    o_ref[...] = (r_ref[...] + jnp.dot(a1_ref[...], w1_ref[...], preferred_element_type=F32)
                  + jnp.dot(a2_ref[...], w2_ref[...], preferred_element_type=F32))


def _out_proj_residual(a1, a2, w, res, *, tm=1024, tn=1024):
    m, k1 = a1.shape
    k2 = a2.shape[1]
    n = w.shape[1]
    tm, tn = min(tm, m), min(tn, n)
    assert m % tm == 0 and n % tn == 0
    return pl.pallas_call(
        _out_proj_kernel,
        grid=(m // tm, n // tn),
        in_specs=[pl.BlockSpec((tm, k1), lambda i, j: (i, 0)),
                  pl.BlockSpec((tm, k2), lambda i, j: (i, 0)),
                  pl.BlockSpec((k1, tn), lambda i, j: (0, j)),
                  pl.BlockSpec((k2, tn), lambda i, j: (0, j)),
                  pl.BlockSpec((tm, tn), lambda i, j: (i, j))],
        out_specs=pl.BlockSpec((tm, tn), lambda i, j: (i, j)),
        out_shape=jax.ShapeDtypeStruct((m, n), F32),
        compiler_params=_cparams("parallel", "parallel"),
        name="out_proj",
    )(a1, a2, w[:k1], w[k1:], res)


def _mlp_kernel(h_ref, g_ref, w1_ref, w2_ref, o_ref, acc_ref, *, nf):
    j = pl.program_id(1)

    @pl.when(j == 0)
    def _():
        acc_ref[...] = jnp.zeros_like(acc_ref)

    hn = _rmsnorm_rows(h_ref[...], g_ref[...]).astype(BF16)
    mid = jnp.dot(hn, w1_ref[...], preferred_element_type=F32)
    mid = jnp.square(jnp.maximum(mid, 0.0)).astype(BF16)
    acc_ref[...] += jnp.dot(mid, w2_ref[...], preferred_element_type=F32)

    @pl.when(j == nf - 1)
    def _():
        o_ref[...] = h_ref[...] + acc_ref[...]


def _mlp_residual(h, g, w1, w2, layer, *, tm=512, tf=1024):
    m, d = h.shape
    f = w1.shape[2]
    tm, tf = min(tm, m), min(tf, f)
    assert m % tm == 0 and f % tf == 0
    nf = f // tf
    return pl.pallas_call(
        functools.partial(_mlp_kernel, nf=nf),
        grid=(m // tm, nf),
        in_specs=[pl.BlockSpec((tm, d), lambda i, j: (i, 0)),
                  pl.BlockSpec((1, d), lambda i, j: (0, 0)),
                  pl.BlockSpec((None, d, tf), lambda i, j: (layer, 0, j)),
                  pl.BlockSpec((None, tf, d), lambda i, j: (layer, j, 0))],
        out_specs=pl.BlockSpec((tm, d), lambda i, j: (i, 0)),
        out_shape=jax.ShapeDtypeStruct((m, d), F32),
        scratch_shapes=[pltpu.VMEM((tm, d), F32)],
        compiler_params=_cparams("parallel", "arbitrary"),
        name="mlp",
    )(h, g.reshape(1, d), w1, w2)


def _gate_kernel(h_ref, g_ref, wg_ref, p_ref, wp_ref, gn_ref, *o_refs, final):
    h = h_ref[...]
    hn = _rmsnorm_rows(h, g_ref[...]).astype(BF16)
    gate = _sigmoid(jnp.dot(hn, wg_ref[...], preferred_element_type=F32))
    pp = jnp.dot(p_ref[...].astype(BF16), wp_ref[...], preferred_element_type=F32)
    ho = h + gate * pp
    nxt = _rmsnorm_rows(ho, gn_ref[...])
    if final:
        o_refs[0][...] = nxt
    else:
        o_refs[0][...] = ho
        o_refs[1][...] = nxt.astype(BF16)


def _gated_embed(h, g, wg, p, layer, wp, g_next, *, final, tm=512):
    m, k = h.shape
    kp = p.shape[2]
    tm = min(tm, m)
    assert m % tm == 0 and wg.shape[1:] == (k, k)
    const = lambda shape: pl.BlockSpec(shape, lambda i: (0, 0))
    rows = lambda w: pl.BlockSpec((tm, w), lambda i: (i, 0))
    out_specs = rows(k) if final else [rows(k), rows(k)]
    out_shape = (jax.ShapeDtypeStruct((m, k), F32) if final
                 else [jax.ShapeDtypeStruct((m, k), F32), jax.ShapeDtypeStruct((m, k), BF16)])
    return pl.pallas_call(
        functools.partial(_gate_kernel, final=final),
        grid=(m // tm,),
        in_specs=[rows(k), const((1, k)), pl.BlockSpec((None, k, k), lambda i: (layer, 0, 0)),
                  pl.BlockSpec((None, tm, kp), lambda i: (layer, i, 0)),
                  pl.BlockSpec((None, kp, k), lambda i: (layer, 0, 0)), const((1, k))],
        out_specs=out_specs,
        out_shape=out_shape,
        compiler_params=_cparams("parallel"),
        name="gated_embed",
    )(h, g.reshape(1, k), wg, p, wp, g_next.reshape(1, k))


def _s5_prep_kernel(lr_ref, li_ref, st_ref, btr_ref, bti_ref, b8_ref, a8_re_ref, a8_im_ref):
    lr, li, st = lr_ref[...], li_ref[...], jnp.exp(st_ref[...])
    lrs, lis = lr * st, li * st

    def apow(n):
        mag = jnp.exp(n * lrs)
        return mag * jnp.cos(n * lis), mag * jnp.sin(n * lis)

    mag = jnp.exp(lrs)
    abar_re, abar_im = mag * jnp.cos(lis), mag * jnp.sin(lis)
    den = lr * lr + li * li
    nr = abar_re - 1.0
    coef_re = (nr * lr + abar_im * li) / den
    coef_im = (abar_im * lr - nr * li) / den
    btr, bti = btr_ref[...], bti_ref[...]
    bbar_re = coef_re * btr - coef_im * bti
    bbar_im = coef_re * bti + coef_im * btr
    nblk, taps_ch, lanes2 = b8_ref.shape
    lane_blk = lanes2 // 2
    ch_blk = taps_ch // V7X_SUBLANES
    rg = lax.broadcasted_iota(jnp.int32, (ch_blk, lane_blk), 0) // S5_GROUP
    cg = lax.broadcasted_iota(jnp.int32, (ch_blk, lane_blk), 1) // S5_STATE
    same = rg == cg
    for s in range(V7X_SUBLANES):
        if s == 0:
            br, bi = bbar_re, bbar_im
        else:
            pr, pi = apow(float(s))
            br, bi = pr * bbar_re - pi * bbar_im, pr * bbar_im + pi * bbar_re
        for j in range(nblk):
            ls = slice(j * lane_blk, (j + 1) * lane_blk)
            rs = slice(s * ch_blk, (s + 1) * ch_blk)
            tile = lambda m: jnp.where(same, jnp.concatenate([m[:, ls]] * (ch_blk // S5_GROUP), axis=0), 0.0)
            b8_ref[j, rs, :lane_blk] = tile(br).astype(BF16)
            b8_ref[j, rs, lane_blk:] = tile(bi).astype(BF16)
    a8r, a8i = apow(float(V7X_SUBLANES))
    a8_re_ref[...] = jnp.broadcast_to(a8r, a8_re_ref.shape)
    a8_im_ref[...] = jnp.broadcast_to(a8i, a8_im_ref.shape)


def _s5_kernel(u_ref, b8_ref, cre_ref, cim_ref, a8r_ref, a8i_ref, d_ref, gw_ref, gb_ref, o_ref,
               ue_ref, car_re, car_im):
    @pl.when(pl.program_id(1) == 0)
    def _():
        ue_ref[...] = jnp.zeros_like(ue_ref)
        car_re[...] = jnp.zeros_like(car_re)
        car_im[...] = jnp.zeros_like(car_im)

    u = u_ref[...]
    tt = u.shape[0]
    nblk, taps_ch, lanes2 = b8_ref.shape
    lane_blk = lanes2 // 2
    ch_blk = taps_ch // V7X_SUBLANES
    taps = [u] + _delayed(u, ue_ref, range(1, V7X_SUBLANES))
    ys = []
    for j in range(nblk):
        ls = slice(j * lane_blk, (j + 1) * lane_blk)
        cs = slice(j * ch_blk, (j + 1) * ch_blk)
        u8 = jnp.concatenate([t[:, cs].astype(BF16) for t in taps], axis=1)
        z = jnp.dot(u8, b8_ref[j], preferred_element_type=F32)
        pr, pi = a8r_ref[:, ls], a8i_ref[:, ls]
        cr, ci = car_re[:, ls], car_im[:, ls]
        xr, xi = [], []
        for i in range(tt // V7X_SUBLANES):
            rs = slice(i * V7X_SUBLANES, (i + 1) * V7X_SUBLANES)
            cr, ci = (z[rs, :lane_blk] + (pr * cr - pi * ci), z[rs, lane_blk:] + (pr * ci + pi * cr))
            xr.append(cr)
            xi.append(ci)
        car_re[:, ls] = cr
        car_im[:, ls] = ci
        ys.append(_bdot(jnp.concatenate(xr, axis=0), cre_ref[j]) - _bdot(jnp.concatenate(xi, axis=0), cim_ref[j]))
    y = jnp.concatenate(ys, axis=1) + d_ref[...] * u
    act = _gelu(y)
    o_ref[...] = (act * _sigmoid(_bdot(act, gw_ref[...]) + gb_ref[...])).astype(o_ref.dtype)


def _s5_mixer(proj, col_blk, bsz, seq, lam_re, lam_im, log_step, b_re, b_im, c_re, c_im,
              d_skip, glu_w, glu_b, *, tt=256, lane_blk=512):
    ng, ns = lam_re.shape
    width = ng * S5_GROUP
    nstate = ng * ns
    tt = min(tt, seq)
    assert seq % tt == 0 and tt % V7X_SUBLANES == 0 and nstate % lane_blk == 0 and lane_blk % ns == 0
    nblk = nstate // lane_blk
    gpb = lane_blk // ns
    ch_blk = gpb * S5_GROUP
    flat = lambda a: a.astype(F32).reshape(1, nstate)
    step = jnp.broadcast_to(log_step.astype(F32)[:, None], (ng, ns))
    bt = lambda a: a.astype(F32).transpose(2, 0, 1).reshape(S5_GROUP, nstate)
    vm = pl.BlockSpec(memory_space=pltpu.VMEM)
    b8, a8_re, a8_im = pl.pallas_call(
        _s5_prep_kernel,
        in_specs=[vm] * 5,
        out_specs=[vm] * 3,
        out_shape=[jax.ShapeDtypeStruct((nblk, V7X_SUBLANES * ch_blk, 2 * lane_blk), BF16)]
        + [jax.ShapeDtypeStruct((V7X_SUBLANES, nstate), F32)] * 2,
        compiler_params=pltpu.CompilerParams(vmem_limit_bytes=VMEM_LIMIT_BYTES),
        name="s5_prep",
    )(flat(lam_re), flat(lam_im), flat(step), bt(b_re), bt(b_im))
    eye = jnp.eye(gpb, dtype=F32)
    cbd = lambda c: (c.astype(F32).reshape(nblk, gpb, S5_GROUP, ns).transpose(0, 1, 3, 2)[:, :, :, None, :]
                     * eye[None, :, None, :, None]).reshape(nblk, lane_blk, ch_blk).astype(BF16)
    nt = seq // tt
    const = lambda shape: pl.BlockSpec(shape, lambda b, t: (0,) * len(shape))
    return pl.pallas_call(
        _s5_kernel,
        grid=(bsz, nt),
        in_specs=[pl.BlockSpec((tt, width), lambda b, t: (b * nt + t, col_blk)),
                  const((nblk, V7X_SUBLANES * ch_blk, 2 * lane_blk)),
                  const((nblk, lane_blk, ch_blk)), const((nblk, lane_blk, ch_blk)),
                  const((V7X_SUBLANES, nstate)), const((V7X_SUBLANES, nstate)),
                  const((1, width)), const((width, width)), const((1, width))],
        out_specs=pl.BlockSpec((tt, width), lambda b, t: (b * nt + t, 0)),
        out_shape=jax.ShapeDtypeStruct((bsz * seq, width), BF16),
        scratch_shapes=[pltpu.VMEM((V7X_SUBLANES, width), F32)]
        + [pltpu.VMEM((V7X_SUBLANES, nstate), F32)] * 2,
        compiler_params=_cparams("parallel", "arbitrary"),
        name="s5_mixer",
    )(proj, b8, cbd(c_re), cbd(c_im), a8_re, a8_im,
      d_skip.astype(F32).reshape(1, width), glu_w.astype(BF16), glu_b.astype(F32).reshape(1, width))


def _ssd_kernel(z_ref, xs_ref, bc_ref, dt_ref, cwx_ref, cbx_ref, cwb_ref, cbb_ref, e64_ref,
                dtb_ref, alog_ref, alogx_ref, dskx_ref, ng_ref, o_ref,
                tailx, tailb, st_ref):
    @pl.when(pl.program_id(1) == 0)
    def _():
        tailx[...] = jnp.zeros_like(tailx)
        tailb[...] = jnp.zeros_like(tailb)
        st_ref[...] = jnp.zeros_like(st_ref)

    L = SSD_CHUNK
    tt = xs_ref.shape[0]
    width = xs_ref.shape[1]
    gstate = SSD_GROUPS * SSD_STATE
    xs = _silu(_causal_conv(xs_ref[...], tailx, cwx_ref, cbx_ref))
    bc = _silu(_causal_conv(bc_ref[...], tailb, cwb_ref, cbb_ref))
    dt = _softplus(dt_ref[...] + dtb_ref[...])
    da = dt * (-jnp.exp(alog_ref[...]))
    dtx = _dot_rhs01(dt, e64_ref[...])
    dax = dtx * (-jnp.exp(alogx_ref[...]))
    xdt = xs * dtx
    ri = lax.broadcasted_iota(jnp.int32, (L, L), 0)
    ci = lax.broadcasted_iota(jnp.int32, (L, L), 1)
    tril = ri >= ci
    tri01 = tril.astype(BF16)
    lane_lo = lax.broadcasted_iota(jnp.int32, (1, V7X_LANES), 1) < SSD_HEAD_DIM
    pairs_per_group = SSD_HEADS // SSD_GROUPS // 2
    ys = []
    for c in range(tt // L):
        rs = slice(c * L, (c + 1) * L)
        acum = _dot_lhs01(tri01, da[rs, :])
        acum_t = acum.T
        acx = _dot_lhs01(tri01, dax[rs, :])
        ycols = []
        for g in range(SSD_GROUPS):
            bg = bc[rs, g * SSD_STATE:(g + 1) * SSD_STATE]
            cg = bc[rs, gstate + g * SSD_STATE:gstate + (g + 1) * SSD_STATE]
            scores = _bdot_nt(cg, bg)
            bg_t = bg.T
            for pr in range(pairs_per_group):
                q = g * pairs_per_group + pr
                ls = slice(q * V7X_LANES, (q + 1) * V7X_LANES)
                xp = xdt[rs, ls]
                acp = acx[:, ls]
                yd = []
                for h in (2 * q, 2 * q + 1):
                    seg = acum[:, h:h + 1] - acum_t[h:h + 1, :]
                    yd.append(_bdot(scores * jnp.exp(jnp.where(tril, seg, -jnp.inf)), xp))
                y = jnp.where(lane_lo, yd[0], yd[1])
                prev_t = st_ref[q]
                y = y + _bdot(cg, prev_t) * jnp.exp(acp)
                last = acp[L - 1:L, :]
                st_ref[q] = prev_t * jnp.exp(last) + _bdot(bg_t, xp * jnp.exp(last - acp))
                ycols.append(y)
        ys.append(jnp.concatenate(ycols, axis=1))
    y = jnp.concatenate(ys, axis=0) if len(ys) > 1 else ys[0]
    y = (y + xs * dskx_ref[...]) * _silu(z_ref[...])
    gw = width // SSD_GROUPS
    outs = []
    for g in range(SSD_GROUPS):
        yg = y[:, g * gw:(g + 1) * gw]
        outs.append(yg * lax.rsqrt(jnp.mean(yg * yg, axis=-1, keepdims=True) + NORM_EPS))
    o_ref[...] = (jnp.concatenate(outs, axis=1) * ng_ref[...]).astype(o_ref.dtype)


def _ssd_mixer(proj, blk_z, blk_xs, blk_bc, blk_dt, bsz, seq, conv_w, conv_b, dt_bias, a_log,
               d_skip, norm_g, *, tt=256):
    width = SSD_HEADS * SSD_HEAD_DIM
    gstate2 = 2 * SSD_GROUPS * SSD_STATE
    tt = min(tt, seq)
    assert seq % tt == 0 and tt % SSD_CHUNK == 0
    nt = seq // tt
    f = lambda a: a.astype(F32)
    pad_h = lambda a: jnp.pad(f(a), (0, V7X_LANES - SSD_HEADS)).reshape(1, V7X_LANES)
    exp_h = lambda a: jnp.repeat(f(a), SSD_HEAD_DIM).reshape(1, width)
    e64 = (jnp.arange(V7X_LANES)[:, None] == (jnp.arange(width)[None, :] // SSD_HEAD_DIM)).astype(BF16)
    const = lambda shape: pl.BlockSpec(shape, lambda b, t: (0,) * len(shape))
    rowblk = lambda w, cb: pl.BlockSpec((tt, w), lambda b, t: (b * nt + t, cb))
    return pl.pallas_call(
        _ssd_kernel,
        grid=(bsz, nt),
        in_specs=[rowblk(width, blk_z), rowblk(width, blk_xs), rowblk(gstate2, blk_bc),
                  rowblk(V7X_LANES, blk_dt),
                  const((CONV_K, width)), const((1, width)), const((CONV_K, gstate2)), const((1, gstate2)),
                  const((V7X_LANES, width)),
                  const((1, V7X_LANES)), const((1, V7X_LANES)),
                  const((1, width)), const((1, width)), const((1, width))],
        out_specs=pl.BlockSpec((tt, width), lambda b, t: (b * nt + t, 0)),
        out_shape=jax.ShapeDtypeStruct((bsz * seq, width), BF16),
        scratch_shapes=[pltpu.VMEM((V7X_SUBLANES, width), F32),
                        pltpu.VMEM((V7X_SUBLANES, gstate2), F32),
                        pltpu.VMEM((SSD_HEADS // 2, SSD_STATE, V7X_LANES), F32)],
        compiler_params=_cparams("parallel", "arbitrary"),
        name="ssd_mixer",
    )(proj, proj, proj, proj,
      f(conv_w[:, :width]), f(conv_b[:width]).reshape(1, width),
      f(conv_w[:, width:]), f(conv_b[width:]).reshape(1, gstate2),
      e64, pad_h(dt_bias), pad_h(a_log), exp_h(a_log), exp_h(d_skip),
      f(norm_g).reshape(1, width))


def _stack_heads(x, lane_lo):
    return jnp.concatenate([jnp.where(lane_lo, x, 0.0), jnp.where(lane_lo, 0.0, x)], axis=0)


def _unit_lower_inverses(mats, same_blk, eye):
    size = eye.shape[0]
    side = lambda x, y: _bdot(x, jnp.concatenate([x, y], axis=1))
    ad = [jnp.where(same_blk, a, 0.0) for a in mats]
    ao = [a - d for a, d in zip(mats, ad)]
    s1 = [eye + d for d in ad]
    a2 = [_bdot(d, d) for d in ad]
    r = [side(x, s) for x, s in zip(a2, s1)]
    a4, s2 = [x[:, :size] for x in r], [s + x[:, size:] for s, x in zip(s1, r)]
    r = [side(x, s) for x, s in zip(a4, s2)]
    a8, s3 = [x[:, :size] for x in r], [s + x[:, size:] for s, x in zip(s2, r)]
    td = [s + _bdot(x, s) for x, s in zip(a8, s3)]
    n = [_bdot(t, o) for t, o in zip(td, ao)]
    r = [side(x, t) for x, t in zip(n, td)]
    n2, w1 = [x[:, :size] for x in r], [t + x[:, size:] for t, x in zip(td, r)]
    return [w + _bdot(x, w) for x, w in zip(n2, w1)]


def _rwkv_kernel(rkv_ref, lo_ref, mu_rkv_ref, mu_lo_ref, w0_ref, wup_ref, a0_ref, aup_ref, gup_ref,
                 kk_ref, ka_ref, rk_ref, lng_ref, lnb_ref, ones_ref, o_ref,
                 prev_rkv, prev_lo, state_ref):
    @pl.when(pl.program_id(1) == 0)
    def _():
        prev_rkv[...] = jnp.zeros_like(prev_rkv)
        prev_lo[...] = jnp.zeros_like(prev_lo)
        state_ref[...] = jnp.zeros_like(state_ref)

    L = RWKV_CHUNK
    tb = rkv_ref.shape[0]
    width = rkv_ref.shape[1] // 3

    def shift_mix(ref, prev, mu_ref):
        f = ref[...]
        fs = jnp.where(_row_iota(f.shape) == 0, prev[0:1, :], pltpu.roll(f, 1, 0))
        prev[0:1, :] = f[tb - 1:tb, :]
        return f + (fs - f) * mu_ref[...]

    rkv = shift_mix(rkv_ref, prev_rkv, mu_rkv_ref)
    lo = shift_mix(lo_ref, prev_lo, mu_lo_ref)
    r, k, v = rkv[:, :width], rkv[:, width:2 * width], rkv[:, 2 * width:]
    nl = wup_ref.shape[0]
    wl, al, gl = lo[:, :nl], lo[:, nl:2 * nl], lo[:, 2 * nl:]
    w = -_softplus(-(w0_ref[...] + _bdot(jnp.tanh(wl), wup_ref[...]))) - 0.5
    logw = -jnp.exp(w)
    a_sig = _sigmoid(a0_ref[...] + _bdot(al, aup_ref[...]))
    g = _bdot(_sigmoid(gl), gup_ref[...])
    kk = k * kk_ref[...]
    k = k * (1.0 + (a_sig - 1.0) * ka_ref[...])
    ones_bd = ones_ref[...]

    def head_sum(x):
        return jnp.dot(x.astype(BF16), ones_bd, preferred_element_type=F32)

    P2 = 2 * L
    ri = lax.broadcasted_iota(jnp.int32, (P2, P2), 0)
    ci = lax.broadcasted_iota(jnp.int32, (P2, P2), 1)
    same_head = (ri // L) == (ci // L)
    strict = same_head & (ri > ci)
    incl = same_head & (ri >= ci)
    same_blk = (ri // RWKV_INV_BLOCK) == (ci // RWKV_INV_BLOCK)
    eye = (ri == ci).astype(F32)
    tri01 = (lax.broadcasted_iota(jnp.int32, (L, L), 0) >= lax.broadcasted_iota(jnp.int32, (L, L), 1)).astype(BF16)
    lane_lo = lax.broadcasted_iota(jnp.int32, (1, V7X_LANES), 1) < RWKV_HEAD_DIM

    nq = width // V7X_LANES
    lss = [slice(q * V7X_LANES, (q + 1) * V7X_LANES) for q in range(nq)]
    kks = [kk[:, ls] for ls in lss]
    kkn = [x * lax.rsqrt(jnp.maximum(head_sum(x * x), 1e-24)) for x in kks]
    nc = tb // L
    inst = [(slice(c * L, (c + 1) * L), q, ls) for c in range(nc) for q, ls in enumerate(lss)]
    cums = [_dot_lhs01(tri01, logw[c * L:(c + 1) * L, :]) for c in range(nc)]
    g_in = jnp.concatenate([jnp.exp(cm) for cm in cums], axis=0)
    g_prev = jnp.concatenate([jnp.exp(cm - logw[c * L:(c + 1) * L, :]) for c, cm in enumerate(cums)], axis=0)
    g_inv = jnp.concatenate([jnp.exp(-cm) for cm in cums], axis=0)
    ats = [_stack_heads(-kkn[q][rs] * g_prev[rs, ls], lane_lo) for rs, q, ls in inst]
    rts = [_stack_heads(r[rs, ls] * g_in[rs, ls], lane_lo) for rs, q, ls in inst]
    bts = [_stack_heads(kkn[q][rs] * a_sig[rs, ls] * g_inv[rs, ls], lane_lo) for rs, q, ls in inst]
    kts = [_stack_heads(k[rs, ls] * g_inv[rs, ls], lane_lo) for rs, q, ls in inst]
    vss = [_stack_heads(v[rs, ls], lane_lo) for rs, q, ls in inst]
    ars = [jnp.concatenate([a, b], axis=0).astype(BF16) for a, b in zip(ats, rts)]
    ps = [_bdot_nt(ar, jnp.concatenate([b, kq], axis=0)) for ar, b, kq in zip(ars, bts, kts)]
    tinv = _unit_lower_inverses([jnp.where(strict, p[:P2, :P2], 0.0) for p in ps], same_blk, eye)
    akv = [_bdot(jnp.where(strict, p[:P2, P2:], 0.0), vs) for p, vs in zip(ps, vss)]
    a_r = [jnp.concatenate([jnp.where(incl, p[P2:, :P2], 0.0), jnp.where(incl, p[P2:, P2:], 0.0)],
                           axis=1).astype(BF16) for p in ps]
    g_last = [g_in[rs, ls][L - 1:L, :] for rs, q, ls in inst]
    bkl = [(jnp.concatenate([b, kq], axis=0) * gl_).astype(BF16) for b, kq, gl_ in zip(bts, kts, g_last)]
    hts = [state_ref[q] for q in range(nq)]
    yrows = []
    for c in range(nc):
        sel = lambda xs: xs[c * nq:(c + 1) * nq]
        arh = [_bdot_nt(ar, ht) for ar, ht in zip(sel(ars), hts)]
        us = [_bdot(t, x[:P2] + y) for t, x, y in zip(sel(tinv), arh, sel(akv))]
        uv = [jnp.concatenate([u, vs], axis=0).astype(BF16) for u, vs in zip(us, sel(vss))]
        hts = [ht * gl_ + _bdot_tn(w, b) for ht, gl_, w, b in zip(hts, sel(g_last), uv, sel(bkl))]
        ys = [x[P2:] + jnp.dot(m, w, preferred_element_type=F32) for x, m, w in zip(arh, sel(a_r), uv)]
        yrows.append(jnp.concatenate([y[:L] + y[L:] for y in ys], axis=1))
    for q in range(nq):
        state_ref[q] = hts[q]
    y = jnp.concatenate(yrows, axis=0) if len(yrows) > 1 else yrows[0]
    inv_n = 1.0 / RWKV_HEAD_DIM
    outs = []
    for q, ls in enumerate(lss):
        yq = y[:, ls]
        mean = head_sum(yq) * inv_n
        yc = yq - mean
        var = head_sum(yc * yc) * inv_n
        yn = yc * lax.rsqrt(var + RWKV_GN_EPS) * lng_ref[:, ls] + lnb_ref[:, ls]
        bonus = head_sum(r[:, ls] * k[:, ls] * rk_ref[:, ls]) * v[:, ls]
        outs.append((yn + bonus) * g[:, ls])
    o_ref[...] = jnp.concatenate(outs, axis=1).astype(o_ref.dtype)


def _rwkv_mixer(proj, blk_rkv, blk_lo, bsz, seq, mu_rkv, mu_lo, w0, w_up, a0, a_up, g_up,
                k_k, k_a, r_k, ln_g, ln_b, *, tb=256):
    width = w0.shape[0]
    nl = w_up.shape[0]
    tb = min(tb, seq)
    assert seq % tb == 0 and tb % RWKV_CHUNK == 0
    nt = seq // tb
    f = lambda a: a.astype(F32).reshape(1, -1)
    lane = jnp.arange(V7X_LANES) // RWKV_HEAD_DIM
    ones_bd = (lane[:, None] == lane[None, :]).astype(BF16)
    const = lambda shape: pl.BlockSpec(shape, lambda b, t: (0,) * len(shape))
    return pl.pallas_call(
        _rwkv_kernel,
        grid=(bsz, nt),
        in_specs=[pl.BlockSpec((tb, 3 * width), lambda b, t: (b * nt + t, blk_rkv)),
                  pl.BlockSpec((tb, 4 * nl), lambda b, t: (b * nt + t, blk_lo)),
                  const((1, 3 * width)), const((1, 4 * nl)),
                  const((1, width)), const((nl, width)), const((1, width)), const((nl, width)),
                  const((2 * nl, width)),
                  const((1, width)), const((1, width)), const((1, width)), const((1, width)),
                  const((1, width)), const((V7X_LANES, V7X_LANES))],
        out_specs=pl.BlockSpec((tb, width), lambda b, t: (b * nt + t, 0)),
        out_shape=jax.ShapeDtypeStruct((bsz * seq, width), BF16),
        scratch_shapes=[pltpu.VMEM((V7X_SUBLANES, 3 * width), F32),
                        pltpu.VMEM((V7X_SUBLANES, 4 * nl), F32),
                        pltpu.VMEM((width // V7X_LANES, V7X_LANES, V7X_LANES), F32)],
        compiler_params=_cparams("parallel", "arbitrary"),
        name="rwkv_mixer",
    )(proj, proj, mu_rkv, mu_lo, f(w0), w_up.astype(BF16), f(a0), a_up.astype(BF16),
      g_up.astype(BF16), f(k_k), f(k_a), f(r_k), f(ln_g), f(ln_b), ones_bd)


def _lru_kernel(xl_ref, gl_ref, cw_ref, cb_ref, wa_ref, ba_ref, wx_ref, bx_ref, lam_ref, o_ref,
                tail, carry):
    @pl.when(pl.program_id(1) == 0)
    def _():
        tail[...] = jnp.zeros_like(tail)
        carry[...] = jnp.zeros_like(carry)

    tt, width = xl_ref.shape
    row = _row_iota((tt, width))
    xc = _causal_conv(xl_ref[...], tail, cw_ref, cb_ref)
    gr, gi = [], []
    for j in range(width // V7X_LANES):
        xj = xc[:, j * V7X_LANES:(j + 1) * V7X_LANES]
        gr.append(_bdot(xj, wa_ref[j]))
        gi.append(_bdot(xj, wx_ref[j]))
    gate_r = _sigmoid(jnp.concatenate(gr, axis=1) + ba_ref[...])
    gate_i = _sigmoid(jnp.concatenate(gi, axis=1) + bx_ref[...])
    log_a = -LRU_C * gate_r * _softplus(-lam_ref[...])
    a = jnp.exp(log_a)
    mult = jnp.sqrt(jnp.maximum(-(jnp.tanh(log_a) * (a * a + 1.0)), 0.0))
    mult = jnp.where(row + pl.program_id(1) * tt == 0, 1.0, mult)
    b = xc * gate_i * mult
    nt = tt // V7X_SUBLANES
    a, b = a.reshape(nt, V7X_SUBLANES, width), b.reshape(nt, V7X_SUBLANES, width)
    sub = lax.broadcasted_iota(jnp.int32, a.shape, 1)
    for s in (1, 2, 4):
        keep = sub >= s
        sa, sb = pltpu.roll(a, s, 1), pltpu.roll(b, s, 1)
        b = b + jnp.where(keep, a * sb, 0.0)
        a = jnp.where(keep, a * sa, a)
    a, b = a.reshape(tt, width), b.reshape(tt, width)
    c = carry[0:1, :]
    hs = []
    for i in range(tt // V7X_SUBLANES):
        rs = slice(i * V7X_SUBLANES, (i + 1) * V7X_SUBLANES)
        hi = b[rs, :] + a[rs, :] * c
        hs.append(hi)
        c = hi[V7X_SUBLANES - 1:, :]
    carry[0:1, :] = c
    h = jnp.concatenate(hs, axis=0)
    o_ref[...] = (h * _gelu(gl_ref[...])).astype(o_ref.dtype)


def _lru_mixer(proj, blk_xl, blk_gl, bsz, seq, conv_w, conv_b, w_a, b_a, w_x, b_x, lam, *, tt=256):
    nb, blk, _ = w_a.shape
    width = nb * blk
    tt = min(tt, seq)
    assert seq % tt == 0 and tt % V7X_SUBLANES == 0
    nt = seq // tt
    per = V7X_LANES // blk
    eye = jnp.eye(per, dtype=F32)
    bd = lambda w: (w.astype(F32).reshape(nb // per, per, blk, 1, blk) * eye[None, :, None, :, None]
                    ).reshape(nb // per, V7X_LANES, V7X_LANES).astype(BF16)
    f = lambda a: a.astype(F32).reshape(1, width)
    const = lambda shape: pl.BlockSpec(shape, lambda b, t: (0,) * len(shape))
    return pl.pallas_call(
        _lru_kernel,
        grid=(bsz, nt),
        in_specs=[pl.BlockSpec((tt, width), lambda b, t: (b * nt + t, blk_xl)),
                  pl.BlockSpec((tt, width), lambda b, t: (b * nt + t, blk_gl)),
                  const((CONV_K, width)), const((1, width)),
                  const((nb // per, V7X_LANES, V7X_LANES)), const((1, width)),
                  const((nb // per, V7X_LANES, V7X_LANES)), const((1, width)), const((1, width))],
        out_specs=pl.BlockSpec((tt, width), lambda b, t: (b * nt + t, 0)),
        out_shape=jax.ShapeDtypeStruct((bsz * seq, width), BF16),
        scratch_shapes=[pltpu.VMEM((V7X_SUBLANES, width), F32), pltpu.VMEM((V7X_SUBLANES, width), F32)],
        compiler_params=_cparams("parallel", "arbitrary"),
        name="lru_mixer",
    )(proj, proj, conv_w.astype(F32), f(conv_b), bd(w_a), f(b_a), bd(w_x), f(b_x), f(lam))


def _pad_cols(w, n):
    return jnp.pad(w, ((0, 0), (0, n - w.shape[1])))


def _even_mixer(h, hn, bsz, seq, norm_g, in_proj, out_proj, lam_re, lam_im, log_step, b_re, b_im, c_re, c_im,
                s5_d, glu_w, glu_b, conv_w, conv_b, dt_bias, a_log, ssd_d, ssd_norm):
    s5w = lam_re.shape[0] * S5_GROUP
    ssdw = SSD_HEADS * SSD_HEAD_DIM
    gstate2 = 2 * SSD_GROUPS * SSD_STATE
    u_w, z_w, xbc_w, dt_w = jnp.split(in_proj, [s5w, s5w + ssdw, s5w + 2 * ssdw + gstate2], axis=1)
    w_all = jnp.concatenate([z_w, xbc_w, u_w, dt_w], axis=1)
    w_all = _pad_cols(w_all, -(-w_all.shape[1] // 512) * 512).astype(BF16)
    proj = _in_proj(h, hn, norm_g, w_all)
    off_bc, off_u, off_dt = 2 * ssdw, 2 * ssdw + gstate2, 2 * ssdw + gstate2 + s5w
    y_a = _s5_mixer(proj, off_u // s5w, bsz, seq, lam_re, lam_im, log_step, b_re, b_im, c_re, c_im,
                    s5_d, glu_w, glu_b)
    y_b = _ssd_mixer(proj, 0, 1, off_bc // gstate2, off_dt // V7X_LANES, bsz, seq,
                     conv_w, conv_b, dt_bias, a_log, ssd_d, ssd_norm)
    return _out_proj_residual(y_a, y_b, out_proj.astype(BF16), h)


def _odd_mixer(h, hn, bsz, seq, norm_g, in_proj, out_proj, mu, w0, w_up, a0, a_up, g_up, k_k, k_a, r_k,
               ln_g, ln_b, conv_w, conv_b, w_a, b_a, w_x, b_x, lam):
    rw = w0.shape[0]
    nl = w_up.shape[0]
    ngl = g_up.shape[0]
    lw = lam.shape[0] * lam.shape[1]
    nlp = V7X_LANES
    assert nl <= nlp and ngl == 2 * nlp
    rkv_w, wl_w, al_w, gl_w, xl_w, g2_w = jnp.split(
        in_proj, [3 * rw, 3 * rw + nl, 3 * rw + 2 * nl, 3 * rw + 2 * nl + ngl, 3 * rw + 2 * nl + ngl + lw], axis=1)
    w_all = jnp.concatenate([rkv_w, xl_w, g2_w, _pad_cols(wl_w, nlp), _pad_cols(al_w, nlp), gl_w],
                            axis=1).astype(BF16)
    mu = mu.astype(F32)
    mu_rkv = mu[:3 * rw].reshape(1, -1)
    padv = lambda v: jnp.pad(v, (0, nlp - v.shape[0]))
    mu_lo = jnp.concatenate([padv(mu[3 * rw:3 * rw + nl]), padv(mu[3 * rw + nl:3 * rw + 2 * nl]),
                             mu[3 * rw + 2 * nl:]]).reshape(1, -1)
    padr = lambda w: jnp.pad(w, ((0, nlp - w.shape[0]), (0, 0)))
    proj = _in_proj(h, hn, norm_g, w_all)
    off_xl, off_gl, off_lo = 3 * rw, 3 * rw + lw, 3 * rw + 2 * lw
    y_c = _rwkv_mixer(proj, 0, off_lo // (4 * nlp), bsz, seq, mu_rkv, mu_lo, w0, padr(w_up), a0, padr(a_up),
                      g_up, k_k, k_a, r_k.reshape(-1), ln_g, ln_b)
    y_d = _lru_mixer(proj, off_xl // lw, off_gl // lw, bsz, seq, conv_w, conv_b, w_a, b_a, w_x, b_x, lam)
    return _out_proj_residual(y_c, y_d, out_proj.astype(BF16), h)


def kernel(x, p, norm_mix, norm_ffn, norm_pl, mlp_w1, mlp_w2, pl_proj, pl_gate, e_in_proj, e_out_proj, s5_lam_re, s5_lam_im, s5_log_step, s5_b_re, s5_b_im, s5_c_re, s5_c_im, s5_d, s5_glu_w, s5_glu_b, ssd_conv_w, ssd_conv_b, ssd_dt_bias, ssd_a_log, ssd_d, ssd_norm, o_in_proj, o_out_proj, rwkv_mu, rwkv_w0, rwkv_w_up, rwkv_a0, rwkv_a_up, rwkv_g_up, rwkv_k_k, rwkv_k_a, rwkv_r_k, rwkv_ln_g, rwkv_ln_b, lru_conv_w, lru_conv_b, lru_w_a, lru_b_a, lru_w_x, lru_b_x, lru_lam, norm_final):
    bsz, seq, d = x.shape
    depth = p.shape[0]
    h = x.astype(F32).reshape(bsz * seq, d)
    hn = None
    w1_all, w2_all = mlp_w1.astype(BF16), mlp_w2.astype(BF16)
    wg_all, wp_all = pl_gate.astype(BF16), pl_proj.astype(BF16)
    p_all = p.reshape(depth, bsz * seq, -1)
    for i in range(depth):
        j = i // 2
        if i % 2 == 0:
            h = _even_mixer(h, hn, bsz, seq, norm_mix[i], e_in_proj[j], e_out_proj[j], s5_lam_re[j], s5_lam_im[j],
                            s5_log_step[j], s5_b_re[j], s5_b_im[j], s5_c_re[j], s5_c_im[j], s5_d[j],
                            s5_glu_w[j], s5_glu_b[j], ssd_conv_w[j], ssd_conv_b[j], ssd_dt_bias[j],
                            ssd_a_log[j], ssd_d[j], ssd_norm[j])
        else:
            h = _odd_mixer(h, hn, bsz, seq, norm_mix[i], o_in_proj[j], o_out_proj[j], rwkv_mu[j], rwkv_w0[j],
                           rwkv_w_up[j], rwkv_a0[j], rwkv_a_up[j], rwkv_g_up[j], rwkv_k_k[j], rwkv_k_a[j],
                           rwkv_r_k[j], rwkv_ln_g[j], rwkv_ln_b[j], lru_conv_w[j], lru_conv_b[j],
                           lru_w_a[j], lru_b_a[j], lru_w_x[j], lru_b_x[j], lru_lam[j])
        h = _mlp_residual(h, norm_ffn[i], w1_all, w2_all, i)
        gate_args = (h, norm_pl[i], wg_all, p_all, i, wp_all)
        if i + 1 < depth:
            h, hn = _gated_embed(*gate_args, norm_mix[i + 1], final=False)
        else:
            h = _gated_embed(*gate_args, norm_final, final=True)
    return h.reshape(bsz, seq, d)
```

```python
import functools

import jax
import jax.numpy as jnp
from jax import lax
from jax.experimental import pallas as pl
from jax.experimental.pallas import tpu as pltpu

F32 = jnp.float32
BF16 = jnp.bfloat16

V7X_LANES = 128
V7X_SUBLANES = 8
V7X_BF16_SUBLANES = 16
V7X_VMEM_BYTES = 64 * 1024 * 1024
VMEM_LIMIT_BYTES = V7X_VMEM_BYTES - 8 * 1024 * 1024

NORM_EPS = 1e-6
S5_GROUP = 16
S5_STATE = 64
SSD_HEAD_DIM = 64
SSD_HEADS = 24
SSD_GROUPS = 4
SSD_STATE = 128
SSD_CHUNK = 128
CONV_K = 4
RWKV_HEAD_DIM = 64
RWKV_GN_EPS = 64e-5
RWKV_CHUNK = 64
RWKV_INV_BLOCK = 16
LRU_BLOCK = 64
LRU_C = 8.0


def _cparams(*sem):
    return pltpu.CompilerParams(dimension_semantics=sem, vmem_limit_bytes=VMEM_LIMIT_BYTES)


def _bdot(a, b):
    return jnp.dot(a.astype(BF16), b.astype(BF16), preferred_element_type=F32)


def _bdot_nt(a, b):
    return lax.dot_general(a.astype(BF16), b.astype(BF16), (((1,), (1,)), ((), ())),
                           preferred_element_type=F32)


def _bdot_tn(a, b):
    return lax.dot_general(a.astype(BF16), b.astype(BF16), (((0,), (0,)), ((), ())),
                           preferred_element_type=F32)


def _split3(x):
    x1 = x.astype(BF16)
    r1 = x - x1.astype(F32)
    x2 = r1.astype(BF16)
    x3 = (r1 - x2.astype(F32)).astype(BF16)
    return x1, x2, x3


def _dot_lhs01(m01, x):
    m = m01.astype(BF16)
    x1, x2, x3 = _split3(x)
    d = lambda v: jnp.dot(m, v, preferred_element_type=F32)
    return d(x1) + d(x2) + d(x3)


def _dot_rhs01(x, m01):
    m = m01.astype(BF16)
    x1, x2, x3 = _split3(x)
    d = lambda v: jnp.dot(v, m, preferred_element_type=F32)
    return d(x1) + d(x2) + d(x3)


def _softplus(x):
    return jnp.maximum(x, 0.0) + jnp.log1p(jnp.exp(-jnp.abs(x)))


def _sigmoid(x):
    return jax.nn.sigmoid(x)


def _gelu(x):
    return jax.nn.gelu(x, approximate=True)


def _silu(x):
    return x * _sigmoid(x)


def _row_iota(shape):
    return lax.broadcasted_iota(jnp.int32, shape, 0)


def _delayed(x, tail_ref, delays):
    rows, c = x.shape
    nt = rows // V7X_SUBLANES
    x3 = x.reshape(nt, V7X_SUBLANES, c)
    tail = tail_ref[...].reshape(1, V7X_SUBLANES, c)
    sub = lax.broadcasted_iota(jnp.int32, (nt, V7X_SUBLANES, c), 1)
    out = []
    for d in delays:
        cur = pltpu.roll(x3, d, 1)
        prev = jnp.concatenate([pltpu.roll(tail, d, 1), cur[:nt - 1]], axis=0)
        out.append(jnp.where(sub >= d, cur, prev).reshape(rows, c))
    tail_ref[...] = x[rows - V7X_SUBLANES:, :]
    return out


def _causal_conv(x, ext_ref, w_ref, b_ref):
    acc = b_ref[...] + w_ref[CONV_K - 1:CONV_K, :] * x
    for k, xd in enumerate(_delayed(x, ext_ref, range(CONV_K - 1, 0, -1))):
        acc = acc + w_ref[k:k + 1, :] * xd
    return acc


def _with_weight_cast(body, n_in):
    def kernel(*refs):
        ins, wi_ref, o_ref, wo_ref, scratch = refs[:n_in], refs[n_in], refs[n_in + 1], refs[n_in + 2], refs[n_in + 3:]
        wo_ref[...] = wi_ref[...].astype(BF16)
        body(*ins, o_ref, *scratch)
    return kernel


def _cast_slab(w_all, layer, bsz, nt):
    _, rows, cols = w_all.shape
    blk = rows // (bsz * nt)
    assert blk * bsz * nt == rows and blk % V7X_BF16_SUBLANES == 0, (rows, bsz, nt)
    return (pl.BlockSpec((None, blk, cols), lambda b, t: (layer, b * nt + t, 0)),
            pl.BlockSpec((blk, cols), lambda b, t: (b * nt + t, 0)),
            jax.ShapeDtypeStruct((rows, cols), BF16))


def _rmsnorm_rows(h, g):
    ms = jnp.mean(h * h, axis=-1, keepdims=True)
    return h * lax.rsqrt(ms + NORM_EPS) * g


def _norm_mm_kernel(h_ref, g_ref, w_ref, o_ref):
    hn = _rmsnorm_rows(h_ref[...], g_ref[...]).astype(BF16)
    o_ref[...] = jnp.dot(hn, w_ref[...], preferred_element_type=F32)


def _col_tile(n, target):
    return max(t for t in range(V7X_LANES, max(target, V7X_LANES) + 1, V7X_LANES) if n % t == 0)


def _norm_matmul(h, g, w, *, tm, tn):
    m, k = h.shape
    n = w.shape[1]
    tm, tn = min(tm, m), _col_tile(n, tn)
    assert m % tm == 0 and n % tn == 0, (m, n, tm, tn)
    return pl.pallas_call(
        _norm_mm_kernel,
        grid=(m // tm, n // tn),
        in_specs=[pl.BlockSpec((tm, k), lambda i, j: (i, 0)),
                  pl.BlockSpec((1, k), lambda i, j: (0, 0)),
                  pl.BlockSpec((k, tn), lambda i, j: (0, j))],
        out_specs=pl.BlockSpec((tm, tn), lambda i, j: (i, j)),
        out_shape=jax.ShapeDtypeStruct((m, n), F32),
        compiler_params=_cparams("parallel", "parallel"),
        name="norm_matmul",
    )(h, g.reshape(1, k), w)


def _mm_kernel(a_ref, w_ref, o_ref):
    o_ref[...] = jnp.dot(a_ref[...], w_ref[...], preferred_element_type=F32)


def _in_proj(h, hn, g, w, *, tm=1024, tn=1536):
    if hn is None:
        return _norm_matmul(h, g, w, tm=tm, tn=tn)
    m, k = hn.shape
    n = w.shape[1]
    tm, tn = min(tm, m), _col_tile(n, tn)
    assert m % tm == 0
    return pl.pallas_call(
        _mm_kernel,
        grid=(m // tm, n // tn),
        in_specs=[pl.BlockSpec((tm, k), lambda i, j: (i, 0)), pl.BlockSpec((k, tn), lambda i, j: (0, j))],
        out_specs=pl.BlockSpec((tm, tn), lambda i, j: (i, j)),
        out_shape=jax.ShapeDtypeStruct((m, n), F32),
        compiler_params=_cparams("parallel", "parallel"),
        name="in_proj",
    )(hn, w)


def _out_proj_kernel(a1_ref, a2_ref, w1_ref, w2_ref, r_ref, g_ref, o_ref, on_ref):
    ho = (r_ref[...] + jnp.dot(a1_ref[...], w1_ref[...], preferred_element_type=F32)
          + jnp.dot(a2_ref[...], w2_ref[...], preferred_element_type=F32))
    o_ref[...] = ho
    on_ref[...] = _rmsnorm_rows(ho, g_ref[...]).astype(BF16)


def _out_proj_residual(a1, a2, w, res, g_next, *, tm=512):
    m, k1 = a1.shape
    k2 = a2.shape[1]
    n = w.shape[1]
    tm = min(tm, m)
    assert m % tm == 0
    rows = lambda width: pl.BlockSpec((tm, width), lambda i: (i, 0))
    const = lambda shape: pl.BlockSpec(shape, lambda i: (0, 0))
    return pl.pallas_call(
        _out_proj_kernel,
        grid=(m // tm,),
        in_specs=[rows(k1), rows(k2), const((k1, n)), const((k2, n)), rows(n), const((1, n))],
        out_specs=[rows(n), rows(n)],
        out_shape=[jax.ShapeDtypeStruct((m, n), F32), jax.ShapeDtypeStruct((m, n), BF16)],
        compiler_params=_cparams("parallel"),
        name="out_proj",
    )(a1, a2, w[:k1], w[k1:], res, g_next.reshape(1, n))


def _mlp_kernel(h_ref, hn_ref, w1_ref, w2_ref, o_ref, acc_ref, *, nf):
    j = pl.program_id(1)

    @pl.when(j == 0)
    def _():
        acc_ref[...] = jnp.zeros_like(acc_ref)

    mid = jnp.dot(hn_ref[...], w1_ref[...], preferred_element_type=F32)
    mid = jnp.square(jnp.maximum(mid, 0.0)).astype(BF16)
    acc_ref[...] += jnp.dot(mid, w2_ref[...], preferred_element_type=F32)

    @pl.when(j == nf - 1)
    def _():
        o_ref[...] = h_ref[...] + acc_ref[...]


def _mlp_residual(h, hn, w1, w2, *, tm=512, tf=1024):
    m, d = h.shape
    f = w1.shape[1]
    tm, tf = min(tm, m), min(tf, f)
    assert m % tm == 0 and f % tf == 0
    nf = f // tf
    return pl.pallas_call(
        functools.partial(_mlp_kernel, nf=nf),
        grid=(m // tm, nf),
        in_specs=[pl.BlockSpec((tm, d), lambda i, j: (i, 0)),
                  pl.BlockSpec((tm, d), lambda i, j: (i, 0)),
                  pl.BlockSpec((d, tf), lambda i, j: (0, j)),
                  pl.BlockSpec((tf, d), lambda i, j: (j, 0))],
        out_specs=pl.BlockSpec((tm, d), lambda i, j: (i, 0)),
        out_shape=jax.ShapeDtypeStruct((m, d), F32),
        scratch_shapes=[pltpu.VMEM((tm, d), F32)],
        compiler_params=_cparams("parallel", "arbitrary"),
        name="mlp",
    )(h, hn, w1, w2)


def _gate_kernel(h_ref, g_ref, wg_ref, p_ref, wp_ref, gn_ref, *o_refs, final):
    h = h_ref[...]
    hn = _rmsnorm_rows(h, g_ref[...]).astype(BF16)
    gate = _sigmoid(jnp.dot(hn, wg_ref[...], preferred_element_type=F32))
    pp = jnp.dot(p_ref[...].astype(BF16), wp_ref[...], preferred_element_type=F32)
    ho = h + gate * pp
    nxt = _rmsnorm_rows(ho, gn_ref[...])
    if final:
        o_refs[0][...] = nxt
    else:
        o_refs[0][...] = ho
        o_refs[1][...] = nxt.astype(BF16)


def _gated_embed(h, g, wg, p, layer, wp, g_next, *, final, tm=512):
    m, k = h.shape
    kp = p.shape[2]
    tm = min(tm, m)
    assert m % tm == 0 and wg.shape[1:] == (k, k)
    const = lambda shape: pl.BlockSpec(shape, lambda i: (0, 0))
    rows = lambda w: pl.BlockSpec((tm, w), lambda i: (i, 0))
    out_specs = rows(k) if final else [rows(k), rows(k)]
    out_shape = (jax.ShapeDtypeStruct((m, k), F32) if final
                 else [jax.ShapeDtypeStruct((m, k), F32), jax.ShapeDtypeStruct((m, k), BF16)])
    return pl.pallas_call(
        functools.partial(_gate_kernel, final=final),
        grid=(m // tm,),
        in_specs=[rows(k), const((1, k)), pl.BlockSpec((None, k, k), lambda i: (layer, 0, 0)),
                  pl.BlockSpec((None, tm, kp), lambda i: (layer, i, 0)),
                  pl.BlockSpec((None, kp, k), lambda i: (layer, 0, 0)), const((1, k))],
        out_specs=out_specs,
        out_shape=out_shape,
        compiler_params=_cparams("parallel"),
        name="gated_embed",
    )(h, g.reshape(1, k), wg, p, wp, g_next.reshape(1, k))


def _s5_prep_kernel(lr_ref, li_ref, st_ref, btr_ref, bti_ref, b8_ref, a8_re_ref, a8_im_ref):
    lr, li, st = lr_ref[...], li_ref[...], jnp.exp(st_ref[...])
    lrs, lis = lr * st, li * st

    def apow(n):
        mag = jnp.exp(n * lrs)
        return mag * jnp.cos(n * lis), mag * jnp.sin(n * lis)

    mag = jnp.exp(lrs)
    abar_re, abar_im = mag * jnp.cos(lis), mag * jnp.sin(lis)
    den = lr * lr + li * li
    nr = abar_re - 1.0
    coef_re = (nr * lr + abar_im * li) / den
    coef_im = (abar_im * lr - nr * li) / den
    btr, bti = btr_ref[...], bti_ref[...]
    bbar_re = coef_re * btr - coef_im * bti
    bbar_im = coef_re * bti + coef_im * btr
    nblk, taps_ch, lanes2 = b8_ref.shape
    lane_blk = lanes2 // 2
    ch_blk = taps_ch // V7X_SUBLANES
    rg = lax.broadcasted_iota(jnp.int32, (ch_blk, lane_blk), 0) // S5_GROUP
    cg = lax.broadcasted_iota(jnp.int32, (ch_blk, lane_blk), 1) // S5_STATE
    same = rg == cg
    for s in range(V7X_SUBLANES):
        if s == 0:
            br, bi = bbar_re, bbar_im
        else:
            pr, pi = apow(float(s))
            br, bi = pr * bbar_re - pi * bbar_im, pr * bbar_im + pi * bbar_re
        for j in range(nblk):
            ls = slice(j * lane_blk, (j + 1) * lane_blk)
            rs = slice(s * ch_blk, (s + 1) * ch_blk)
            tile = lambda m: jnp.where(same, jnp.concatenate([m[:, ls]] * (ch_blk // S5_GROUP), axis=0), 0.0)
            b8_ref[j, rs, :lane_blk] = tile(br).astype(BF16)
            b8_ref[j, rs, lane_blk:] = tile(bi).astype(BF16)
    a8r, a8i = apow(float(V7X_SUBLANES))
    a8_re_ref[...] = jnp.broadcast_to(a8r, a8_re_ref.shape)
    a8_im_ref[...] = jnp.broadcast_to(a8i, a8_im_ref.shape)


def _s5_kernel(u_ref, b8_ref, cre_ref, cim_ref, a8r_ref, a8i_ref, d_ref, gw_ref, gb_ref, o_ref,
               ue_ref, car_re, car_im):
    @pl.when(pl.program_id(1) == 0)
    def _():
        ue_ref[...] = jnp.zeros_like(ue_ref)
        car_re[...] = jnp.zeros_like(car_re)
        car_im[...] = jnp.zeros_like(car_im)

    u = u_ref[...]
    tt = u.shape[0]
    nblk, taps_ch, lanes2 = b8_ref.shape
    lane_blk = lanes2 // 2
    ch_blk = taps_ch // V7X_SUBLANES
    taps = [u] + _delayed(u, ue_ref, range(1, V7X_SUBLANES))
    ys = []
    for j in range(nblk):
        ls = slice(j * lane_blk, (j + 1) * lane_blk)
        cs = slice(j * ch_blk, (j + 1) * ch_blk)
        u8 = jnp.concatenate([t[:, cs].astype(BF16) for t in taps], axis=1)
        z = jnp.dot(u8, b8_ref[j], preferred_element_type=F32)
        pr, pi = a8r_ref[:, ls], a8i_ref[:, ls]
        cr, ci = car_re[:, ls], car_im[:, ls]
        xr, xi = [], []
        for i in range(tt // V7X_SUBLANES):
            rs = slice(i * V7X_SUBLANES, (i + 1) * V7X_SUBLANES)
            cr, ci = (z[rs, :lane_blk] + (pr * cr - pi * ci), z[rs, lane_blk:] + (pr * ci + pi * cr))
            xr.append(cr)
            xi.append(ci)
        car_re[:, ls] = cr
        car_im[:, ls] = ci
        ys.append(_bdot(jnp.concatenate(xr, axis=0), cre_ref[j]) - _bdot(jnp.concatenate(xi, axis=0), cim_ref[j]))
    y = jnp.concatenate(ys, axis=1) + d_ref[...] * u
    act = _gelu(y)
    o_ref[...] = (act * _sigmoid(_bdot(act, gw_ref[...]) + gb_ref[...])).astype(o_ref.dtype)


def _s5_mixer(proj, col_blk, bsz, seq, lam_re, lam_im, log_step, b_re, b_im, c_re, c_im,
              d_skip, glu_w, glu_b, cast_w, cast_layer, *, tt=256, lane_blk=512):
    ng, ns = lam_re.shape
    width = ng * S5_GROUP
    nstate = ng * ns
    tt = min(tt, seq)
    assert seq % tt == 0 and tt % V7X_SUBLANES == 0 and nstate % lane_blk == 0 and lane_blk % ns == 0
    nblk = nstate // lane_blk
    gpb = lane_blk // ns
    ch_blk = gpb * S5_GROUP
    flat = lambda a: a.astype(F32).reshape(1, nstate)
    step = jnp.broadcast_to(log_step.astype(F32)[:, None], (ng, ns))
    bt = lambda a: a.astype(F32).transpose(2, 0, 1).reshape(S5_GROUP, nstate)
    vm = pl.BlockSpec(memory_space=pltpu.VMEM)
    b8, a8_re, a8_im = pl.pallas_call(
        _s5_prep_kernel,
        in_specs=[vm] * 5,
        out_specs=[vm] * 3,
        out_shape=[jax.ShapeDtypeStruct((nblk, V7X_SUBLANES * ch_blk, 2 * lane_blk), BF16)]
        + [jax.ShapeDtypeStruct((V7X_SUBLANES, nstate), F32)] * 2,
        compiler_params=pltpu.CompilerParams(vmem_limit_bytes=VMEM_LIMIT_BYTES),
        name="s5_prep",
    )(flat(lam_re), flat(lam_im), flat(step), bt(b_re), bt(b_im))
    eye = jnp.eye(gpb, dtype=F32)
    cbd = lambda c: (c.astype(F32).reshape(nblk, gpb, S5_GROUP, ns).transpose(0, 1, 3, 2)[:, :, :, None, :]
                     * eye[None, :, None, :, None]).reshape(nblk, lane_blk, ch_blk).astype(BF16)
    nt = seq // tt
    cast_in, cast_out, cast_shape = _cast_slab(cast_w, cast_layer, bsz, nt)
    const = lambda shape: pl.BlockSpec(shape, lambda b, t: (0,) * len(shape))
    return pl.pallas_call(
        _with_weight_cast(_s5_kernel, 9),
        grid=(bsz, nt),
        in_specs=[pl.BlockSpec((tt, width), lambda b, t: (b * nt + t, col_blk)),
                  const((nblk, V7X_SUBLANES * ch_blk, 2 * lane_blk)),
                  const((nblk, lane_blk, ch_blk)), const((nblk, lane_blk, ch_blk)),
                  const((V7X_SUBLANES, nstate)), const((V7X_SUBLANES, nstate)),
                  const((1, width)), const((width, width)), const((1, width)), cast_in],
        out_specs=[pl.BlockSpec((tt, width), lambda b, t: (b * nt + t, 0)), cast_out],
        out_shape=[jax.ShapeDtypeStruct((bsz * seq, width), BF16), cast_shape],
        scratch_shapes=[pltpu.VMEM((V7X_SUBLANES, width), F32)]
        + [pltpu.VMEM((V7X_SUBLANES, nstate), F32)] * 2,
        compiler_params=_cparams("parallel", "arbitrary"),
        name="s5_mixer",
    )(proj, b8, cbd(c_re), cbd(c_im), a8_re, a8_im,
      d_skip.astype(F32).reshape(1, width), glu_w.astype(BF16), glu_b.astype(F32).reshape(1, width), cast_w)


def _ssd_kernel(z_ref, xs_ref, bc_ref, dt_ref, cwx_ref, cbx_ref, cwb_ref, cbb_ref, e64_ref,
                dtb_ref, alog_ref, dskx_ref, ng_ref, o_ref,
                tailx, tailb, st_ref):
    @pl.when(pl.program_id(1) == 0)
    def _():
        tailx[...] = jnp.zeros_like(tailx)
        tailb[...] = jnp.zeros_like(tailb)
        st_ref[...] = jnp.zeros_like(st_ref)

    L = SSD_CHUNK
    tt = xs_ref.shape[0]
    width = xs_ref.shape[1]
    gstate = SSD_GROUPS * SSD_STATE
    xs = _silu(_causal_conv(xs_ref[...], tailx, cwx_ref, cbx_ref))
    bc = _silu(_causal_conv(bc_ref[...], tailb, cwb_ref, cbb_ref))
    dt = _softplus(dt_ref[...] + dtb_ref[...])
    da = dt * (-jnp.exp(alog_ref[...]))
    e64 = e64_ref[...]
    xdt = xs * _dot_rhs01(dt, e64)
    ri = lax.broadcasted_iota(jnp.int32, (L, L), 0)
    ci = lax.broadcasted_iota(jnp.int32, (L, L), 1)
    tril = ri >= ci
    tri01 = tril.astype(BF16)
    lane_lo = lax.broadcasted_iota(jnp.int32, (1, V7X_LANES), 1) < SSD_HEAD_DIM
    pairs_per_group = SSD_HEADS // SSD_GROUPS // 2
    ys = []
    for c in range(tt // L):
        rs = slice(c * L, (c + 1) * L)
        acum = _dot_lhs01(tri01, da[rs, :])
        acum_t = acum.T
        acx = _dot_rhs01(acum, e64)
        ycols = []
        for g in range(SSD_GROUPS):
            bg = bc[rs, g * SSD_STATE:(g + 1) * SSD_STATE]
            cg = bc[rs, gstate + g * SSD_STATE:gstate + (g + 1) * SSD_STATE]
            scores = _bdot_nt(cg, bg).astype(BF16)
            bg_t = bg.T
            for pr in range(pairs_per_group):
                q = g * pairs_per_group + pr
                ls = slice(q * V7X_LANES, (q + 1) * V7X_LANES)
                xp = xdt[rs, ls]
                acp = acx[:, ls]
                yd = []
                for h in (2 * q, 2 * q + 1):
                    seg = acum[:, h:h + 1] - acum_t[h:h + 1, :]
                    decay = jnp.exp(jnp.where(tril, seg, -jnp.inf).astype(BF16))
                    yd.append(jnp.dot(scores * decay, xp.astype(BF16), preferred_element_type=F32))
                y = jnp.where(lane_lo, yd[0], yd[1])
                prev_t = st_ref[q]
                y = y + _bdot(cg, prev_t) * jnp.exp(acp)
                last = acp[L - 1:L, :]
                st_ref[q] = prev_t * jnp.exp(last) + _bdot(bg_t, xp * jnp.exp(last - acp))
                ycols.append(y)
        ys.append(jnp.concatenate(ycols, axis=1))
    y = jnp.concatenate(ys, axis=0) if len(ys) > 1 else ys[0]
    y = (y + xs * dskx_ref[...]) * _silu(z_ref[...])
    gw = width // SSD_GROUPS
    outs = []
    for g in range(SSD_GROUPS):
        yg = y[:, g * gw:(g + 1) * gw]
        outs.append(yg * lax.rsqrt(jnp.mean(yg * yg, axis=-1, keepdims=True) + NORM_EPS))
    o_ref[...] = (jnp.concatenate(outs, axis=1) * ng_ref[...]).astype(o_ref.dtype)


def _ssd_mixer(proj, blk_z, blk_xs, blk_bc, blk_dt, bsz, seq, conv_w, conv_b, dt_bias, a_log,
               d_skip, norm_g, cast_w, cast_layer, *, tt=256):
    width = SSD_HEADS * SSD_HEAD_DIM
    gstate2 = 2 * SSD_GROUPS * SSD_STATE
    tt = min(tt, seq)
    assert seq % tt == 0 and tt % SSD_CHUNK == 0
    nt = seq // tt
    cast_in, cast_out, cast_shape = _cast_slab(cast_w, cast_layer, bsz, nt)
    f = lambda a: a.astype(F32)
    pad_h = lambda a: jnp.pad(f(a), (0, V7X_LANES - SSD_HEADS)).reshape(1, V7X_LANES)
    exp_h = lambda a: jnp.repeat(f(a), SSD_HEAD_DIM).reshape(1, width)
    e64 = (jnp.arange(V7X_LANES)[:, None] == (jnp.arange(width)[None, :] // SSD_HEAD_DIM)).astype(BF16)
    const = lambda shape: pl.BlockSpec(shape, lambda b, t: (0,) * len(shape))
    rowblk = lambda w, cb: pl.BlockSpec((tt, w), lambda b, t: (b * nt + t, cb))
    return pl.pallas_call(
        _with_weight_cast(_ssd_kernel, 13),
        grid=(bsz, nt),
        in_specs=[rowblk(width, blk_z), rowblk(width, blk_xs), rowblk(gstate2, blk_bc),
                  rowblk(V7X_LANES, blk_dt),
                  const((CONV_K, width)), const((1, width)), const((CONV_K, gstate2)), const((1, gstate2)),
                  const((V7X_LANES, width)),
                  const((1, V7X_LANES)), const((1, V7X_LANES)),
                  const((1, width)), const((1, width)), cast_in],
        out_specs=[pl.BlockSpec((tt, width), lambda b, t: (b * nt + t, 0)), cast_out],
        out_shape=[jax.ShapeDtypeStruct((bsz * seq, width), BF16), cast_shape],
        scratch_shapes=[pltpu.VMEM((V7X_SUBLANES, width), F32),
                        pltpu.VMEM((V7X_SUBLANES, gstate2), F32),
                        pltpu.VMEM((SSD_HEADS // 2, SSD_STATE, V7X_LANES), F32)],
        compiler_params=_cparams("parallel", "arbitrary"),
        name="ssd_mixer",
    )(proj, proj, proj, proj,
      f(conv_w[:, :width]), f(conv_b[:width]).reshape(1, width),
      f(conv_w[:, width:]), f(conv_b[width:]).reshape(1, gstate2),
      e64, pad_h(dt_bias), pad_h(a_log), exp_h(d_skip),
      f(norm_g).reshape(1, width), cast_w)


def _stack_heads(x, lane_lo):
    return jnp.concatenate([jnp.where(lane_lo, x, 0.0), jnp.where(lane_lo, 0.0, x)], axis=0)


def _unit_lower_inverses(mats, same_blk, eye):
    size = eye.shape[0]
    side = lambda x, y: _bdot(x, jnp.concatenate([x, y], axis=1))
    ad = [jnp.where(same_blk, a, 0.0) for a in mats]
    ao = [a - d for a, d in zip(mats, ad)]
    s1 = [eye + d for d in ad]
    a2 = [_bdot(d, d) for d in ad]
    r = [side(x, s) for x, s in zip(a2, s1)]
    a4, s2 = [x[:, :size] for x in r], [s + x[:, size:] for s, x in zip(s1, r)]
    r = [side(x, s) for x, s in zip(a4, s2)]
    a8, s3 = [x[:, :size] for x in r], [s + x[:, size:] for s, x in zip(s2, r)]
    td = [s + _bdot(x, s) for x, s in zip(a8, s3)]
    n = [_bdot(t, o) for t, o in zip(td, ao)]
    r = [side(x, t) for x, t in zip(n, td)]
    n2, w1 = [x[:, :size] for x in r], [t + x[:, size:] for t, x in zip(td, r)]
    return [w + _bdot(x, w) for x, w in zip(n2, w1)]


def _rwkv_kernel(rkv_ref, lo_ref, mu_rkv_ref, mu_lo_ref, w0_ref, wup_ref, a0_ref, aup_ref, gup_ref,
                 kk_ref, ka_ref, rk_ref, lng_ref, lnb_ref, ones_ref, o_ref,
                 prev_rkv, prev_lo, state_ref):
    @pl.when(pl.program_id(1) == 0)
    def _():
        prev_rkv[...] = jnp.zeros_like(prev_rkv)
        prev_lo[...] = jnp.zeros_like(prev_lo)
        state_ref[...] = jnp.zeros_like(state_ref)

    L = RWKV_CHUNK
    tb = rkv_ref.shape[0]
    width = rkv_ref.shape[1] // 3

    def shift_mix(ref, prev, mu_ref):
        f = ref[...]
        fs = jnp.where(_row_iota(f.shape) == 0, prev[0:1, :], pltpu.roll(f, 1, 0))
        prev[0:1, :] = f[tb - 1:tb, :]
        return f + (fs - f) * mu_ref[...]

    rkv = shift_mix(rkv_ref, prev_rkv, mu_rkv_ref)
    lo = shift_mix(lo_ref, prev_lo, mu_lo_ref)
    r, k, v = rkv[:, :width], rkv[:, width:2 * width], rkv[:, 2 * width:]
    nl = wup_ref.shape[0]
    wl, al, gl = lo[:, :nl], lo[:, nl:2 * nl], lo[:, 2 * nl:]
    w = -_softplus(-(w0_ref[...] + _bdot(jnp.tanh(wl), wup_ref[...]))) - 0.5
    logw = -jnp.exp(w)
    a_sig = _sigmoid(a0_ref[...] + _bdot(al, aup_ref[...]))
    g = _bdot(_sigmoid(gl), gup_ref[...])
    kk = k * kk_ref[...]
    k = k * (1.0 + (a_sig - 1.0) * ka_ref[...])
    ones_bd = ones_ref[...]

    def head_sum(x):
        return jnp.dot(x.astype(BF16), ones_bd, preferred_element_type=F32)

    P2 = 2 * L
    ri = lax.broadcasted_iota(jnp.int32, (P2, P2), 0)
    ci = lax.broadcasted_iota(jnp.int32, (P2, P2), 1)
    same_head = (ri // L) == (ci // L)
    strict = same_head & (ri > ci)
    incl = same_head & (ri >= ci)
    same_blk = (ri // RWKV_INV_BLOCK) == (ci // RWKV_INV_BLOCK)
    eye = (ri == ci).astype(F32)
    tri01 = (lax.broadcasted_iota(jnp.int32, (L, L), 0) >= lax.broadcasted_iota(jnp.int32, (L, L), 1)).astype(BF16)
    lane_lo = lax.broadcasted_iota(jnp.int32, (1, V7X_LANES), 1) < RWKV_HEAD_DIM

    nq = width // V7X_LANES
    lss = [slice(q * V7X_LANES, (q + 1) * V7X_LANES) for q in range(nq)]
    kks = [kk[:, ls] for ls in lss]
    kkn = [x * lax.rsqrt(jnp.maximum(head_sum(x * x), 1e-24)) for x in kks]
    nc = tb // L
    inst = [(slice(c * L, (c + 1) * L), q, ls) for c in range(nc) for q, ls in enumerate(lss)]
    cums = [_dot_lhs01(tri01, logw[c * L:(c + 1) * L, :]) for c in range(nc)]
    g_in = jnp.concatenate([jnp.exp(cm) for cm in cums], axis=0)
    g_prev = jnp.concatenate([jnp.exp(cm - logw[c * L:(c + 1) * L, :]) for c, cm in enumerate(cums)], axis=0)
    g_inv = jnp.concatenate([jnp.exp(-cm) for cm in cums], axis=0)
    ats = [_stack_heads(-kkn[q][rs] * g_prev[rs, ls], lane_lo) for rs, q, ls in inst]
    rts = [_stack_heads(r[rs, ls] * g_in[rs, ls], lane_lo) for rs, q, ls in inst]
    bts = [_stack_heads(kkn[q][rs] * a_sig[rs, ls] * g_inv[rs, ls], lane_lo) for rs, q, ls in inst]
    kts = [_stack_heads(k[rs, ls] * g_inv[rs, ls], lane_lo) for rs, q, ls in inst]
    vss = [_stack_heads(v[rs, ls], lane_lo).astype(BF16) for rs, q, ls in inst]
    ars = [jnp.concatenate([a, b], axis=0).astype(BF16) for a, b in zip(ats, rts)]
    ps = [_bdot_nt(ar, jnp.concatenate([b, kq], axis=0)) for ar, b, kq in zip(ars, bts, kts)]
    tinv = _unit_lower_inverses([jnp.where(strict, p[:P2, :P2], 0.0) for p in ps], same_blk, eye)
    akv = [_bdot(jnp.where(strict, p[:P2, P2:], 0.0), vs) for p, vs in zip(ps, vss)]
    a_r = [jnp.concatenate([jnp.where(incl, p[P2:, :P2], 0.0), jnp.where(incl, p[P2:, P2:], 0.0)],
                           axis=1).astype(BF16) for p in ps]
    g_last = [g_in[rs, ls][L - 1:L, :] for rs, q, ls in inst]
    bkl = [(jnp.concatenate([b, kq], axis=0) * gl_).astype(BF16) for b, kq, gl_ in zip(bts, kts, g_last)]
    hts = [state_ref[q] for q in range(nq)]
    yrows = []
    for c in range(nc):
        sel = lambda xs: xs[c * nq:(c + 1) * nq]
        arh = [_bdot_nt(ar, ht) for ar, ht in zip(sel(ars), hts)]
        us = [_bdot(t, x[:P2] + y) for t, x, y in zip(sel(tinv), arh, sel(akv))]
        uv = [jnp.concatenate([u.astype(BF16), vs], axis=0) for u, vs in zip(us, sel(vss))]
        hts = [ht * gl_ + _bdot_tn(w, b) for ht, gl_, w, b in zip(hts, sel(g_last), uv, sel(bkl))]
        ys = [x[P2:] + jnp.dot(m, w, preferred_element_type=F32) for x, m, w in zip(arh, sel(a_r), uv)]
        yrows.append(jnp.concatenate([y[:L] + y[L:] for y in ys], axis=1))
    for q in range(nq):
        state_ref[q] = hts[q]
    y = jnp.concatenate(yrows, axis=0) if len(yrows) > 1 else yrows[0]
    inv_n = 1.0 / RWKV_HEAD_DIM
    outs = []
    for q, ls in enumerate(lss):
        yq = y[:, ls]
        mean = head_sum(yq) * inv_n
        yc = yq - mean
        var = head_sum(yc * yc) * inv_n
        yn = yc * lax.rsqrt(var + RWKV_GN_EPS) * lng_ref[:, ls] + lnb_ref[:, ls]
        bonus = head_sum(r[:, ls] * k[:, ls] * rk_ref[:, ls]) * v[:, ls]
        outs.append((yn + bonus) * g[:, ls])
    o_ref[...] = jnp.concatenate(outs, axis=1).astype(o_ref.dtype)


def _rwkv_mixer(proj, blk_rkv, blk_lo, bsz, seq, mu_rkv, mu_lo, w0, w_up, a0, a_up, g_up,
                k_k, k_a, r_k, ln_g, ln_b, cast_w, cast_layer, *, tb=256):
    width = w0.shape[0]
    nl = w_up.shape[0]
    tb = min(tb, seq)
    assert seq % tb == 0 and tb % RWKV_CHUNK == 0
    nt = seq // tb
    cast_in, cast_out, cast_shape = _cast_slab(cast_w, cast_layer, bsz, nt)
    f = lambda a: a.astype(F32).reshape(1, -1)
    lane = jnp.arange(V7X_LANES) // RWKV_HEAD_DIM
    ones_bd = (lane[:, None] == lane[None, :]).astype(BF16)
    const = lambda shape: pl.BlockSpec(shape, lambda b, t: (0,) * len(shape))
    return pl.pallas_call(
        _with_weight_cast(_rwkv_kernel, 15),
        grid=(bsz, nt),
        in_specs=[pl.BlockSpec((tb, 3 * width), lambda b, t: (b * nt + t, blk_rkv)),
                  pl.BlockSpec((tb, 4 * nl), lambda b, t: (b * nt + t, blk_lo)),
                  const((1, 3 * width)), const((1, 4 * nl)),
                  const((1, width)), const((nl, width)), const((1, width)), const((nl, width)),
                  const((2 * nl, width)),
                  const((1, width)), const((1, width)), const((1, width)), const((1, width)),
                  const((1, width)), const((V7X_LANES, V7X_LANES)), cast_in],
        out_specs=[pl.BlockSpec((tb, width), lambda b, t: (b * nt + t, 0)), cast_out],
        out_shape=[jax.ShapeDtypeStruct((bsz * seq, width), BF16), cast_shape],
        scratch_shapes=[pltpu.VMEM((V7X_SUBLANES, 3 * width), F32),
                        pltpu.VMEM((V7X_SUBLANES, 4 * nl), F32),
                        pltpu.VMEM((width // V7X_LANES, V7X_LANES, V7X_LANES), F32)],
        compiler_params=_cparams("parallel", "arbitrary"),
        name="rwkv_mixer",
    )(proj, proj, mu_rkv, mu_lo, f(w0), w_up.astype(BF16), f(a0), a_up.astype(BF16),
      g_up.astype(BF16), f(k_k), f(k_a), f(r_k), f(ln_g), f(ln_b), ones_bd, cast_w)


def _lru_kernel(xl_ref, gl_ref, cw_ref, cb_ref, wa_ref, ba_ref, wx_ref, bx_ref, lam_ref, o_ref,
                tail, carry):
    @pl.when(pl.program_id(1) == 0)
    def _():
        tail[...] = jnp.zeros_like(tail)
        carry[...] = jnp.zeros_like(carry)

    tt, width = xl_ref.shape
    row = _row_iota((tt, width))
    xc = _causal_conv(xl_ref[...], tail, cw_ref, cb_ref)
    gr, gi = [], []
    for j in range(width // V7X_LANES):
        xj = xc[:, j * V7X_LANES:(j + 1) * V7X_LANES]
        gr.append(_bdot(xj, wa_ref[j]))
        gi.append(_bdot(xj, wx_ref[j]))
    gate_r = _sigmoid(jnp.concatenate(gr, axis=1) + ba_ref[...])
    gate_i = _sigmoid(jnp.concatenate(gi, axis=1) + bx_ref[...])
    log_a = -LRU_C * gate_r * _softplus(-lam_ref[...])
    a = jnp.exp(log_a)
    mult = jnp.sqrt(jnp.maximum(-(jnp.tanh(log_a) * (a * a + 1.0)), 0.0))
    mult = jnp.where(row + pl.program_id(1) * tt == 0, 1.0, mult)
    b = xc * gate_i * mult
    nt = tt // V7X_SUBLANES
    a, b = a.reshape(nt, V7X_SUBLANES, width), b.reshape(nt, V7X_SUBLANES, width)
    sub = lax.broadcasted_iota(jnp.int32, a.shape, 1)
    for s in (1, 2, 4):
        keep = sub >= s
        sa, sb = pltpu.roll(a, s, 1), pltpu.roll(b, s, 1)
        b = b + jnp.where(keep, a * sb, 0.0)
        a = jnp.where(keep, a * sa, a)
    a, b = a.reshape(tt, width), b.reshape(tt, width)
    c = carry[0:1, :]
    hs = []
    for i in range(tt // V7X_SUBLANES):
        rs = slice(i * V7X_SUBLANES, (i + 1) * V7X_SUBLANES)
        hi = b[rs, :] + a[rs, :] * c
        hs.append(hi)
        c = hi[V7X_SUBLANES - 1:, :]
    carry[0:1, :] = c
    h = jnp.concatenate(hs, axis=0)
    o_ref[...] = (h * _gelu(gl_ref[...])).astype(o_ref.dtype)


def _lru_mixer(proj, blk_xl, blk_gl, bsz, seq, conv_w, conv_b, w_a, b_a, w_x, b_x, lam, cast_w, cast_layer,
               *, tt=256):
    nb, blk, _ = w_a.shape
    width = nb * blk
    tt = min(tt, seq)
    assert seq % tt == 0 and tt % V7X_SUBLANES == 0
    nt = seq // tt
    cast_in, cast_out, cast_shape = _cast_slab(cast_w, cast_layer, bsz, nt)
    per = V7X_LANES // blk
    eye = jnp.eye(per, dtype=F32)
    bd = lambda w: (w.astype(F32).reshape(nb // per, per, blk, 1, blk) * eye[None, :, None, :, None]
                    ).reshape(nb // per, V7X_LANES, V7X_LANES).astype(BF16)
    f = lambda a: a.astype(F32).reshape(1, width)
    const = lambda shape: pl.BlockSpec(shape, lambda b, t: (0,) * len(shape))
    return pl.pallas_call(
        _with_weight_cast(_lru_kernel, 9),
        grid=(bsz, nt),
        in_specs=[pl.BlockSpec((tt, width), lambda b, t: (b * nt + t, blk_xl)),
                  pl.BlockSpec((tt, width), lambda b, t: (b * nt + t, blk_gl)),
                  const((CONV_K, width)), const((1, width)),
                  const((nb // per, V7X_LANES, V7X_LANES)), const((1, width)),
                  const((nb // per, V7X_LANES, V7X_LANES)), const((1, width)), const((1, width)), cast_in],
        out_specs=[pl.BlockSpec((tt, width), lambda b, t: (b * nt + t, 0)), cast_out],
        out_shape=[jax.ShapeDtypeStruct((bsz * seq, width), BF16), cast_shape],
        scratch_shapes=[pltpu.VMEM((V7X_SUBLANES, width), F32), pltpu.VMEM((V7X_SUBLANES, width), F32)],
        compiler_params=_cparams("parallel", "arbitrary"),
        name="lru_mixer",
    )(proj, proj, conv_w.astype(F32), f(conv_b), bd(w_a), f(b_a), bd(w_x), f(b_x), f(lam), cast_w)


def _pad_cols(w, n):
    return jnp.pad(w, ((0, 0), (0, n - w.shape[1])))


def _even_mixer(h, hn, bsz, seq, layer, mlp_w1, mlp_w2, norm_g, norm_out, in_proj, out_proj,
                lam_re, lam_im, log_step, b_re, b_im, c_re, c_im,
                s5_d, glu_w, glu_b, conv_w, conv_b, dt_bias, a_log, ssd_d, ssd_norm):
    s5w = lam_re.shape[0] * S5_GROUP
    ssdw = SSD_HEADS * SSD_HEAD_DIM
    gstate2 = 2 * SSD_GROUPS * SSD_STATE
    u_w, z_w, xbc_w, dt_w = jnp.split(in_proj, [s5w, s5w + ssdw, s5w + 2 * ssdw + gstate2], axis=1)
    w_all = jnp.concatenate([z_w, xbc_w, u_w, dt_w], axis=1)
    w_all = _pad_cols(w_all, -(-w_all.shape[1] // 512) * 512).astype(BF16)
    proj = _in_proj(h, hn, norm_g, w_all)
    off_bc, off_u, off_dt = 2 * ssdw, 2 * ssdw + gstate2, 2 * ssdw + gstate2 + s5w
    y_a, w1 = _s5_mixer(proj, off_u // s5w, bsz, seq, lam_re, lam_im, log_step, b_re, b_im, c_re, c_im,
                        s5_d, glu_w, glu_b, mlp_w1, layer)
    y_b, w2 = _ssd_mixer(proj, 0, 1, off_bc // gstate2, off_dt // V7X_LANES, bsz, seq,
                         conv_w, conv_b, dt_bias, a_log, ssd_d, ssd_norm, mlp_w2, layer)
    return [*_out_proj_residual(y_a, y_b, out_proj.astype(BF16), h, norm_out), w1, w2]


def _odd_mixer(h, hn, bsz, seq, layer, mlp_w1, mlp_w2, norm_g, norm_out, in_proj, out_proj,
               mu, w0, w_up, a0, a_up, g_up, k_k, k_a, r_k,
               ln_g, ln_b, conv_w, conv_b, w_a, b_a, w_x, b_x, lam):
    rw = w0.shape[0]
    nl = w_up.shape[0]
    ngl = g_up.shape[0]
    lw = lam.shape[0] * lam.shape[1]
    nlp = V7X_LANES
    assert nl <= nlp and ngl == 2 * nlp
    rkv_w, wl_w, al_w, gl_w, xl_w, g2_w = jnp.split(
        in_proj, [3 * rw, 3 * rw + nl, 3 * rw + 2 * nl, 3 * rw + 2 * nl + ngl, 3 * rw + 2 * nl + ngl + lw], axis=1)
    w_all = jnp.concatenate([rkv_w, xl_w, g2_w, _pad_cols(wl_w, nlp), _pad_cols(al_w, nlp), gl_w],
                            axis=1).astype(BF16)
    mu = mu.astype(F32)
    mu_rkv = mu[:3 * rw].reshape(1, -1)
    padv = lambda v: jnp.pad(v, (0, nlp - v.shape[0]))
    mu_lo = jnp.concatenate([padv(mu[3 * rw:3 * rw + nl]), padv(mu[3 * rw + nl:3 * rw + 2 * nl]),
                             mu[3 * rw + 2 * nl:]]).reshape(1, -1)
    padr = lambda w: jnp.pad(w, ((0, nlp - w.shape[0]), (0, 0)))
    proj = _in_proj(h, hn, norm_g, w_all)
    off_xl, off_gl, off_lo = 3 * rw, 3 * rw + lw, 3 * rw + 2 * lw
    y_c, w1 = _rwkv_mixer(proj, 0, off_lo // (4 * nlp), bsz, seq, mu_rkv, mu_lo, w0, padr(w_up), a0, padr(a_up),
                          g_up, k_k, k_a, r_k.reshape(-1), ln_g, ln_b, mlp_w1, layer)
    y_d, w2 = _lru_mixer(proj, off_xl // lw, off_gl // lw, bsz, seq, conv_w, conv_b, w_a, b_a, w_x, b_x, lam,
                         mlp_w2, layer)
    return [*_out_proj_residual(y_c, y_d, out_proj.astype(BF16), h, norm_out), w1, w2]


def kernel(x, p, norm_mix, norm_ffn, norm_pl, mlp_w1, mlp_w2, pl_proj, pl_gate, e_in_proj, e_out_proj, s5_lam_re, s5_lam_im, s5_log_step, s5_b_re, s5_b_im, s5_c_re, s5_c_im, s5_d, s5_glu_w, s5_glu_b, ssd_conv_w, ssd_conv_b, ssd_dt_bias, ssd_a_log, ssd_d, ssd_norm, o_in_proj, o_out_proj, rwkv_mu, rwkv_w0, rwkv_w_up, rwkv_a0, rwkv_a_up, rwkv_g_up, rwkv_k_k, rwkv_k_a, rwkv_r_k, rwkv_ln_g, rwkv_ln_b, lru_conv_w, lru_conv_b, lru_w_a, lru_b_a, lru_w_x, lru_b_x, lru_lam, norm_final):
    bsz, seq, d = x.shape
    depth = p.shape[0]
    h = x.astype(F32).reshape(bsz * seq, d)
    hn = None
    wg_all, wp_all = pl_gate.astype(BF16), pl_proj.astype(BF16)
    p_all = p.reshape(depth, bsz * seq, -1)
    mlp_w1, mlp_w2 = mlp_w1.astype(F32), mlp_w2.astype(F32)
    for i in range(depth):
        j = i // 2
        common = (h, hn, bsz, seq, i, mlp_w1, mlp_w2, norm_mix[i], norm_ffn[i])
        if i % 2 == 0:
            h, hn, w1, w2 = _even_mixer(*common, e_in_proj[j], e_out_proj[j], s5_lam_re[j], s5_lam_im[j],
                                        s5_log_step[j], s5_b_re[j], s5_b_im[j], s5_c_re[j], s5_c_im[j], s5_d[j],
                                        s5_glu_w[j], s5_glu_b[j], ssd_conv_w[j], ssd_conv_b[j], ssd_dt_bias[j],
                                        ssd_a_log[j], ssd_d[j], ssd_norm[j])
        else:
            h, hn, w1, w2 = _odd_mixer(*common, o_in_proj[j], o_out_proj[j], rwkv_mu[j], rwkv_w0[j],
                                       rwkv_w_up[j], rwkv_a0[j], rwkv_a_up[j], rwkv_g_up[j], rwkv_k_k[j], rwkv_k_a[j],
                                       rwkv_r_k[j], rwkv_ln_g[j], rwkv_ln_b[j], lru_conv_w[j], lru_conv_b[j],
                                       lru_w_a[j], lru_b_a[j], lru_w_x[j], lru_b_x[j], lru_lam[j])
        h = _mlp_residual(h, hn, w1, w2)
        gate_args = (h, norm_pl[i], wg_all, p_all, i, wp_all)
        if i + 1 < depth:
            h, hn = _gated_embed(*gate_args, norm_mix[i + 1], final=False)
        else:
            h = _gated_embed(*gate_args, norm_final, final=True)
    return h.reshape(bsz, seq, d)
```

```python
import functools

import jax
import jax.numpy as jnp
from jax import lax
from jax.experimental import pallas as pl
from jax.experimental.pallas import tpu as pltpu

F32 = jnp.float32
BF16 = jnp.bfloat16

V7X_LANES = 128
V7X_SUBLANES = 8
V7X_BF16_SUBLANES = 16
V7X_VMEM_BYTES = 64 * 1024 * 1024
VMEM_LIMIT_BYTES = V7X_VMEM_BYTES - 8 * 1024 * 1024

NORM_EPS = 1e-6
S5_GROUP = 16
S5_STATE = 64
SSD_HEAD_DIM = 64
SSD_HEADS = 24
SSD_GROUPS = 4
SSD_STATE = 128
SSD_CHUNK = 128
CONV_K = 4
RWKV_HEAD_DIM = 64
RWKV_GN_EPS = 64e-5
RWKV_CHUNK = 64
RWKV_INV_BLOCK = 16
LRU_BLOCK = 64
LRU_C = 8.0


def _cparams(*sem):
    return pltpu.CompilerParams(dimension_semantics=sem, vmem_limit_bytes=VMEM_LIMIT_BYTES)


def _bdot(a, b):
    return jnp.dot(a.astype(BF16), b.astype(BF16), preferred_element_type=F32)


def _bdot_nt(a, b):
    return lax.dot_general(a.astype(BF16), b.astype(BF16), (((1,), (1,)), ((), ())),
                           preferred_element_type=F32)


def _bdot_tn(a, b):
    return lax.dot_general(a.astype(BF16), b.astype(BF16), (((0,), (0,)), ((), ())),
                           preferred_element_type=F32)


def _split3(x):
    x1 = x.astype(BF16)
    r1 = x - x1.astype(F32)
    x2 = r1.astype(BF16)
    x3 = (r1 - x2.astype(F32)).astype(BF16)
    return x1, x2, x3


def _dot_lhs01(m01, x):
    m = m01.astype(BF16)
    x1, x2, x3 = _split3(x)
    d = lambda v: jnp.dot(m, v, preferred_element_type=F32)
    return d(x1) + d(x2) + d(x3)


def _dot_rhs01(x, m01):
    m = m01.astype(BF16)
    x1, x2, x3 = _split3(x)
    d = lambda v: jnp.dot(v, m, preferred_element_type=F32)
    return d(x1) + d(x2) + d(x3)


def _softplus(x):
    return jnp.maximum(x, 0.0) + jnp.log1p(jnp.exp(-jnp.abs(x)))


def _sigmoid(x):
    return jax.nn.sigmoid(x)


def _gelu(x):
    return jax.nn.gelu(x, approximate=True)


def _silu(x):
    return x * _sigmoid(x)


def _row_iota(shape):
    return lax.broadcasted_iota(jnp.int32, shape, 0)


def _delayed(x, tail_ref, delays):
    rows, c = x.shape
    nt = rows // V7X_SUBLANES
    x3 = x.reshape(nt, V7X_SUBLANES, c)
    tail = tail_ref[...].reshape(1, V7X_SUBLANES, c)
    sub = lax.broadcasted_iota(jnp.int32, (nt, V7X_SUBLANES, c), 1)
    out = []
    for d in delays:
        cur = pltpu.roll(x3, d, 1)
        prev = jnp.concatenate([pltpu.roll(tail, d, 1), cur[:nt - 1]], axis=0)
        out.append(jnp.where(sub >= d, cur, prev).reshape(rows, c))
    tail_ref[...] = x[rows - V7X_SUBLANES:, :]
    return out


def _causal_conv(x, ext_ref, w_ref, b_ref):
    acc = b_ref[...] + w_ref[CONV_K - 1:CONV_K, :] * x
    for k, xd in enumerate(_delayed(x, ext_ref, range(CONV_K - 1, 0, -1))):
        acc = acc + w_ref[k:k + 1, :] * xd
    return acc


def _with_weight_cast(body, n_in):
    def kernel(*refs):
        ins, wi_ref, o_ref, wo_ref, scratch = refs[:n_in], refs[n_in], refs[n_in + 1], refs[n_in + 2], refs[n_in + 3:]
        wo_ref[...] = wi_ref[...].astype(BF16)
        body(*ins, o_ref, *scratch)
    return kernel


def _cast_slab(w_all, layer, bsz, nt):
    _, rows, cols = w_all.shape
    blk = rows // (bsz * nt)
    assert blk * bsz * nt == rows and blk % V7X_BF16_SUBLANES == 0, (rows, bsz, nt)
    return (pl.BlockSpec((None, blk, cols), lambda b, t: (layer, b * nt + t, 0)),
            pl.BlockSpec((blk, cols), lambda b, t: (b * nt + t, 0)),
            jax.ShapeDtypeStruct((rows, cols), BF16))


def _rmsnorm_rows(h, g):
    ms = jnp.mean(h * h, axis=-1, keepdims=True)
    return h * lax.rsqrt(ms + NORM_EPS) * g


def _norm_mm_kernel(h_ref, g_ref, w_ref, o_ref):
    hn = _rmsnorm_rows(h_ref[...], g_ref[...]).astype(BF16)
    o_ref[...] = jnp.dot(hn, w_ref[...], preferred_element_type=F32)


def _col_tile(n, target):
    return max(t for t in range(V7X_LANES, max(target, V7X_LANES) + 1, V7X_LANES) if n % t == 0)


def _norm_matmul(h, g, w, *, tm, tn):
    m, k = h.shape
    n = w.shape[1]
    tm, tn = min(tm, m), _col_tile(n, tn)
    assert m % tm == 0 and n % tn == 0, (m, n, tm, tn)
    return pl.pallas_call(
        _norm_mm_kernel,
        grid=(m // tm, n // tn),
        in_specs=[pl.BlockSpec((tm, k), lambda i, j: (i, 0)),
                  pl.BlockSpec((1, k), lambda i, j: (0, 0)),
                  pl.BlockSpec((k, tn), lambda i, j: (0, j))],
        out_specs=pl.BlockSpec((tm, tn), lambda i, j: (i, j)),
        out_shape=jax.ShapeDtypeStruct((m, n), F32),
        compiler_params=_cparams("parallel", "parallel"),
        name="norm_matmul",
    )(h, g.reshape(1, k), w)


def _mm_kernel(a_ref, w_ref, o_ref):
    o_ref[...] = jnp.dot(a_ref[...], w_ref[...], preferred_element_type=F32)


def _in_proj(h, hn, g, w, *, tm=1024, tn=1536):
    if hn is None:
        return _norm_matmul(h, g, w, tm=tm, tn=tn)
    m, k = hn.shape
    n = w.shape[1]
    tm, tn = min(tm, m), _col_tile(n, tn)
    assert m % tm == 0
    return pl.pallas_call(
        _mm_kernel,
        grid=(m // tm, n // tn),
        in_specs=[pl.BlockSpec((tm, k), lambda i, j: (i, 0)), pl.BlockSpec((k, tn), lambda i, j: (0, j))],
        out_specs=pl.BlockSpec((tm, tn), lambda i, j: (i, j)),
        out_shape=jax.ShapeDtypeStruct((m, n), F32),
        compiler_params=_cparams("parallel", "parallel"),
        name="in_proj",
    )(hn, w)


def _out_proj_kernel(a1_ref, a2_ref, w1_ref, w2_ref, r_ref, g_ref, o_ref, on_ref):
    ho = (r_ref[...] + jnp.dot(a1_ref[...], w1_ref[...], preferred_element_type=F32)
          + jnp.dot(a2_ref[...], w2_ref[...], preferred_element_type=F32))
    o_ref[...] = ho
    on_ref[...] = _rmsnorm_rows(ho, g_ref[...]).astype(BF16)


def _out_proj_residual(a1, a2, w, res, g_next, *, tm=512):
    m, k1 = a1.shape
    k2 = a2.shape[1]
    n = w.shape[1]
    tm = min(tm, m)
    assert m % tm == 0
    rows = lambda width: pl.BlockSpec((tm, width), lambda i: (i, 0))
    const = lambda shape: pl.BlockSpec(shape, lambda i: (0, 0))
    return pl.pallas_call(
        _out_proj_kernel,
        grid=(m // tm,),
        in_specs=[rows(k1), rows(k2), const((k1, n)), const((k2, n)), rows(n), const((1, n))],
        out_specs=[rows(n), rows(n)],
        out_shape=[jax.ShapeDtypeStruct((m, n), F32), jax.ShapeDtypeStruct((m, n), BF16)],
        compiler_params=_cparams("parallel"),
        name="out_proj",
    )(a1, a2, w[:k1], w[k1:], res, g_next.reshape(1, n))


def _mlp_kernel(h_ref, hn_ref, w1_ref, w2_ref, o_ref, acc_ref, *, nf):
    j = pl.program_id(1)

    @pl.when(j == 0)
    def _():
        acc_ref[...] = jnp.zeros_like(acc_ref)

    mid = jnp.dot(hn_ref[...], w1_ref[...], preferred_element_type=F32)
    mid = jnp.square(jnp.maximum(mid, 0.0)).astype(BF16)
    acc_ref[...] += jnp.dot(mid, w2_ref[...], preferred_element_type=F32)

    @pl.when(j == nf - 1)
    def _():
        o_ref[...] = h_ref[...] + acc_ref[...]


def _mlp_residual(h, hn, w1, w2, *, tm=512, tf=1024):
    m, d = h.shape
    f = w1.shape[1]
    tm, tf = min(tm, m), min(tf, f)
    assert m % tm == 0 and f % tf == 0
    nf = f // tf
    return pl.pallas_call(
        functools.partial(_mlp_kernel, nf=nf),
        grid=(m // tm, nf),
        in_specs=[pl.BlockSpec((tm, d), lambda i, j: (i, 0)),
                  pl.BlockSpec((tm, d), lambda i, j: (i, 0)),
                  pl.BlockSpec((d, tf), lambda i, j: (0, j)),
                  pl.BlockSpec((tf, d), lambda i, j: (j, 0))],
        out_specs=pl.BlockSpec((tm, d), lambda i, j: (i, 0)),
        out_shape=jax.ShapeDtypeStruct((m, d), F32),
        scratch_shapes=[pltpu.VMEM((tm, d), F32)],
        compiler_params=_cparams("parallel", "arbitrary"),
        name="mlp",
    )(h, hn, w1, w2)


def _gate_kernel(h_ref, g_ref, wg_ref, p_ref, wp_ref, gn_ref, *o_refs, final):
    h = h_ref[...]
    hn = _rmsnorm_rows(h, g_ref[...]).astype(BF16)
    gate = _sigmoid(jnp.dot(hn, wg_ref[...], preferred_element_type=F32))
    pp = jnp.dot(p_ref[...].astype(BF16), wp_ref[...], preferred_element_type=F32)
    ho = h + gate * pp
    nxt = _rmsnorm_rows(ho, gn_ref[...])
    if final:
        o_refs[0][...] = nxt
    else:
        o_refs[0][...] = ho
        o_refs[1][...] = nxt.astype(BF16)


def _gated_embed(h, g, wg, p, layer, wp, g_next, *, final, tm=512):
    m, k = h.shape
    kp = p.shape[2]
    tm = min(tm, m)
    assert m % tm == 0 and wg.shape[1:] == (k, k)
    const = lambda shape: pl.BlockSpec(shape, lambda i: (0, 0))
    rows = lambda w: pl.BlockSpec((tm, w), lambda i: (i, 0))
    out_specs = rows(k) if final else [rows(k), rows(k)]
    out_shape = (jax.ShapeDtypeStruct((m, k), F32) if final
                 else [jax.ShapeDtypeStruct((m, k), F32), jax.ShapeDtypeStruct((m, k), BF16)])
    return pl.pallas_call(
        functools.partial(_gate_kernel, final=final),
        grid=(m // tm,),
        in_specs=[rows(k), const((1, k)), pl.BlockSpec((None, k, k), lambda i: (layer, 0, 0)),
                  pl.BlockSpec((None, tm, kp), lambda i: (layer, i, 0)),
                  pl.BlockSpec((None, kp, k), lambda i: (layer, 0, 0)), const((1, k))],
        out_specs=out_specs,
        out_shape=out_shape,
        compiler_params=_cparams("parallel"),
        name="gated_embed",
    )(h, g.reshape(1, k), wg, p, wp, g_next.reshape(1, k))


def _s5_prep_kernel(lr_ref, li_ref, st_ref, btr_ref, bti_ref, b8_ref, a8_re_ref, a8_im_ref):
    lr, li, st = lr_ref[...], li_ref[...], jnp.exp(st_ref[...])
    lrs, lis = lr * st, li * st

    def apow(n):
        mag = jnp.exp(n * lrs)
        return mag * jnp.cos(n * lis), mag * jnp.sin(n * lis)

    mag = jnp.exp(lrs)
    abar_re, abar_im = mag * jnp.cos(lis), mag * jnp.sin(lis)
    den = lr * lr + li * li
    nr = abar_re - 1.0
    coef_re = (nr * lr + abar_im * li) / den
    coef_im = (abar_im * lr - nr * li) / den
    btr, bti = btr_ref[...], bti_ref[...]
    bbar_re = coef_re * btr - coef_im * bti
    bbar_im = coef_re * bti + coef_im * btr
    nblk, taps_ch, lanes2 = b8_ref.shape
    lane_blk = lanes2 // 2
    ch_blk = taps_ch // V7X_SUBLANES
    rg = lax.broadcasted_iota(jnp.int32, (ch_blk, lane_blk), 0) // S5_GROUP
    cg = lax.broadcasted_iota(jnp.int32, (ch_blk, lane_blk), 1) // S5_STATE
    same = rg == cg
    for s in range(V7X_SUBLANES):
        if s == 0:
            br, bi = bbar_re, bbar_im
        else:
            pr, pi = apow(float(s))
            br, bi = pr * bbar_re - pi * bbar_im, pr * bbar_im + pi * bbar_re
        for j in range(nblk):
            ls = slice(j * lane_blk, (j + 1) * lane_blk)
            rs = slice(s * ch_blk, (s + 1) * ch_blk)
            tile = lambda m: jnp.where(same, jnp.concatenate([m[:, ls]] * (ch_blk // S5_GROUP), axis=0), 0.0)
            b8_ref[j, rs, :lane_blk] = tile(br).astype(BF16)
            b8_ref[j, rs, lane_blk:] = tile(bi).astype(BF16)
    a8r, a8i = apow(float(V7X_SUBLANES))
    a8_re_ref[...] = jnp.broadcast_to(a8r, a8_re_ref.shape)
    a8_im_ref[...] = jnp.broadcast_to(a8i, a8_im_ref.shape)


def _s5_kernel(u_ref, b8_ref, cre_ref, cim_ref, a8r_ref, a8i_ref, d_ref, gw_ref, gb_ref, o_ref,
               ue_ref, car_re, car_im):
    @pl.when(pl.program_id(1) == 0)
    def _():
        ue_ref[...] = jnp.zeros_like(ue_ref)
        car_re[...] = jnp.zeros_like(car_re)
        car_im[...] = jnp.zeros_like(car_im)

    u = u_ref[...]
    tt = u.shape[0]
    nblk, taps_ch, lanes2 = b8_ref.shape
    lane_blk = lanes2 // 2
    ch_blk = taps_ch // V7X_SUBLANES
    taps = [u] + _delayed(u, ue_ref, range(1, V7X_SUBLANES))
    ys = []
    for j in range(nblk):
        ls = slice(j * lane_blk, (j + 1) * lane_blk)
        cs = slice(j * ch_blk, (j + 1) * ch_blk)
        u8 = jnp.concatenate([t[:, cs].astype(BF16) for t in taps], axis=1)
        z = jnp.dot(u8, b8_ref[j], preferred_element_type=F32)
        pr, pi = a8r_ref[:, ls], a8i_ref[:, ls]
        cr, ci = car_re[:, ls], car_im[:, ls]
        xr, xi = [], []
        for i in range(tt // V7X_SUBLANES):
            rs = slice(i * V7X_SUBLANES, (i + 1) * V7X_SUBLANES)
            cr, ci = (z[rs, :lane_blk] + (pr * cr - pi * ci), z[rs, lane_blk:] + (pr * ci + pi * cr))
            xr.append(cr)
            xi.append(ci)
        car_re[:, ls] = cr
        car_im[:, ls] = ci
        ys.append(_bdot(jnp.concatenate(xr, axis=0), cre_ref[j]) - _bdot(jnp.concatenate(xi, axis=0), cim_ref[j]))
    y = jnp.concatenate(ys, axis=1) + d_ref[...] * u
    act = _gelu(y)
    o_ref[...] = (act * _sigmoid(_bdot(act, gw_ref[...]) + gb_ref[...])).astype(o_ref.dtype)


def _s5_mixer(proj, col_blk, bsz, seq, lam_re, lam_im, log_step, b_re, b_im, c_re, c_im,
              d_skip, glu_w, glu_b, cast_w, cast_layer, *, tt=512, lane_blk=128):
    ng, ns = lam_re.shape
    width = ng * S5_GROUP
    nstate = ng * ns
    tt = min(tt, seq)
    assert seq % tt == 0 and tt % V7X_SUBLANES == 0 and nstate % lane_blk == 0 and lane_blk % ns == 0
    nblk = nstate // lane_blk
    gpb = lane_blk // ns
    ch_blk = gpb * S5_GROUP
    flat = lambda a: a.astype(F32).reshape(1, nstate)
    step = jnp.broadcast_to(log_step.astype(F32)[:, None], (ng, ns))
    bt = lambda a: a.astype(F32).transpose(2, 0, 1).reshape(S5_GROUP, nstate)
    vm = pl.BlockSpec(memory_space=pltpu.VMEM)
    b8, a8_re, a8_im = pl.pallas_call(
        _s5_prep_kernel,
        in_specs=[vm] * 5,
        out_specs=[vm] * 3,
        out_shape=[jax.ShapeDtypeStruct((nblk, V7X_SUBLANES * ch_blk, 2 * lane_blk), BF16)]
        + [jax.ShapeDtypeStruct((V7X_SUBLANES, nstate), F32)] * 2,
        compiler_params=pltpu.CompilerParams(vmem_limit_bytes=VMEM_LIMIT_BYTES),
        name="s5_prep",
    )(flat(lam_re), flat(lam_im), flat(step), bt(b_re), bt(b_im))
    eye = jnp.eye(gpb, dtype=F32)
    cbd = lambda c: (c.astype(F32).reshape(nblk, gpb, S5_GROUP, ns).transpose(0, 1, 3, 2)[:, :, :, None, :]
                     * eye[None, :, None, :, None]).reshape(nblk, lane_blk, ch_blk).astype(BF16)
    nt = seq // tt
    cast_in, cast_out, cast_shape = _cast_slab(cast_w, cast_layer, bsz, nt)
    const = lambda shape: pl.BlockSpec(shape, lambda b, t: (0,) * len(shape))
    return pl.pallas_call(
        _with_weight_cast(_s5_kernel, 9),
        grid=(bsz, nt),
        in_specs=[pl.BlockSpec((tt, width), lambda b, t: (b * nt + t, col_blk)),
                  const((nblk, V7X_SUBLANES * ch_blk, 2 * lane_blk)),
                  const((nblk, lane_blk, ch_blk)), const((nblk, lane_blk, ch_blk)),
                  const((V7X_SUBLANES, nstate)), const((V7X_SUBLANES, nstate)),
                  const((1, width)), const((width, width)), const((1, width)), cast_in],
        out_specs=[pl.BlockSpec((tt, width), lambda b, t: (b * nt + t, 0)), cast_out],
        out_shape=[jax.ShapeDtypeStruct((bsz * seq, width), BF16), cast_shape],
        scratch_shapes=[pltpu.VMEM((V7X_SUBLANES, width), F32)]
        + [pltpu.VMEM((V7X_SUBLANES, nstate), F32)] * 2,
        compiler_params=_cparams("parallel", "arbitrary"),
        name="s5_mixer",
    )(proj, b8, cbd(c_re), cbd(c_im), a8_re, a8_im,
      d_skip.astype(F32).reshape(1, width), glu_w.astype(BF16), glu_b.astype(F32).reshape(1, width), cast_w)


def _ssd_kernel(z_ref, xs_ref, bc_ref, dt_ref, cwx_ref, cbx_ref, cwb_ref, cbb_ref, e64_ref,
                dtb_ref, alog_ref, dskx_ref, ng_ref, o_ref,
                tailx, tailb, st_ref):
    @pl.when(pl.program_id(1) == 0)
    def _():
        tailx[...] = jnp.zeros_like(tailx)
        tailb[...] = jnp.zeros_like(tailb)
        st_ref[...] = jnp.zeros_like(st_ref)

    L = SSD_CHUNK
    tt = xs_ref.shape[0]
    width = xs_ref.shape[1]
    gstate = SSD_GROUPS * SSD_STATE
    xs = _silu(_causal_conv(xs_ref[...], tailx, cwx_ref, cbx_ref))
    bc = _silu(_causal_conv(bc_ref[...], tailb, cwb_ref, cbb_ref))
    dt = _softplus(dt_ref[...] + dtb_ref[...])
    da = dt * (-jnp.exp(alog_ref[...]))
    e64 = e64_ref[...]
    xdt = xs * _dot_rhs01(dt, e64)
    ri = lax.broadcasted_iota(jnp.int32, (L, L), 0)
    ci = lax.broadcasted_iota(jnp.int32, (L, L), 1)
    tril = ri >= ci
    tri01 = tril.astype(BF16)
    lane_lo = lax.broadcasted_iota(jnp.int32, (1, V7X_LANES), 1) < SSD_HEAD_DIM
    pairs_per_group = SSD_HEADS // SSD_GROUPS // 2
    ys = []
    for c in range(tt // L):
        rs = slice(c * L, (c + 1) * L)
        acum = _dot_lhs01(tri01, da[rs, :])
        acum_t = acum.T
        acx = _dot_rhs01(acum, e64)
        ycols = []
        for g in range(SSD_GROUPS):
            bg = bc[rs, g * SSD_STATE:(g + 1) * SSD_STATE]
            cg = bc[rs, gstate + g * SSD_STATE:gstate + (g + 1) * SSD_STATE]
            scores = _bdot_nt(cg, bg).astype(BF16)
            bg_t = bg.T
            for pr in range(pairs_per_group):
                q = g * pairs_per_group + pr
                ls = slice(q * V7X_LANES, (q + 1) * V7X_LANES)
                xp = xdt[rs, ls]
                acp = acx[:, ls]
                yd = []
                for h in (2 * q, 2 * q + 1):
                    seg = acum[:, h:h + 1] - acum_t[h:h + 1, :]
                    decay = jnp.exp(jnp.where(tril, seg, -jnp.inf).astype(BF16))
                    yd.append(jnp.dot(scores * decay, xp.astype(BF16), preferred_element_type=F32))
                y = jnp.where(lane_lo, yd[0], yd[1])
                prev_t = st_ref[q]
                y = y + _bdot(cg, prev_t) * jnp.exp(acp)
                last = acp[L - 1:L, :]
                st_ref[q] = prev_t * jnp.exp(last) + _bdot(bg_t, xp * jnp.exp(last - acp))
                ycols.append(y)
        ys.append(jnp.concatenate(ycols, axis=1))
    y = jnp.concatenate(ys, axis=0) if len(ys) > 1 else ys[0]
    y = (y + xs * dskx_ref[...]) * _silu(z_ref[...])
    gw = width // SSD_GROUPS
    outs = []
    for g in range(SSD_GROUPS):
        yg = y[:, g * gw:(g + 1) * gw]
        outs.append(yg * lax.rsqrt(jnp.mean(yg * yg, axis=-1, keepdims=True) + NORM_EPS))
    o_ref[...] = (jnp.concatenate(outs, axis=1) * ng_ref[...]).astype(o_ref.dtype)


def _ssd_mixer(proj, blk_z, blk_xs, blk_bc, blk_dt, bsz, seq, conv_w, conv_b, dt_bias, a_log,
               d_skip, norm_g, cast_w, cast_layer, *, tt=512):
    width = SSD_HEADS * SSD_HEAD_DIM
    gstate2 = 2 * SSD_GROUPS * SSD_STATE
    tt = min(tt, seq)
    assert seq % tt == 0 and tt % SSD_CHUNK == 0
    nt = seq // tt
    cast_in, cast_out, cast_shape = _cast_slab(cast_w, cast_layer, bsz, nt)
    f = lambda a: a.astype(F32)
    pad_h = lambda a: jnp.pad(f(a), (0, V7X_LANES - SSD_HEADS)).reshape(1, V7X_LANES)
    exp_h = lambda a: jnp.repeat(f(a), SSD_HEAD_DIM).reshape(1, width)
    e64 = (jnp.arange(V7X_LANES)[:, None] == (jnp.arange(width)[None, :] // SSD_HEAD_DIM)).astype(BF16)
    const = lambda shape: pl.BlockSpec(shape, lambda b, t: (0,) * len(shape))
    rowblk = lambda w, cb: pl.BlockSpec((tt, w), lambda b, t: (b * nt + t, cb))
    return pl.pallas_call(
        _with_weight_cast(_ssd_kernel, 13),
        grid=(bsz, nt),
        in_specs=[rowblk(width, blk_z), rowblk(width, blk_xs), rowblk(gstate2, blk_bc),
                  rowblk(V7X_LANES, blk_dt),
                  const((CONV_K, width)), const((1, width)), const((CONV_K, gstate2)), const((1, gstate2)),
                  const((V7X_LANES, width)),
                  const((1, V7X_LANES)), const((1, V7X_LANES)),
                  const((1, width)), const((1, width)), cast_in],
        out_specs=[pl.BlockSpec((tt, width), lambda b, t: (b * nt + t, 0)), cast_out],
        out_shape=[jax.ShapeDtypeStruct((bsz * seq, width), BF16), cast_shape],
        scratch_shapes=[pltpu.VMEM((V7X_SUBLANES, width), F32),
                        pltpu.VMEM((V7X_SUBLANES, gstate2), F32),
                        pltpu.VMEM((SSD_HEADS // 2, SSD_STATE, V7X_LANES), F32)],
        compiler_params=_cparams("parallel", "arbitrary"),
        name="ssd_mixer",
    )(proj, proj, proj, proj,
      f(conv_w[:, :width]), f(conv_b[:width]).reshape(1, width),
      f(conv_w[:, width:]), f(conv_b[width:]).reshape(1, gstate2),
      e64, pad_h(dt_bias), pad_h(a_log), exp_h(d_skip),
      f(norm_g).reshape(1, width), cast_w)


def _stack_heads(x, lane_lo):
    return jnp.concatenate([jnp.where(lane_lo, x, 0.0), jnp.where(lane_lo, 0.0, x)], axis=0)


def _unit_lower_inverses(mats, same_blk, eye):
    size = eye.shape[0]
    side = lambda x, y: _bdot(x, jnp.concatenate([x, y], axis=1))
    ad = [jnp.where(same_blk, a, 0.0) for a in mats]
    ao = [a - d for a, d in zip(mats, ad)]
    s1 = [eye + d for d in ad]
    a2 = [_bdot(d, d) for d in ad]
    r = [side(x, s) for x, s in zip(a2, s1)]
    a4, s2 = [x[:, :size] for x in r], [s + x[:, size:] for s, x in zip(s1, r)]
    r = [side(x, s) for x, s in zip(a4, s2)]
    a8, s3 = [x[:, :size] for x in r], [s + x[:, size:] for s, x in zip(s2, r)]
    td = [s + _bdot(x, s) for x, s in zip(a8, s3)]
    n = [_bdot(t, o) for t, o in zip(td, ao)]
    r = [side(x, t) for x, t in zip(n, td)]
    n2, w1 = [x[:, :size] for x in r], [t + x[:, size:] for t, x in zip(td, r)]
    return [w + _bdot(x, w) for x, w in zip(n2, w1)]


def _rwkv_kernel(rkv_ref, lo_ref, mu_rkv_ref, mu_lo_ref, w0_ref, wup_ref, a0_ref, aup_ref, gup_ref,
                 kk_ref, ka_ref, rk_ref, lng_ref, lnb_ref, ones_ref, o_ref,
                 prev_rkv, prev_lo, state_ref):
    @pl.when(pl.program_id(1) == 0)
    def _():
        prev_rkv[...] = jnp.zeros_like(prev_rkv)
        prev_lo[...] = jnp.zeros_like(prev_lo)
        state_ref[...] = jnp.zeros_like(state_ref)

    L = RWKV_CHUNK
    tb = rkv_ref.shape[0]
    width = rkv_ref.shape[1] // 3

    def shift_mix(ref, prev, mu_ref):
        f = ref[...]
        fs = jnp.where(_row_iota(f.shape) == 0, prev[0:1, :], pltpu.roll(f, 1, 0))
        prev[0:1, :] = f[tb - 1:tb, :]
        return f + (fs - f) * mu_ref[...]

    rkv = shift_mix(rkv_ref, prev_rkv, mu_rkv_ref)
    lo = shift_mix(lo_ref, prev_lo, mu_lo_ref)
    r, k, v = rkv[:, :width], rkv[:, width:2 * width], rkv[:, 2 * width:]
    nl = wup_ref.shape[0]
    wl, al, gl = lo[:, :nl], lo[:, nl:2 * nl], lo[:, 2 * nl:]
    w = -_softplus(-(w0_ref[...] + _bdot(jnp.tanh(wl), wup_ref[...]))) - 0.5
    logw = -jnp.exp(w)
    a_sig = _sigmoid(a0_ref[...] + _bdot(al, aup_ref[...]))
    g = _bdot(_sigmoid(gl), gup_ref[...])
    kk = k * kk_ref[...]
    k = k * (1.0 + (a_sig - 1.0) * ka_ref[...])
    ones_bd = ones_ref[...]

    def head_sum(x):
        return jnp.dot(x.astype(BF16), ones_bd, preferred_element_type=F32)

    P2 = 2 * L
    ri = lax.broadcasted_iota(jnp.int32, (P2, P2), 0)
    ci = lax.broadcasted_iota(jnp.int32, (P2, P2), 1)
    same_head = (ri // L) == (ci // L)
    strict = same_head & (ri > ci)
    incl = same_head & (ri >= ci)
    same_blk = (ri // RWKV_INV_BLOCK) == (ci // RWKV_INV_BLOCK)
    eye = (ri == ci).astype(F32)
    tri01 = (lax.broadcasted_iota(jnp.int32, (L, L), 0) >= lax.broadcasted_iota(jnp.int32, (L, L), 1)).astype(BF16)
    lane_lo = lax.broadcasted_iota(jnp.int32, (1, V7X_LANES), 1) < RWKV_HEAD_DIM

    nq = width // V7X_LANES
    lss = [slice(q * V7X_LANES, (q + 1) * V7X_LANES) for q in range(nq)]
    kks = [kk[:, ls] for ls in lss]
    kkn = [x * lax.rsqrt(jnp.maximum(head_sum(x * x), 1e-24)) for x in kks]
    nc = tb // L
    inst = [(slice(c * L, (c + 1) * L), q, ls) for c in range(nc) for q, ls in enumerate(lss)]
    cums = [_dot_lhs01(tri01, logw[c * L:(c + 1) * L, :]) for c in range(nc)]
    g_in = jnp.concatenate([jnp.exp(cm) for cm in cums], axis=0)
    g_prev = jnp.concatenate([jnp.exp(cm - logw[c * L:(c + 1) * L, :]) for c, cm in enumerate(cums)], axis=0)
    g_inv = jnp.concatenate([jnp.exp(-cm) for cm in cums], axis=0)
    ats = [_stack_heads(-kkn[q][rs] * g_prev[rs, ls], lane_lo) for rs, q, ls in inst]
    rts = [_stack_heads(r[rs, ls] * g_in[rs, ls], lane_lo) for rs, q, ls in inst]
    bts = [_stack_heads(kkn[q][rs] * a_sig[rs, ls] * g_inv[rs, ls], lane_lo) for rs, q, ls in inst]
    kts = [_stack_heads(k[rs, ls] * g_inv[rs, ls], lane_lo) for rs, q, ls in inst]
    vss = [_stack_heads(v[rs, ls], lane_lo).astype(BF16) for rs, q, ls in inst]
    ars = [jnp.concatenate([a, b], axis=0).astype(BF16) for a, b in zip(ats, rts)]
    ps = [_bdot_nt(ar, jnp.concatenate([b, kq], axis=0)) for ar, b, kq in zip(ars, bts, kts)]
    tinv = _unit_lower_inverses([jnp.where(strict, p[:P2, :P2], 0.0) for p in ps], same_blk, eye)
    akv = [_bdot(jnp.where(strict, p[:P2, P2:], 0.0), vs) for p, vs in zip(ps, vss)]
    a_r = [jnp.concatenate([jnp.where(incl, p[P2:, :P2], 0.0), jnp.where(incl, p[P2:, P2:], 0.0)],
                           axis=1).astype(BF16) for p in ps]
    g_last = [g_in[rs, ls][L - 1:L, :] for rs, q, ls in inst]
    bkl = [(jnp.concatenate([b, kq], axis=0) * gl_).astype(BF16) for b, kq, gl_ in zip(bts, kts, g_last)]
    hts = [state_ref[q] for q in range(nq)]
    yrows = []
    for c in range(nc):
        sel = lambda xs: xs[c * nq:(c + 1) * nq]
        arh = [_bdot_nt(ar, ht) for ar, ht in zip(sel(ars), hts)]
        us = [_bdot(t, x[:P2] + y) for t, x, y in zip(sel(tinv), arh, sel(akv))]
        uv = [jnp.concatenate([u.astype(BF16), vs], axis=0) for u, vs in zip(us, sel(vss))]
        hts = [ht * gl_ + _bdot_tn(w, b) for ht, gl_, w, b in zip(hts, sel(g_last), uv, sel(bkl))]
        ys = [x[P2:] + jnp.dot(m, w, preferred_element_type=F32) for x, m, w in zip(arh, sel(a_r), uv)]
        yrows.append(jnp.concatenate([y[:L] + y[L:] for y in ys], axis=1))
    for q in range(nq):
        state_ref[q] = hts[q]
    y = jnp.concatenate(yrows, axis=0) if len(yrows) > 1 else yrows[0]
    inv_n = 1.0 / RWKV_HEAD_DIM
    outs = []
    for q, ls in enumerate(lss):
        yq = y[:, ls]
        mean = head_sum(yq) * inv_n
        yc = yq - mean
        var = head_sum(yc * yc) * inv_n
        yn = yc * lax.rsqrt(var + RWKV_GN_EPS) * lng_ref[:, ls] + lnb_ref[:, ls]
        bonus = head_sum(r[:, ls] * k[:, ls] * rk_ref[:, ls]) * v[:, ls]
        outs.append((yn + bonus) * g[:, ls])
    o_ref[...] = jnp.concatenate(outs, axis=1).astype(o_ref.dtype)


def _rwkv_mixer(proj, blk_rkv, blk_lo, bsz, seq, mu_rkv, mu_lo, w0, w_up, a0, a_up, g_up,
                k_k, k_a, r_k, ln_g, ln_b, cast_w, cast_layer, *, tb=256):
    width = w0.shape[0]
    nl = w_up.shape[0]
    tb = min(tb, seq)
    assert seq % tb == 0 and tb % RWKV_CHUNK == 0
    nt = seq // tb
    cast_in, cast_out, cast_shape = _cast_slab(cast_w, cast_layer, bsz, nt)
    f = lambda a: a.astype(F32).reshape(1, -1)
    lane = jnp.arange(V7X_LANES) // RWKV_HEAD_DIM
    ones_bd = (lane[:, None] == lane[None, :]).astype(BF16)
    const = lambda shape: pl.BlockSpec(shape, lambda b, t: (0,) * len(shape))
    return pl.pallas_call(
        _with_weight_cast(_rwkv_kernel, 15),
        grid=(bsz, nt),
        in_specs=[pl.BlockSpec((tb, 3 * width), lambda b, t: (b * nt + t, blk_rkv)),
                  pl.BlockSpec((tb, 4 * nl), lambda b, t: (b * nt + t, blk_lo)),
                  const((1, 3 * width)), const((1, 4 * nl)),
                  const((1, width)), const((nl, width)), const((1, width)), const((nl, width)),
                  const((2 * nl, width)),
                  const((1, width)), const((1, width)), const((1, width)), const((1, width)),
                  const((1, width)), const((V7X_LANES, V7X_LANES)), cast_in],
        out_specs=[pl.BlockSpec((tb, width), lambda b, t: (b * nt + t, 0)), cast_out],
        out_shape=[jax.ShapeDtypeStruct((bsz * seq, width), BF16), cast_shape],
        scratch_shapes=[pltpu.VMEM((V7X_SUBLANES, 3 * width), F32),
                        pltpu.VMEM((V7X_SUBLANES, 4 * nl), F32),
                        pltpu.VMEM((width // V7X_LANES, V7X_LANES, V7X_LANES), F32)],
        compiler_params=_cparams("parallel", "arbitrary"),
        name="rwkv_mixer",
    )(proj, proj, mu_rkv, mu_lo, f(w0), w_up.astype(BF16), f(a0), a_up.astype(BF16),
      g_up.astype(BF16), f(k_k), f(k_a), f(r_k), f(ln_g), f(ln_b), ones_bd, cast_w)


def _lru_kernel(xl_ref, gl_ref, cw_ref, cb_ref, wa_ref, ba_ref, wx_ref, bx_ref, lam_ref, o_ref,
                tail, carry):
    @pl.when(pl.program_id(1) == 0)
    def _():
        tail[...] = jnp.zeros_like(tail)
        carry[...] = jnp.zeros_like(carry)

    tt, width = xl_ref.shape
    row = _row_iota((tt, width))
    xc = _causal_conv(xl_ref[...], tail, cw_ref, cb_ref)
    gr, gi = [], []
    for j in range(width // V7X_LANES):
        xj = xc[:, j * V7X_LANES:(j + 1) * V7X_LANES]
        gr.append(_bdot(xj, wa_ref[j]))
        gi.append(_bdot(xj, wx_ref[j]))
    gate_r = _sigmoid(jnp.concatenate(gr, axis=1) + ba_ref[...])
    gate_i = _sigmoid(jnp.concatenate(gi, axis=1) + bx_ref[...])
    log_a = -LRU_C * gate_r * _softplus(-lam_ref[...])
    a = jnp.exp(log_a)
    mult = jnp.sqrt(jnp.maximum(-(jnp.tanh(log_a) * (a * a + 1.0)), 0.0))
    mult = jnp.where(row + pl.program_id(1) * tt == 0, 1.0, mult)
    b = xc * gate_i * mult
    nt = tt // V7X_SUBLANES
    a, b = a.reshape(nt, V7X_SUBLANES, width), b.reshape(nt, V7X_SUBLANES, width)
    sub = lax.broadcasted_iota(jnp.int32, a.shape, 1)
    for s in (1, 2, 4):
        keep = sub >= s
        sa, sb = pltpu.roll(a, s, 1), pltpu.roll(b, s, 1)
        b = b + jnp.where(keep, a * sb, 0.0)
        a = jnp.where(keep, a * sa, a)
    a, b = a.reshape(tt, width), b.reshape(tt, width)
    c = carry[0:1, :]
    hs = []
    for i in range(tt // V7X_SUBLANES):
        rs = slice(i * V7X_SUBLANES, (i + 1) * V7X_SUBLANES)
        hi = b[rs, :] + a[rs, :] * c
        hs.append(hi)
        c = hi[V7X_SUBLANES - 1:, :]
    carry[0:1, :] = c
    h = jnp.concatenate(hs, axis=0)
    o_ref[...] = (h * _gelu(gl_ref[...])).astype(o_ref.dtype)


def _lru_mixer(proj, blk_xl, blk_gl, bsz, seq, conv_w, conv_b, w_a, b_a, w_x, b_x, lam, cast_w, cast_layer,
               *, tt=512):
    nb, blk, _ = w_a.shape
    width = nb * blk
    tt = min(tt, seq)
    assert seq % tt == 0 and tt % V7X_SUBLANES == 0
    nt = seq // tt
    cast_in, cast_out, cast_shape = _cast_slab(cast_w, cast_layer, bsz, nt)
    per = V7X_LANES // blk
    eye = jnp.eye(per, dtype=F32)
    bd = lambda w: (w.astype(F32).reshape(nb // per, per, blk, 1, blk) * eye[None, :, None, :, None]
                    ).reshape(nb // per, V7X_LANES, V7X_LANES).astype(BF16)
    f = lambda a: a.astype(F32).reshape(1, width)
    const = lambda shape: pl.BlockSpec(shape, lambda b, t: (0,) * len(shape))
    return pl.pallas_call(
        _with_weight_cast(_lru_kernel, 9),
        grid=(bsz, nt),
        in_specs=[pl.BlockSpec((tt, width), lambda b, t: (b * nt + t, blk_xl)),
                  pl.BlockSpec((tt, width), lambda b, t: (b * nt + t, blk_gl)),
                  const((CONV_K, width)), const((1, width)),
                  const((nb // per, V7X_LANES, V7X_LANES)), const((1, width)),
                  const((nb // per, V7X_LANES, V7X_LANES)), const((1, width)), const((1, width)), cast_in],
        out_specs=[pl.BlockSpec((tt, width), lambda b, t: (b * nt + t, 0)), cast_out],
        out_shape=[jax.ShapeDtypeStruct((bsz * seq, width), BF16), cast_shape],
        scratch_shapes=[pltpu.VMEM((V7X_SUBLANES, width), F32), pltpu.VMEM((V7X_SUBLANES, width), F32)],
        compiler_params=_cparams("parallel", "arbitrary"),
        name="lru_mixer",
    )(proj, proj, conv_w.astype(F32), f(conv_b), bd(w_a), f(b_a), bd(w_x), f(b_x), f(lam), cast_w)


def _pad_cols(w, n):
    return jnp.pad(w, ((0, 0), (0, n - w.shape[1])))


def _even_mixer(h, hn, bsz, seq, layer, mlp_w1, mlp_w2, norm_g, norm_out, in_proj, out_proj,
                lam_re, lam_im, log_step, b_re, b_im, c_re, c_im,
                s5_d, glu_w, glu_b, conv_w, conv_b, dt_bias, a_log, ssd_d, ssd_norm):
    s5w = lam_re.shape[0] * S5_GROUP
    ssdw = SSD_HEADS * SSD_HEAD_DIM
    gstate2 = 2 * SSD_GROUPS * SSD_STATE
    u_w, z_w, xbc_w, dt_w = jnp.split(in_proj, [s5w, s5w + ssdw, s5w + 2 * ssdw + gstate2], axis=1)
    w_all = jnp.concatenate([z_w, xbc_w, u_w, dt_w], axis=1)
    w_all = _pad_cols(w_all, -(-w_all.shape[1] // 512) * 512).astype(BF16)
    proj = _in_proj(h, hn, norm_g, w_all)
    off_bc, off_u, off_dt = 2 * ssdw, 2 * ssdw + gstate2, 2 * ssdw + gstate2 + s5w
    y_a, w1 = _s5_mixer(proj, off_u // s5w, bsz, seq, lam_re, lam_im, log_step, b_re, b_im, c_re, c_im,
                        s5_d, glu_w, glu_b, mlp_w1, layer)
    y_b, w2 = _ssd_mixer(proj, 0, 1, off_bc // gstate2, off_dt // V7X_LANES, bsz, seq,
                         conv_w, conv_b, dt_bias, a_log, ssd_d, ssd_norm, mlp_w2, layer)
    return [*_out_proj_residual(y_a, y_b, out_proj.astype(BF16), h, norm_out), w1, w2]


def _odd_mixer(h, hn, bsz, seq, layer, mlp_w1, mlp_w2, norm_g, norm_out, in_proj, out_proj,
               mu, w0, w_up, a0, a_up, g_up, k_k, k_a, r_k,
               ln_g, ln_b, conv_w, conv_b, w_a, b_a, w_x, b_x, lam):
    rw = w0.shape[0]
    nl = w_up.shape[0]
    ngl = g_up.shape[0]
    lw = lam.shape[0] * lam.shape[1]
    nlp = V7X_LANES
    assert nl <= nlp and ngl == 2 * nlp
    rkv_w, wl_w, al_w, gl_w, xl_w, g2_w = jnp.split(
        in_proj, [3 * rw, 3 * rw + nl, 3 * rw + 2 * nl, 3 * rw + 2 * nl + ngl, 3 * rw + 2 * nl + ngl + lw], axis=1)
    w_all = jnp.concatenate([rkv_w, xl_w, g2_w, _pad_cols(wl_w, nlp), _pad_cols(al_w, nlp), gl_w],
                            axis=1).astype(BF16)
    mu = mu.astype(F32)
    mu_rkv = mu[:3 * rw].reshape(1, -1)
    padv = lambda v: jnp.pad(v, (0, nlp - v.shape[0]))
    mu_lo = jnp.concatenate([padv(mu[3 * rw:3 * rw + nl]), padv(mu[3 * rw + nl:3 * rw + 2 * nl]),
                             mu[3 * rw + 2 * nl:]]).reshape(1, -1)
    padr = lambda w: jnp.pad(w, ((0, nlp - w.shape[0]), (0, 0)))
    proj = _in_proj(h, hn, norm_g, w_all)
    off_xl, off_gl, off_lo = 3 * rw, 3 * rw + lw, 3 * rw + 2 * lw
    y_c, w1 = _rwkv_mixer(proj, 0, off_lo // (4 * nlp), bsz, seq, mu_rkv, mu_lo, w0, padr(w_up), a0, padr(a_up),
                          g_up, k_k, k_a, r_k.reshape(-1), ln_g, ln_b, mlp_w1, layer)
    y_d, w2 = _lru_mixer(proj, off_xl // lw, off_gl // lw, bsz, seq, conv_w, conv_b, w_a, b_a, w_x, b_x, lam,
                         mlp_w2, layer)
    return [*_out_proj_residual(y_c, y_d, out_proj.astype(BF16), h, norm_out), w1, w2]


def kernel(x, p, norm_mix, norm_ffn, norm_pl, mlp_w1, mlp_w2, pl_proj, pl_gate, e_in_proj, e_out_proj, s5_lam_re, s5_lam_im, s5_log_step, s5_b_re, s5_b_im, s5_c_re, s5_c_im, s5_d, s5_glu_w, s5_glu_b, ssd_conv_w, ssd_conv_b, ssd_dt_bias, ssd_a_log, ssd_d, ssd_norm, o_in_proj, o_out_proj, rwkv_mu, rwkv_w0, rwkv_w_up, rwkv_a0, rwkv_a_up, rwkv_g_up, rwkv_k_k, rwkv_k_a, rwkv_r_k, rwkv_ln_g, rwkv_ln_b, lru_conv_w, lru_conv_b, lru_w_a, lru_b_a, lru_w_x, lru_b_x, lru_lam, norm_final):
    bsz, seq, d = x.shape
    depth = p.shape[0]
    h = x.astype(F32).reshape(bsz * seq, d)
    hn = None
    wg_all, wp_all = pl_gate.astype(BF16), pl_proj.astype(BF16)
    p_all = p.reshape(depth, bsz * seq, -1)
    mlp_w1, mlp_w2 = mlp_w1.astype(F32), mlp_w2.astype(F32)
    for i in range(depth):
        j = i // 2
        common = (h, hn, bsz, seq, i, mlp_w1, mlp_w2, norm_mix[i], norm_ffn[i])
        if i % 2 == 0:
            h, hn, w1, w2 = _even_mixer(*common, e_in_proj[j], e_out_proj[j], s5_lam_re[j], s5_lam_im[j],
                                        s5_log_step[j], s5_b_re[j], s5_b_im[j], s5_c_re[j], s5_c_im[j], s5_d[j],
                                        s5_glu_w[j], s5_glu_b[j], ssd_conv_w[j], ssd_conv_b[j], ssd_dt_bias[j],
                                        ssd_a_log[j], ssd_d[j], ssd_norm[j])
        else:
            h, hn, w1, w2 = _odd_mixer(*common, o_in_proj[j], o_out_proj[j], rwkv_mu[j], rwkv_w0[j],
                                       rwkv_w_up[j], rwkv_a0[j], rwkv_a_up[j], rwkv_g_up[j], rwkv_k_k[j], rwkv_k_a[j],
                                       rwkv_r_k[j], rwkv_ln_g[j], rwkv_ln_b[j], lru_conv_w[j], lru_conv_b[j],
                                       lru_w_a[j], lru_b_a[j], lru_w_x[j], lru_b_x[j], lru_lam[j])
        h = _mlp_residual(h, hn, w1, w2)
        gate_args = (h, norm_pl[i], wg_all, p_all, i, wp_all)
        if i + 1 < depth:
            h, hn = _gated_embed(*gate_args, norm_mix[i + 1], final=False)
        else:
            h = _gated_embed(*gate_args, norm_final, final=True)
    return h.reshape(bsz, seq, d)
```

```python
import functools

import jax
import jax.numpy as jnp
from jax import lax
from jax.experimental import pallas as pl
from jax.experimental.pallas import tpu as pltpu

F32 = jnp.float32
BF16 = jnp.bfloat16

V7X_LANES = 128
V7X_SUBLANES = 8
V7X_BF16_SUBLANES = 16
V7X_VMEM_BYTES = 64 * 1024 * 1024
VMEM_LIMIT_BYTES = V7X_VMEM_BYTES - 8 * 1024 * 1024

NORM_EPS = 1e-6
S5_GROUP = 16
S5_STATE = 64
SSD_HEAD_DIM = 64
SSD_HEADS = 24
SSD_GROUPS = 4
SSD_STATE = 128
SSD_CHUNK = 128
CONV_K = 4
RWKV_HEAD_DIM = 64
RWKV_GN_EPS = 64e-5
RWKV_CHUNK = 64
RWKV_INV_BLOCK = 16
LRU_BLOCK = 64
LRU_C = 8.0


def _cparams(*sem):
    return pltpu.CompilerParams(dimension_semantics=sem, vmem_limit_bytes=VMEM_LIMIT_BYTES)


def _bdot(a, b):
    return jnp.dot(a.astype(BF16), b.astype(BF16), preferred_element_type=F32)


def _bdot_nt(a, b):
    return lax.dot_general(a.astype(BF16), b.astype(BF16), (((1,), (1,)), ((), ())),
                           preferred_element_type=F32)


def _bdot_tn(a, b):
    return lax.dot_general(a.astype(BF16), b.astype(BF16), (((0,), (0,)), ((), ())),
                           preferred_element_type=F32)


def _split3(x):
    x1 = x.astype(BF16)
    r1 = x - x1.astype(F32)
    x2 = r1.astype(BF16)
    x3 = (r1 - x2.astype(F32)).astype(BF16)
    return x1, x2, x3


def _dot_lhs01(m01, x):
    m = m01.astype(BF16)
    x1, x2, x3 = _split3(x)
    d = lambda v: jnp.dot(m, v, preferred_element_type=F32)
    return d(x1) + d(x2) + d(x3)


def _dot_rhs01(x, m01):
    m = m01.astype(BF16)
    x1, x2, x3 = _split3(x)
    d = lambda v: jnp.dot(v, m, preferred_element_type=F32)
    return d(x1) + d(x2) + d(x3)


def _softplus(x):
    return jnp.maximum(x, 0.0) + jnp.log1p(jnp.exp(-jnp.abs(x)))


def _sigmoid(x):
    return jax.nn.sigmoid(x)


def _gelu(x):
    return jax.nn.gelu(x, approximate=True)


def _silu(x):
    return x * _sigmoid(x)


def _row_iota(shape):
    return lax.broadcasted_iota(jnp.int32, shape, 0)


def _delayed(x, tail_ref, delays):
    rows, c = x.shape
    nt = rows // V7X_SUBLANES
    x3 = x.reshape(nt, V7X_SUBLANES, c)
    tail = tail_ref[...].reshape(1, V7X_SUBLANES, c)
    sub = lax.broadcasted_iota(jnp.int32, (nt, V7X_SUBLANES, c), 1)
    out = []
    for d in delays:
        cur = pltpu.roll(x3, d, 1)
        prev = jnp.concatenate([pltpu.roll(tail, d, 1), cur[:nt - 1]], axis=0)
        out.append(jnp.where(sub >= d, cur, prev).reshape(rows, c))
    tail_ref[...] = x[rows - V7X_SUBLANES:, :]
    return out


def _causal_conv(x, ext_ref, w_ref, b_ref):
    acc = b_ref[...] + w_ref[CONV_K - 1:CONV_K, :] * x
    for k, xd in enumerate(_delayed(x, ext_ref, range(CONV_K - 1, 0, -1))):
        acc = acc + w_ref[k:k + 1, :] * xd
    return acc


def _with_weight_cast(body, n_in):
    def kernel(*refs):
        ins, wi_ref, o_ref, wo_ref, scratch = refs[:n_in], refs[n_in], refs[n_in + 1], refs[n_in + 2], refs[n_in + 3:]
        wo_ref[...] = wi_ref[...].astype(BF16)
        body(*ins, o_ref, *scratch)
    return kernel


def _cast_slab(w_all, layer, bsz, nt):
    _, rows, cols = w_all.shape
    blk = rows // (bsz * nt)
    assert blk * bsz * nt == rows and blk % V7X_BF16_SUBLANES == 0, (rows, bsz, nt)
    return (pl.BlockSpec((None, blk, cols), lambda b, t: (layer, b * nt + t, 0)),
            pl.BlockSpec((blk, cols), lambda b, t: (b * nt + t, 0)),
            jax.ShapeDtypeStruct((rows, cols), BF16))


def _regroup_kernel(w_ref, o_ref, *, segments):
    o_ref[...] = jnp.zeros_like(o_ref)
    for src, width, dst in segments:
        o_ref[:, dst:dst + width] = w_ref[:, src:src + width].astype(BF16)


def _regroup_columns(w_all, index, segments, n_out, *, rows_blk=256):
    _, k, n_in = w_all.shape
    rows_blk = min(rows_blk, k)
    assert k % rows_blk == 0 and all(dst % V7X_LANES == 0 for _, _, dst in segments)
    return pl.pallas_call(
        functools.partial(_regroup_kernel, segments=tuple(segments)),
        grid=(k // rows_blk,),
        in_specs=[pl.BlockSpec((None, rows_blk, n_in), lambda i: (index, i, 0))],
        out_specs=pl.BlockSpec((rows_blk, n_out), lambda i: (i, 0)),
        out_shape=jax.ShapeDtypeStruct((k, n_out), BF16),
        compiler_params=_cparams("parallel"),
        name="regroup_columns",
    )(w_all.astype(F32))


def _rmsnorm_rows(h, g):
    ms = jnp.mean(h * h, axis=-1, keepdims=True)
    return h * lax.rsqrt(ms + NORM_EPS) * g


def _norm_mm_kernel(h_ref, g_ref, w_ref, o_ref):
    hn = _rmsnorm_rows(h_ref[...], g_ref[...]).astype(BF16)
    o_ref[...] = jnp.dot(hn, w_ref[...], preferred_element_type=F32)


def _col_tile(n, target):
    return max(t for t in range(V7X_LANES, max(target, V7X_LANES) + 1, V7X_LANES) if n % t == 0)


def _norm_matmul(h, g, w, *, tm, tn):
    m, k = h.shape
    n = w.shape[1]
    tm, tn = min(tm, m), _col_tile(n, tn)
    assert m % tm == 0 and n % tn == 0, (m, n, tm, tn)
    return pl.pallas_call(
        _norm_mm_kernel,
        grid=(m // tm, n // tn),
        in_specs=[pl.BlockSpec((tm, k), lambda i, j: (i, 0)),
                  pl.BlockSpec((1, k), lambda i, j: (0, 0)),
                  pl.BlockSpec((k, tn), lambda i, j: (0, j))],
        out_specs=pl.BlockSpec((tm, tn), lambda i, j: (i, j)),
        out_shape=jax.ShapeDtypeStruct((m, n), F32),
        compiler_params=_cparams("parallel", "parallel"),
        name="norm_matmul",
    )(h, g.reshape(1, k), w)


def _mm_kernel(a_ref, w_ref, o_ref):
    o_ref[...] = jnp.dot(a_ref[...], w_ref[...], preferred_element_type=F32)


def _in_proj(h, hn, g, w, *, tm=1024, tn=1536):
    if hn is None:
        return _norm_matmul(h, g, w, tm=tm, tn=tn)
    m, k = hn.shape
    n = w.shape[1]
    tm, tn = min(tm, m), _col_tile(n, tn)
    assert m % tm == 0
    return pl.pallas_call(
        _mm_kernel,
        grid=(m // tm, n // tn),
        in_specs=[pl.BlockSpec((tm, k), lambda i, j: (i, 0)), pl.BlockSpec((k, tn), lambda i, j: (0, j))],
        out_specs=pl.BlockSpec((tm, tn), lambda i, j: (i, j)),
        out_shape=jax.ShapeDtypeStruct((m, n), F32),
        compiler_params=_cparams("parallel", "parallel"),
        name="in_proj",
    )(hn, w)


def _out_proj_kernel(a1_ref, a2_ref, w_ref, r_ref, g_ref, o_ref, on_ref):
    k1 = a1_ref.shape[1]
    ho = (r_ref[...] + jnp.dot(a1_ref[...], w_ref[:k1, :], preferred_element_type=F32)
          + jnp.dot(a2_ref[...], w_ref[k1:, :], preferred_element_type=F32))
    o_ref[...] = ho
    on_ref[...] = _rmsnorm_rows(ho, g_ref[...]).astype(BF16)


def _out_proj_residual(a1, a2, w, res, g_next, *, tm=512):
    m, k1 = a1.shape
    k2 = a2.shape[1]
    n = w.shape[1]
    tm = min(tm, m)
    assert m % tm == 0
    rows = lambda width: pl.BlockSpec((tm, width), lambda i: (i, 0))
    const = lambda shape: pl.BlockSpec(shape, lambda i: (0, 0))
    return pl.pallas_call(
        _out_proj_kernel,
        grid=(m // tm,),
        in_specs=[rows(k1), rows(k2), const((k1 + k2, n)), rows(n), const((1, n))],
        out_specs=[rows(n), rows(n)],
        out_shape=[jax.ShapeDtypeStruct((m, n), F32), jax.ShapeDtypeStruct((m, n), BF16)],
        compiler_params=_cparams("parallel"),
        name="out_proj",
    )(a1, a2, w, res, g_next.reshape(1, n))


def _mlp_kernel(h_ref, hn_ref, w1_ref, w2_ref, o_ref, acc_ref, *, nf):
    j = pl.program_id(1)

    @pl.when(j == 0)
    def _():
        acc_ref[...] = jnp.zeros_like(acc_ref)

    mid = jnp.dot(hn_ref[...], w1_ref[...], preferred_element_type=F32)
    mid = jnp.square(jnp.maximum(mid, 0.0)).astype(BF16)
    acc_ref[...] += jnp.dot(mid, w2_ref[...], preferred_element_type=F32)

    @pl.when(j == nf - 1)
    def _():
        o_ref[...] = h_ref[...] + acc_ref[...]


def _mlp_residual(h, hn, w1, w2, *, tm=512, tf=1024):
    m, d = h.shape
    f = w1.shape[1]
    tm, tf = min(tm, m), min(tf, f)
    assert m % tm == 0 and f % tf == 0
    nf = f // tf
    return pl.pallas_call(
        functools.partial(_mlp_kernel, nf=nf),
        grid=(m // tm, nf),
        in_specs=[pl.BlockSpec((tm, d), lambda i, j: (i, 0)),
                  pl.BlockSpec((tm, d), lambda i, j: (i, 0)),
                  pl.BlockSpec((d, tf), lambda i, j: (0, j)),
                  pl.BlockSpec((tf, d), lambda i, j: (j, 0))],
        out_specs=pl.BlockSpec((tm, d), lambda i, j: (i, 0)),
        out_shape=jax.ShapeDtypeStruct((m, d), F32),
        scratch_shapes=[pltpu.VMEM((tm, d), F32)],
        compiler_params=_cparams("parallel", "arbitrary"),
        name="mlp",
    )(h, hn, w1, w2)


def _gate_kernel(h_ref, g_ref, wg_ref, p_ref, wp_ref, gn_ref, *o_refs, final):
    h = h_ref[...]
    hn = _rmsnorm_rows(h, g_ref[...]).astype(BF16)
    gate = _sigmoid(jnp.dot(hn, wg_ref[...], preferred_element_type=F32))
    pp = jnp.dot(p_ref[...].astype(BF16), wp_ref[...], preferred_element_type=F32)
    ho = h + gate * pp
    nxt = _rmsnorm_rows(ho, gn_ref[...])
    if final:
        o_refs[0][...] = nxt
    else:
        o_refs[0][...] = ho
        o_refs[1][...] = nxt.astype(BF16)


def _gated_embed(h, g, wg, p, layer, wp, g_next, *, final, tm=512):
    m, k = h.shape
    kp = p.shape[2]
    tm = min(tm, m)
    assert m % tm == 0 and wg.shape[1:] == (k, k)
    const = lambda shape: pl.BlockSpec(shape, lambda i: (0, 0))
    rows = lambda w: pl.BlockSpec((tm, w), lambda i: (i, 0))
    out_specs = rows(k) if final else [rows(k), rows(k)]
    out_shape = (jax.ShapeDtypeStruct((m, k), F32) if final
                 else [jax.ShapeDtypeStruct((m, k), F32), jax.ShapeDtypeStruct((m, k), BF16)])
    return pl.pallas_call(
        functools.partial(_gate_kernel, final=final),
        grid=(m // tm,),
        in_specs=[rows(k), const((1, k)), pl.BlockSpec((None, k, k), lambda i: (layer, 0, 0)),
                  pl.BlockSpec((None, tm, kp), lambda i: (layer, i, 0)),
                  pl.BlockSpec((None, kp, k), lambda i: (layer, 0, 0)), const((1, k))],
        out_specs=out_specs,
        out_shape=out_shape,
        compiler_params=_cparams("parallel"),
        name="gated_embed",
    )(h, g.reshape(1, k), wg, p, wp, g_next.reshape(1, k))


def _s5_prep_kernel(lr_ref, li_ref, st_ref, btr_ref, bti_ref, b8_ref, a8_re_ref, a8_im_ref):
    lr, li, st = lr_ref[...], li_ref[...], jnp.exp(st_ref[...])
    lrs, lis = lr * st, li * st

    def apow(n):
        mag = jnp.exp(n * lrs)
        return mag * jnp.cos(n * lis), mag * jnp.sin(n * lis)

    mag = jnp.exp(lrs)
    abar_re, abar_im = mag * jnp.cos(lis), mag * jnp.sin(lis)
    den = lr * lr + li * li
    nr = abar_re - 1.0
    coef_re = (nr * lr + abar_im * li) / den
    coef_im = (abar_im * lr - nr * li) / den
    btr, bti = btr_ref[...], bti_ref[...]
    bbar_re = coef_re * btr - coef_im * bti
    bbar_im = coef_re * bti + coef_im * btr
    nblk, taps_ch, lanes2 = b8_ref.shape
    lane_blk = lanes2 // 2
    ch_blk = taps_ch // V7X_SUBLANES
    rg = lax.broadcasted_iota(jnp.int32, (ch_blk, lane_blk), 0) // S5_GROUP
    cg = lax.broadcasted_iota(jnp.int32, (ch_blk, lane_blk), 1) // S5_STATE
    same = rg == cg
    for s in range(V7X_SUBLANES):
        if s == 0:
            br, bi = bbar_re, bbar_im
        else:
            pr, pi = apow(float(s))
            br, bi = pr * bbar_re - pi * bbar_im, pr * bbar_im + pi * bbar_re
        for j in range(nblk):
            ls = slice(j * lane_blk, (j + 1) * lane_blk)
            rs = slice(s * ch_blk, (s + 1) * ch_blk)
            tile = lambda m: jnp.where(same, jnp.concatenate([m[:, ls]] * (ch_blk // S5_GROUP), axis=0), 0.0)
            b8_ref[j, rs, :lane_blk] = tile(br).astype(BF16)
            b8_ref[j, rs, lane_blk:] = tile(bi).astype(BF16)
    a8r, a8i = apow(float(V7X_SUBLANES))
    a8_re_ref[...] = jnp.broadcast_to(a8r, a8_re_ref.shape)
    a8_im_ref[...] = jnp.broadcast_to(a8i, a8_im_ref.shape)


def _s5_kernel(u_ref, b8_ref, cre_ref, cim_ref, a8r_ref, a8i_ref, d_ref, gw_ref, gb_ref, o_ref,
               ue_ref, car_re, car_im):
    @pl.when(pl.program_id(1) == 0)
    def _():
        ue_ref[...] = jnp.zeros_like(ue_ref)
        car_re[...] = jnp.zeros_like(car_re)
        car_im[...] = jnp.zeros_like(car_im)

    u = u_ref[...]
    tt = u.shape[0]
    nblk, taps_ch, lanes2 = b8_ref.shape
    lane_blk = lanes2 // 2
    ch_blk = taps_ch // V7X_SUBLANES
    taps = [u] + _delayed(u, ue_ref, range(1, V7X_SUBLANES))
    ys = []
    for j in range(nblk):
        ls = slice(j * lane_blk, (j + 1) * lane_blk)
        cs = slice(j * ch_blk, (j + 1) * ch_blk)
        u8 = jnp.concatenate([t[:, cs].astype(BF16) for t in taps], axis=1)
        z = jnp.dot(u8, b8_ref[j], preferred_element_type=F32)
        pr, pi = a8r_ref[:, ls], a8i_ref[:, ls]
        cr, ci = car_re[:, ls], car_im[:, ls]
        xr, xi = [], []
        for i in range(tt // V7X_SUBLANES):
            rs = slice(i * V7X_SUBLANES, (i + 1) * V7X_SUBLANES)
            cr, ci = (z[rs, :lane_blk] + (pr * cr - pi * ci), z[rs, lane_blk:] + (pr * ci + pi * cr))
            xr.append(cr)
            xi.append(ci)
        car_re[:, ls] = cr
        car_im[:, ls] = ci
        ys.append(_bdot(jnp.concatenate(xr, axis=0), cre_ref[j]) - _bdot(jnp.concatenate(xi, axis=0), cim_ref[j]))
    y = jnp.concatenate(ys, axis=1) + d_ref[...] * u
    act = _gelu(y)
    o_ref[...] = (act * _sigmoid(_bdot(act, gw_ref[...]) + gb_ref[...])).astype(o_ref.dtype)


def _s5_mixer(proj, col_blk, bsz, seq, lam_re, lam_im, log_step, b_re, b_im, c_re, c_im,
              d_skip, glu_w, glu_b, cast_w, cast_layer, *, tt=512, lane_blk=128):
    ng, ns = lam_re.shape
    width = ng * S5_GROUP
    nstate = ng * ns
    tt = min(tt, seq)
    assert seq % tt == 0 and tt % V7X_SUBLANES == 0 and nstate % lane_blk == 0 and lane_blk % ns == 0
    nblk = nstate // lane_blk
    gpb = lane_blk // ns
    ch_blk = gpb * S5_GROUP
    flat = lambda a: a.astype(F32).reshape(1, nstate)
    step = jnp.broadcast_to(log_step.astype(F32)[:, None], (ng, ns))
    bt = lambda a: a.astype(F32).transpose(2, 0, 1).reshape(S5_GROUP, nstate)
    vm = pl.BlockSpec(memory_space=pltpu.VMEM)
    b8, a8_re, a8_im = pl.pallas_call(
        _s5_prep_kernel,
        in_specs=[vm] * 5,
        out_specs=[vm] * 3,
        out_shape=[jax.ShapeDtypeStruct((nblk, V7X_SUBLANES * ch_blk, 2 * lane_blk), BF16)]
        + [jax.ShapeDtypeStruct((V7X_SUBLANES, nstate), F32)] * 2,
        compiler_params=pltpu.CompilerParams(vmem_limit_bytes=VMEM_LIMIT_BYTES),
        name="s5_prep",
    )(flat(lam_re), flat(lam_im), flat(step), bt(b_re), bt(b_im))
    eye = jnp.eye(gpb, dtype=F32)
    cbd = lambda c: (c.astype(F32).reshape(nblk, gpb, S5_GROUP, ns).transpose(0, 1, 3, 2)[:, :, :, None, :]
                     * eye[None, :, None, :, None]).reshape(nblk, lane_blk, ch_blk).astype(BF16)
    nt = seq // tt
    cast_in, cast_out, cast_shape = _cast_slab(cast_w, cast_layer, bsz, nt)
    const = lambda shape: pl.BlockSpec(shape, lambda b, t: (0,) * len(shape))
    return pl.pallas_call(
        _with_weight_cast(_s5_kernel, 9),
        grid=(bsz, nt),
        in_specs=[pl.BlockSpec((tt, width), lambda b, t: (b * nt + t, col_blk)),
                  const((nblk, V7X_SUBLANES * ch_blk, 2 * lane_blk)),
                  const((nblk, lane_blk, ch_blk)), const((nblk, lane_blk, ch_blk)),
                  const((V7X_SUBLANES, nstate)), const((V7X_SUBLANES, nstate)),
                  const((1, width)), const((width, width)), const((1, width)), cast_in],
        out_specs=[pl.BlockSpec((tt, width), lambda b, t: (b * nt + t, 0)), cast_out],
        out_shape=[jax.ShapeDtypeStruct((bsz * seq, width), BF16), cast_shape],
        scratch_shapes=[pltpu.VMEM((V7X_SUBLANES, width), F32)]
        + [pltpu.VMEM((V7X_SUBLANES, nstate), F32)] * 2,
        compiler_params=_cparams("parallel", "arbitrary"),
        name="s5_mixer",
    )(proj, b8, cbd(c_re), cbd(c_im), a8_re, a8_im,
      d_skip.astype(F32).reshape(1, width), glu_w.astype(BF16), glu_b.astype(F32).reshape(1, width), cast_w)


def _ssd_kernel(z_ref, xs_ref, bc_ref, dt_ref, cwx_ref, cbx_ref, cwb_ref, cbb_ref, e64_ref,
                dtb_ref, alog_ref, dskx_ref, ng_ref, o_ref,
                tailx, tailb, st_ref):
    @pl.when(pl.program_id(1) == 0)
    def _():
        tailx[...] = jnp.zeros_like(tailx)
        tailb[...] = jnp.zeros_like(tailb)
        st_ref[...] = jnp.zeros_like(st_ref)

    L = SSD_CHUNK
    tt = xs_ref.shape[0]
    width = xs_ref.shape[1]
    gstate = SSD_GROUPS * SSD_STATE
    xs = _silu(_causal_conv(xs_ref[...], tailx, cwx_ref, cbx_ref))
    bc = _silu(_causal_conv(bc_ref[...], tailb, cwb_ref, cbb_ref))
    dt = _softplus(dt_ref[...] + dtb_ref[...])
    da = dt * (-jnp.exp(alog_ref[...]))
    e64 = e64_ref[...]
    xdt = xs * _dot_rhs01(dt, e64)
    ri = lax.broadcasted_iota(jnp.int32, (L, L), 0)
    ci = lax.broadcasted_iota(jnp.int32, (L, L), 1)
    tril = ri >= ci
    tri01 = tril.astype(BF16)
    lane_lo = lax.broadcasted_iota(jnp.int32, (1, V7X_LANES), 1) < SSD_HEAD_DIM
    pairs_per_group = SSD_HEADS // SSD_GROUPS // 2
    ys = []
    for c in range(tt // L):
        rs = slice(c * L, (c + 1) * L)
        acum = _dot_lhs01(tri01, da[rs, :])
        acum_t = acum.T
        acx = _dot_rhs01(acum, e64)
        ycols = []
        for g in range(SSD_GROUPS):
            bg = bc[rs, g * SSD_STATE:(g + 1) * SSD_STATE]
            cg = bc[rs, gstate + g * SSD_STATE:gstate + (g + 1) * SSD_STATE]
            scores = _bdot_nt(cg, bg).astype(BF16)
            bg_t = bg.T
            for pr in range(pairs_per_group):
                q = g * pairs_per_group + pr
                ls = slice(q * V7X_LANES, (q + 1) * V7X_LANES)
                xp = xdt[rs, ls]
                acp = acx[:, ls]
                yd = []
                for h in (2 * q, 2 * q + 1):
                    seg = acum[:, h:h + 1] - acum_t[h:h + 1, :]
                    decay = jnp.exp(jnp.where(tril, seg, -jnp.inf).astype(BF16))
                    yd.append(jnp.dot(scores * decay, xp.astype(BF16), preferred_element_type=F32))
                y = jnp.where(lane_lo, yd[0], yd[1])
                prev_t = st_ref[q]
                y = y + _bdot(cg, prev_t) * jnp.exp(acp)
                last = acp[L - 1:L, :]
                st_ref[q] = prev_t * jnp.exp(last) + _bdot(bg_t, xp * jnp.exp(last - acp))
                ycols.append(y)
        ys.append(jnp.concatenate(ycols, axis=1))
    y = jnp.concatenate(ys, axis=0) if len(ys) > 1 else ys[0]
    y = (y + xs * dskx_ref[...]) * _silu(z_ref[...])
    gw = width // SSD_GROUPS
    outs = []
    for g in range(SSD_GROUPS):
        yg = y[:, g * gw:(g + 1) * gw]
        outs.append(yg * lax.rsqrt(jnp.mean(yg * yg, axis=-1, keepdims=True) + NORM_EPS))
    o_ref[...] = (jnp.concatenate(outs, axis=1) * ng_ref[...]).astype(o_ref.dtype)


def _ssd_mixer(proj, blk_z, blk_xs, blk_bc, blk_dt, bsz, seq, conv_w, conv_b, dt_bias, a_log,
               d_skip, norm_g, cast_w, cast_layer, *, tt=512):
    width = SSD_HEADS * SSD_HEAD_DIM
    gstate2 = 2 * SSD_GROUPS * SSD_STATE
    tt = min(tt, seq)
    assert seq % tt == 0 and tt % SSD_CHUNK == 0
    nt = seq // tt
    cast_in, cast_out, cast_shape = _cast_slab(cast_w, cast_layer, bsz, nt)
    f = lambda a: a.astype(F32)
    pad_h = lambda a: jnp.pad(f(a), (0, V7X_LANES - SSD_HEADS)).reshape(1, V7X_LANES)
    exp_h = lambda a: jnp.repeat(f(a), SSD_HEAD_DIM).reshape(1, width)
    e64 = (jnp.arange(V7X_LANES)[:, None] == (jnp.arange(width)[None, :] // SSD_HEAD_DIM)).astype(BF16)
    const = lambda shape: pl.BlockSpec(shape, lambda b, t: (0,) * len(shape))
    rowblk = lambda w, cb: pl.BlockSpec((tt, w), lambda b, t: (b * nt + t, cb))
    return pl.pallas_call(
        _with_weight_cast(_ssd_kernel, 13),
        grid=(bsz, nt),
        in_specs=[rowblk(width, blk_z), rowblk(width, blk_xs), rowblk(gstate2, blk_bc),
                  rowblk(V7X_LANES, blk_dt),
                  const((CONV_K, width)), const((1, width)), const((CONV_K, gstate2)), const((1, gstate2)),
                  const((V7X_LANES, width)),
                  const((1, V7X_LANES)), const((1, V7X_LANES)),
                  const((1, width)), const((1, width)), cast_in],
        out_specs=[pl.BlockSpec((tt, width), lambda b, t: (b * nt + t, 0)), cast_out],
        out_shape=[jax.ShapeDtypeStruct((bsz * seq, width), BF16), cast_shape],
        scratch_shapes=[pltpu.VMEM((V7X_SUBLANES, width), F32),
                        pltpu.VMEM((V7X_SUBLANES, gstate2), F32),
                        pltpu.VMEM((SSD_HEADS // 2, SSD_STATE, V7X_LANES), F32)],
        compiler_params=_cparams("parallel", "arbitrary"),
        name="ssd_mixer",
    )(proj, proj, proj, proj,
      f(conv_w[:, :width]), f(conv_b[:width]).reshape(1, width),
      f(conv_w[:, width:]), f(conv_b[width:]).reshape(1, gstate2),
      e64, pad_h(dt_bias), pad_h(a_log), exp_h(d_skip),
      f(norm_g).reshape(1, width), cast_w)


def _stack_heads(x, lane_lo):
    return jnp.concatenate([jnp.where(lane_lo, x, 0.0), jnp.where(lane_lo, 0.0, x)], axis=0)


def _unit_lower_inverses(mats, same_blk, eye):
    size = eye.shape[0]
    side = lambda x, y: _bdot(x, jnp.concatenate([x, y], axis=1))
    ad = [jnp.where(same_blk, a, 0.0) for a in mats]
    ao = [a - d for a, d in zip(mats, ad)]
    s1 = [eye + d for d in ad]
    a2 = [_bdot(d, d) for d in ad]
    r = [side(x, s) for x, s in zip(a2, s1)]
    a4, s2 = [x[:, :size] for x in r], [s + x[:, size:] for s, x in zip(s1, r)]
    r = [side(x, s) for x, s in zip(a4, s2)]
    a8, s3 = [x[:, :size] for x in r], [s + x[:, size:] for s, x in zip(s2, r)]
    td = [s + _bdot(x, s) for x, s in zip(a8, s3)]
    n = [_bdot(t, o) for t, o in zip(td, ao)]
    r = [side(x, t) for x, t in zip(n, td)]
    n2, w1 = [x[:, :size] for x in r], [t + x[:, size:] for t, x in zip(td, r)]
    return [w + _bdot(x, w) for x, w in zip(n2, w1)]


def _rwkv_kernel(rkv_ref, lo_ref, mu_rkv_ref, mu_lo_ref, w0_ref, wup_ref, a0_ref, aup_ref, gup_ref,
                 kk_ref, ka_ref, rk_ref, lng_ref, lnb_ref, ones_ref, o_ref,
                 prev_rkv, prev_lo, state_ref):
    @pl.when(pl.program_id(1) == 0)
    def _():
        prev_rkv[...] = jnp.zeros_like(prev_rkv)
        prev_lo[...] = jnp.zeros_like(prev_lo)
        state_ref[...] = jnp.zeros_like(state_ref)

    L = RWKV_CHUNK
    tb = rkv_ref.shape[0]
    width = rkv_ref.shape[1] // 3

    def shift_mix(ref, prev, mu_ref):
        f = ref[...]
        fs = jnp.where(_row_iota(f.shape) == 0, prev[0:1, :], pltpu.roll(f, 1, 0))
        prev[0:1, :] = f[tb - 1:tb, :]
        return f + (fs - f) * mu_ref[...]

    rkv = shift_mix(rkv_ref, prev_rkv, mu_rkv_ref)
    lo = shift_mix(lo_ref, prev_lo, mu_lo_ref)
    r, k, v = rkv[:, :width], rkv[:, width:2 * width], rkv[:, 2 * width:]
    nl = wup_ref.shape[0]
    wl, al, gl = lo[:, :nl], lo[:, nl:2 * nl], lo[:, 2 * nl:]
    w = -_softplus(-(w0_ref[...] + _bdot(jnp.tanh(wl), wup_ref[...]))) - 0.5
    logw = -jnp.exp(w)
    a_sig = _sigmoid(a0_ref[...] + _bdot(al, aup_ref[...]))
    g = _bdot(_sigmoid(gl), gup_ref[...])
    kk = k * kk_ref[...]
    k = k * (1.0 + (a_sig - 1.0) * ka_ref[...])
    ones_bd = ones_ref[...]

    def head_sum(x):
        return jnp.dot(x.astype(BF16), ones_bd, preferred_element_type=F32)

    P2 = 2 * L
    ri = lax.broadcasted_iota(jnp.int32, (P2, P2), 0)
    ci = lax.broadcasted_iota(jnp.int32, (P2, P2), 1)
    same_head = (ri // L) == (ci // L)
    strict = same_head & (ri > ci)
    incl = same_head & (ri >= ci)
    same_blk = (ri // RWKV_INV_BLOCK) == (ci // RWKV_INV_BLOCK)
    eye = (ri == ci).astype(F32)
    tri01 = (lax.broadcasted_iota(jnp.int32, (L, L), 0) >= lax.broadcasted_iota(jnp.int32, (L, L), 1)).astype(BF16)
    lane_lo = lax.broadcasted_iota(jnp.int32, (1, V7X_LANES), 1) < RWKV_HEAD_DIM

    nq = width // V7X_LANES
    lss = [slice(q * V7X_LANES, (q + 1) * V7X_LANES) for q in range(nq)]
    kks = [kk[:, ls] for ls in lss]
    kkn = [x * lax.rsqrt(jnp.maximum(head_sum(x * x), 1e-24)) for x in kks]
    nc = tb // L
    inst = [(slice(c * L, (c + 1) * L), q, ls) for c in range(nc) for q, ls in enumerate(lss)]
    cums = [_dot_lhs01(tri01, logw[c * L:(c + 1) * L, :]) for c in range(nc)]
    g_in = jnp.concatenate([jnp.exp(cm) for cm in cums], axis=0)
    g_prev = jnp.concatenate([jnp.exp(cm - logw[c * L:(c + 1) * L, :]) for c, cm in enumerate(cums)], axis=0)
    g_inv = jnp.concatenate([jnp.exp(-cm) for cm in cums], axis=0)
    ats = [_stack_heads(-kkn[q][rs] * g_prev[rs, ls], lane_lo) for rs, q, ls in inst]
    rts = [_stack_heads(r[rs, ls] * g_in[rs, ls], lane_lo) for rs, q, ls in inst]
    bts = [_stack_heads(kkn[q][rs] * a_sig[rs, ls] * g_inv[rs, ls], lane_lo) for rs, q, ls in inst]
    kts = [_stack_heads(k[rs, ls] * g_inv[rs, ls], lane_lo) for rs, q, ls in inst]
    vss = [_stack_heads(v[rs, ls], lane_lo).astype(BF16) for rs, q, ls in inst]
    ars = [jnp.concatenate([a, b], axis=0).astype(BF16) for a, b in zip(ats, rts)]
    ps = [_bdot_nt(ar, jnp.concatenate([b, kq], axis=0)) for ar, b, kq in zip(ars, bts, kts)]
    tinv = _unit_lower_inverses([jnp.where(strict, p[:P2, :P2], 0.0) for p in ps], same_blk, eye)
    akv = [_bdot(jnp.where(strict, p[:P2, P2:], 0.0), vs) for p, vs in zip(ps, vss)]
    a_r = [jnp.concatenate([jnp.where(incl, p[P2:, :P2], 0.0), jnp.where(incl, p[P2:, P2:], 0.0)],
                           axis=1).astype(BF16) for p in ps]
    g_last = [g_in[rs, ls][L - 1:L, :] for rs, q, ls in inst]
    bkl = [(jnp.concatenate([b, kq], axis=0) * gl_).astype(BF16) for b, kq, gl_ in zip(bts, kts, g_last)]
    hts = [state_ref[q] for q in range(nq)]
    yrows = []
    for c in range(nc):
        sel = lambda xs: xs[c * nq:(c + 1) * nq]
        arh = [_bdot_nt(ar, ht) for ar, ht in zip(sel(ars), hts)]
        us = [_bdot(t, x[:P2] + y) for t, x, y in zip(sel(tinv), arh, sel(akv))]
        uv = [jnp.concatenate([u.astype(BF16), vs], axis=0) for u, vs in zip(us, sel(vss))]
        hts = [ht * gl_ + _bdot_tn(w, b) for ht, gl_, w, b in zip(hts, sel(g_last), uv, sel(bkl))]
        ys = [x[P2:] + jnp.dot(m, w, preferred_element_type=F32) for x, m, w in zip(arh, sel(a_r), uv)]
        yrows.append(jnp.concatenate([y[:L] + y[L:] for y in ys], axis=1))
    for q in range(nq):
        state_ref[q] = hts[q]
    y = jnp.concatenate(yrows, axis=0) if len(yrows) > 1 else yrows[0]
    inv_n = 1.0 / RWKV_HEAD_DIM
    outs = []
    for q, ls in enumerate(lss):
        yq = y[:, ls]
        mean = head_sum(yq) * inv_n
        yc = yq - mean
        var = head_sum(yc * yc) * inv_n
        yn = yc * lax.rsqrt(var + RWKV_GN_EPS) * lng_ref[:, ls] + lnb_ref[:, ls]
        bonus = head_sum(r[:, ls] * k[:, ls] * rk_ref[:, ls]) * v[:, ls]
        outs.append((yn + bonus) * g[:, ls])
    o_ref[...] = jnp.concatenate(outs, axis=1).astype(o_ref.dtype)


def _rwkv_mixer(proj, blk_rkv, blk_lo, bsz, seq, mu_rkv, mu_lo, w0, w_up, a0, a_up, g_up,
                k_k, k_a, r_k, ln_g, ln_b, cast_w, cast_layer, *, tb=256):
    width = w0.shape[0]
    nl = w_up.shape[0]
    tb = min(tb, seq)
    assert seq % tb == 0 and tb % RWKV_CHUNK == 0
    nt = seq // tb
    cast_in, cast_out, cast_shape = _cast_slab(cast_w, cast_layer, bsz, nt)
    f = lambda a: a.astype(F32).reshape(1, -1)
    lane = jnp.arange(V7X_LANES) // RWKV_HEAD_DIM
    ones_bd = (lane[:, None] == lane[None, :]).astype(BF16)
    const = lambda shape: pl.BlockSpec(shape, lambda b, t: (0,) * len(shape))
    return pl.pallas_call(
        _with_weight_cast(_rwkv_kernel, 15),
        grid=(bsz, nt),
        in_specs=[pl.BlockSpec((tb, 3 * width), lambda b, t: (b * nt + t, blk_rkv)),
                  pl.BlockSpec((tb, 4 * nl), lambda b, t: (b * nt + t, blk_lo)),
                  const((1, 3 * width)), const((1, 4 * nl)),
                  const((1, width)), const((nl, width)), const((1, width)), const((nl, width)),
                  const((2 * nl, width)),
                  const((1, width)), const((1, width)), const((1, width)), const((1, width)),
                  const((1, width)), const((V7X_LANES, V7X_LANES)), cast_in],
        out_specs=[pl.BlockSpec((tb, width), lambda b, t: (b * nt + t, 0)), cast_out],
        out_shape=[jax.ShapeDtypeStruct((bsz * seq, width), BF16), cast_shape],
        scratch_shapes=[pltpu.VMEM((V7X_SUBLANES, 3 * width), F32),
                        pltpu.VMEM((V7X_SUBLANES, 4 * nl), F32),
                        pltpu.VMEM((width // V7X_LANES, V7X_LANES, V7X_LANES), F32)],
        compiler_params=_cparams("parallel", "arbitrary"),
        name="rwkv_mixer",
    )(proj, proj, mu_rkv, mu_lo, f(w0), w_up.astype(BF16), f(a0), a_up.astype(BF16),
      g_up.astype(BF16), f(k_k), f(k_a), f(r_k), f(ln_g), f(ln_b), ones_bd, cast_w)


def _lru_kernel(xl_ref, gl_ref, cw_ref, cb_ref, wa_ref, ba_ref, wx_ref, bx_ref, lam_ref, o_ref,
                tail, carry):
    @pl.when(pl.program_id(1) == 0)
    def _():
        tail[...] = jnp.zeros_like(tail)
        carry[...] = jnp.zeros_like(carry)

    tt, width = xl_ref.shape
    row = _row_iota((tt, width))
    xc = _causal_conv(xl_ref[...], tail, cw_ref, cb_ref)
    gr, gi = [], []
    for j in range(width // V7X_LANES):
        xj = xc[:, j * V7X_LANES:(j + 1) * V7X_LANES]
        gr.append(_bdot(xj, wa_ref[j]))
        gi.append(_bdot(xj, wx_ref[j]))
    gate_r = _sigmoid(jnp.concatenate(gr, axis=1) + ba_ref[...])
    gate_i = _sigmoid(jnp.concatenate(gi, axis=1) + bx_ref[...])
    log_a = -LRU_C * gate_r * _softplus(-lam_ref[...])
    a = jnp.exp(log_a)
    mult = jnp.sqrt(jnp.maximum(-(jnp.tanh(log_a) * (a * a + 1.0)), 0.0))
    mult = jnp.where(row + pl.program_id(1) * tt == 0, 1.0, mult)
    b = xc * gate_i * mult
    nt = tt // V7X_SUBLANES
    a, b = a.reshape(nt, V7X_SUBLANES, width), b.reshape(nt, V7X_SUBLANES, width)
    sub = lax.broadcasted_iota(jnp.int32, a.shape, 1)
    for s in (1, 2, 4):
        keep = sub >= s
        sa, sb = pltpu.roll(a, s, 1), pltpu.roll(b, s, 1)
        b = b + jnp.where(keep, a * sb, 0.0)
        a = jnp.where(keep, a * sa, a)
    a, b = a.reshape(tt, width), b.reshape(tt, width)
    c = carry[0:1, :]
    hs = []
    for i in range(tt // V7X_SUBLANES):
        rs = slice(i * V7X_SUBLANES, (i + 1) * V7X_SUBLANES)
        hi = b[rs, :] + a[rs, :] * c
        hs.append(hi)
        c = hi[V7X_SUBLANES - 1:, :]
    carry[0:1, :] = c
    h = jnp.concatenate(hs, axis=0)
    o_ref[...] = (h * _gelu(gl_ref[...])).astype(o_ref.dtype)


def _lru_mixer(proj, blk_xl, blk_gl, bsz, seq, conv_w, conv_b, w_a, b_a, w_x, b_x, lam, cast_w, cast_layer,
               *, tt=512):
    nb, blk, _ = w_a.shape
    width = nb * blk
    tt = min(tt, seq)
    assert seq % tt == 0 and tt % V7X_SUBLANES == 0
    nt = seq // tt
    cast_in, cast_out, cast_shape = _cast_slab(cast_w, cast_layer, bsz, nt)
    per = V7X_LANES // blk
    eye = jnp.eye(per, dtype=F32)
    bd = lambda w: (w.astype(F32).reshape(nb // per, per, blk, 1, blk) * eye[None, :, None, :, None]
                    ).reshape(nb // per, V7X_LANES, V7X_LANES).astype(BF16)
    f = lambda a: a.astype(F32).reshape(1, width)
    const = lambda shape: pl.BlockSpec(shape, lambda b, t: (0,) * len(shape))
    return pl.pallas_call(
        _with_weight_cast(_lru_kernel, 9),
        grid=(bsz, nt),
        in_specs=[pl.BlockSpec((tt, width), lambda b, t: (b * nt + t, blk_xl)),
                  pl.BlockSpec((tt, width), lambda b, t: (b * nt + t, blk_gl)),
                  const((CONV_K, width)), const((1, width)),
                  const((nb // per, V7X_LANES, V7X_LANES)), const((1, width)),
                  const((nb // per, V7X_LANES, V7X_LANES)), const((1, width)), const((1, width)), cast_in],
        out_specs=[pl.BlockSpec((tt, width), lambda b, t: (b * nt + t, 0)), cast_out],
        out_shape=[jax.ShapeDtypeStruct((bsz * seq, width), BF16), cast_shape],
        scratch_shapes=[pltpu.VMEM((V7X_SUBLANES, width), F32), pltpu.VMEM((V7X_SUBLANES, width), F32)],
        compiler_params=_cparams("parallel", "arbitrary"),
        name="lru_mixer",
    )(proj, proj, conv_w.astype(F32), f(conv_b), bd(w_a), f(b_a), bd(w_x), f(b_x), f(lam), cast_w)


def _even_mixer(h, hn, bsz, seq, layer, group, mlp_w1, mlp_w2, norm_g, norm_out, in_proj, out_proj,
                lam_re, lam_im, log_step, b_re, b_im, c_re, c_im,
                s5_d, glu_w, glu_b, conv_w, conv_b, dt_bias, a_log, ssd_d, ssd_norm):
    s5w = lam_re.shape[0] * S5_GROUP
    ssdw = SSD_HEADS * SSD_HEAD_DIM
    gstate2 = 2 * SSD_GROUPS * SSD_STATE
    off_bc, off_u, off_dt = 2 * ssdw, 2 * ssdw + gstate2, 2 * ssdw + gstate2 + s5w
    n_dt = in_proj.shape[2] - (s5w + 2 * ssdw + gstate2)
    segments = [(s5w, ssdw, 0), (s5w + ssdw, ssdw + gstate2, ssdw), (0, s5w, off_u),
                (s5w + 2 * ssdw + gstate2, n_dt, off_dt)]
    w_all = _regroup_columns(in_proj, group, segments, -(-(off_dt + n_dt) // 512) * 512)
    proj = _in_proj(h, hn, norm_g, w_all)
    y_a, w1 = _s5_mixer(proj, off_u // s5w, bsz, seq, lam_re, lam_im, log_step, b_re, b_im, c_re, c_im,
                        s5_d, glu_w, glu_b, mlp_w1, layer)
    y_b, w2 = _ssd_mixer(proj, 0, 1, off_bc // gstate2, off_dt // V7X_LANES, bsz, seq,
                         conv_w, conv_b, dt_bias, a_log, ssd_d, ssd_norm, mlp_w2, layer)
    return [*_out_proj_residual(y_a, y_b, out_proj.astype(BF16), h, norm_out), w1, w2]


def _odd_mixer(h, hn, bsz, seq, layer, group, mlp_w1, mlp_w2, norm_g, norm_out, in_proj, out_proj,
               mu, w0, w_up, a0, a_up, g_up, k_k, k_a, r_k,
               ln_g, ln_b, conv_w, conv_b, w_a, b_a, w_x, b_x, lam):
    rw = w0.shape[0]
    nl = w_up.shape[0]
    ngl = g_up.shape[0]
    lw = lam.shape[0] * lam.shape[1]
    nlp = V7X_LANES
    assert nl <= nlp and ngl == 2 * nlp
    off_xl, off_gl, off_lo = 3 * rw, 3 * rw + lw, 3 * rw + 2 * lw
    src_lo = 3 * rw + 2 * nl + ngl
    segments = [(0, 3 * rw, 0), (src_lo, lw, off_xl), (src_lo + lw, lw, off_gl),
                (3 * rw, nl, off_lo), (3 * rw + nl, nl, off_lo + nlp), (3 * rw + 2 * nl, ngl, off_lo + 2 * nlp)]
    w_all = _regroup_columns(in_proj, group, segments, off_lo + 4 * nlp)
    mu = mu.astype(F32)
    mu_rkv = mu[:3 * rw].reshape(1, -1)
    padv = lambda v: jnp.pad(v, (0, nlp - v.shape[0]))
    mu_lo = jnp.concatenate([padv(mu[3 * rw:3 * rw + nl]), padv(mu[3 * rw + nl:3 * rw + 2 * nl]),
                             mu[3 * rw + 2 * nl:]]).reshape(1, -1)
    padr = lambda w: jnp.pad(w, ((0, nlp - w.shape[0]), (0, 0)))
    proj = _in_proj(h, hn, norm_g, w_all)
    y_c, w1 = _rwkv_mixer(proj, 0, off_lo // (4 * nlp), bsz, seq, mu_rkv, mu_lo, w0, padr(w_up), a0, padr(a_up),
                          g_up, k_k, k_a, r_k.reshape(-1), ln_g, ln_b, mlp_w1, layer)
    y_d, w2 = _lru_mixer(proj, off_xl // lw, off_gl // lw, bsz, seq, conv_w, conv_b, w_a, b_a, w_x, b_x, lam,
                         mlp_w2, layer)
    return [*_out_proj_residual(y_c, y_d, out_proj.astype(BF16), h, norm_out), w1, w2]


def kernel(x, p, norm_mix, norm_ffn, norm_pl, mlp_w1, mlp_w2, pl_proj, pl_gate, e_in_proj, e_out_proj, s5_lam_re, s5_lam_im, s5_log_step, s5_b_re, s5_b_im, s5_c_re, s5_c_im, s5_d, s5_glu_w, s5_glu_b, ssd_conv_w, ssd_conv_b, ssd_dt_bias, ssd_a_log, ssd_d, ssd_norm, o_in_proj, o_out_proj, rwkv_mu, rwkv_w0, rwkv_w_up, rwkv_a0, rwkv_a_up, rwkv_g_up, rwkv_k_k, rwkv_k_a, rwkv_r_k, rwkv_ln_g, rwkv_ln_b, lru_conv_w, lru_conv_b, lru_w_a, lru_b_a, lru_w_x, lru_b_x, lru_lam, norm_final):
    bsz, seq, d = x.shape
    depth = p.shape[0]
    h = x.astype(F32).reshape(bsz * seq, d)
    hn = None
    wg_all, wp_all = pl_gate.astype(BF16), pl_proj.astype(BF16)
    p_all = p.reshape(depth, bsz * seq, -1)
    mlp_w1, mlp_w2 = mlp_w1.astype(F32), mlp_w2.astype(F32)
    for i in range(depth):
        j = i // 2
        common = (h, hn, bsz, seq, i, j, mlp_w1, mlp_w2, norm_mix[i], norm_ffn[i])
        if i % 2 == 0:
            h, hn, w1, w2 = _even_mixer(*common, e_in_proj, e_out_proj[j], s5_lam_re[j], s5_lam_im[j],
                                        s5_log_step[j], s5_b_re[j], s5_b_im[j], s5_c_re[j], s5_c_im[j], s5_d[j],
                                        s5_glu_w[j], s5_glu_b[j], ssd_conv_w[j], ssd_conv_b[j], ssd_dt_bias[j],
                                        ssd_a_log[j], ssd_d[j], ssd_norm[j])
        else:
            h, hn, w1, w2 = _odd_mixer(*common, o_in_proj, o_out_proj[j], rwkv_mu[j], rwkv_w0[j],
                                       rwkv_w_up[j], rwkv_a0[j], rwkv_a_up[j], rwkv_g_up[j], rwkv_k_k[j], rwkv_k_a[j],
                                       rwkv_r_k[j], rwkv_ln_g[j], rwkv_ln_b[j], lru_conv_w[j], lru_conv_b[j],
                                       lru_w_a[j], lru_b_a[j], lru_w_x[j], lru_b_x[j], lru_lam[j])
        h = _mlp_residual(h, hn, w1, w2)
        gate_args = (h, norm_pl[i], wg_all, p_all, i, wp_all)
        if i + 1 < depth:
            h, hn = _gated_embed(*gate_args, norm_mix[i + 1], final=False)
        else:
            h = _gated_embed(*gate_args, norm_final, final=True)
    return h.reshape(bsz, seq, d)
```

```python
import functools

import jax
import jax.numpy as jnp
from jax import lax
from jax.experimental import pallas as pl
from jax.experimental.pallas import tpu as pltpu

F32 = jnp.float32
BF16 = jnp.bfloat16

V7X_LANES = 128
V7X_SUBLANES = 8
V7X_BF16_SUBLANES = 16
V7X_VMEM_BYTES = 64 * 1024 * 1024
VMEM_LIMIT_BYTES = V7X_VMEM_BYTES - 8 * 1024 * 1024

NORM_EPS = 1e-6
S5_GROUP = 16
S5_STATE = 64
SSD_HEAD_DIM = 64
SSD_HEADS = 24
SSD_GROUPS = 4
SSD_STATE = 128
SSD_CHUNK = 128
CONV_K = 4
RWKV_HEAD_DIM = 64
RWKV_GN_EPS = 64e-5
RWKV_CHUNK = 64
RWKV_INV_BLOCK = 16
LRU_BLOCK = 64
LRU_C = 8.0


def _cparams(*sem):
    return pltpu.CompilerParams(dimension_semantics=sem, vmem_limit_bytes=VMEM_LIMIT_BYTES)


def _bdot(a, b):
    return jnp.dot(a.astype(BF16), b.astype(BF16), preferred_element_type=F32)


def _bdot_nt(a, b):
    return lax.dot_general(a.astype(BF16), b.astype(BF16), (((1,), (1,)), ((), ())),
                           preferred_element_type=F32)


def _bdot_tn(a, b):
    return lax.dot_general(a.astype(BF16), b.astype(BF16), (((0,), (0,)), ((), ())),
                           preferred_element_type=F32)


def _split3(x):
    x1 = x.astype(BF16)
    r1 = x - x1.astype(F32)
    x2 = r1.astype(BF16)
    x3 = (r1 - x2.astype(F32)).astype(BF16)
    return x1, x2, x3


def _dot_lhs01(m01, x):
    m = m01.astype(BF16)
    x1, x2, x3 = _split3(x)
    d = lambda v: jnp.dot(m, v, preferred_element_type=F32)
    return d(x1) + d(x2) + d(x3)


def _dot_rhs01(x, m01):
    m = m01.astype(BF16)
    x1, x2, x3 = _split3(x)
    d = lambda v: jnp.dot(v, m, preferred_element_type=F32)
    return d(x1) + d(x2) + d(x3)


def _softplus(x):
    return jnp.maximum(x, 0.0) + jnp.log1p(jnp.exp(-jnp.abs(x)))


def _sigmoid(x):
    return jax.nn.sigmoid(x)


def _gelu(x):
    return jax.nn.gelu(x, approximate=True)


def _silu(x):
    return x * _sigmoid(x)


def _row_iota(shape):
    return lax.broadcasted_iota(jnp.int32, shape, 0)


def _delayed(x, tail_ref, delays):
    rows, c = x.shape
    nt = rows // V7X_SUBLANES
    x3 = x.reshape(nt, V7X_SUBLANES, c)
    tail = tail_ref[...].reshape(1, V7X_SUBLANES, c)
    sub = lax.broadcasted_iota(jnp.int32, (nt, V7X_SUBLANES, c), 1)
    out = []
    for d in delays:
        cur = pltpu.roll(x3, d, 1)
        prev = jnp.concatenate([pltpu.roll(tail, d, 1), cur[:nt - 1]], axis=0)
        out.append(jnp.where(sub >= d, cur, prev).reshape(rows, c))
    tail_ref[...] = x[rows - V7X_SUBLANES:, :]
    return out


def _causal_conv(x, ext_ref, w_ref, b_ref):
    acc = b_ref[...] + w_ref[CONV_K - 1:CONV_K, :] * x
    for k, xd in enumerate(_delayed(x, ext_ref, range(CONV_K - 1, 0, -1))):
        acc = acc + w_ref[k:k + 1, :] * xd
    return acc


def _with_weight_cast(body, n_in):
    def kernel(*refs):
        ins, wi_ref, o_ref, wo_ref, scratch = refs[:n_in], refs[n_in], refs[n_in + 1], refs[n_in + 2], refs[n_in + 3:]
        wo_ref[...] = wi_ref[...].astype(BF16)
        body(*ins, o_ref, *scratch)
    return kernel


def _cast_slab(w_all, layer, bsz, nt):
    _, rows, cols = w_all.shape
    blk = rows // (bsz * nt)
    assert blk * bsz * nt == rows and blk % V7X_BF16_SUBLANES == 0, (rows, bsz, nt)
    return (pl.BlockSpec((None, blk, cols), lambda b, t: (layer, b * nt + t, 0)),
            pl.BlockSpec((blk, cols), lambda b, t: (b * nt + t, 0)),
            jax.ShapeDtypeStruct((rows, cols), BF16))


def _regroup_kernel(wt_ref, o_ref, *, segments):
    o_ref[...] = jnp.zeros_like(o_ref)
    kb = o_ref.shape[0]
    lane = lax.broadcasted_iota(jnp.int32, (kb, V7X_LANES), 1)
    for src, width, dst in segments:
        full = width // V7X_LANES * V7X_LANES
        if full:
            o_ref[:, dst:dst + full] = wt_ref[src:src + full, :].T.astype(BF16)
        rem = width - full
        if rem:
            end = src + width
            win = pltpu.roll(wt_ref[end - V7X_LANES:end, :].T, rem, 1)
            o_ref[:, dst + full:dst + full + V7X_LANES] = jnp.where(lane < rem, win, 0.0).astype(BF16)


def _regroup_columns(w_all, index, segments, n_out, *, rows_blk=256):
    _, k, n_in = w_all.shape
    rows_blk = min(rows_blk, k)
    assert k % rows_blk == 0 and rows_blk % V7X_LANES == 0
    for src, width, dst in segments:
        assert dst % V7X_LANES == 0 and src % V7X_SUBLANES == 0 and width % V7X_SUBLANES == 0
        assert width % V7X_LANES == 0 or src + width >= V7X_LANES
    return pl.pallas_call(
        functools.partial(_regroup_kernel, segments=tuple(segments)),
        grid=(k // rows_blk,),
        in_specs=[pl.BlockSpec((None, n_in, rows_blk), lambda i: (index, 0, i))],
        out_specs=pl.BlockSpec((rows_blk, n_out), lambda i: (i, 0)),
        out_shape=jax.ShapeDtypeStruct((k, n_out), BF16),
        compiler_params=_cparams("parallel"),
        name="regroup_columns",
    )(jnp.swapaxes(w_all.astype(F32), 1, 2))


def _rmsnorm_rows(h, g):
    ms = jnp.mean(h * h, axis=-1, keepdims=True)
    return h * lax.rsqrt(ms + NORM_EPS) * g


def _norm_mm_kernel(h_ref, g_ref, w_ref, o_ref):
    hn = _rmsnorm_rows(h_ref[...], g_ref[...]).astype(BF16)
    o_ref[...] = jnp.dot(hn, w_ref[...], preferred_element_type=F32)


def _col_tile(n, target):
    return max(t for t in range(V7X_LANES, max(target, V7X_LANES) + 1, V7X_LANES) if n % t == 0)


def _norm_matmul(h, g, w, *, tm, tn):
    m, k = h.shape
    n = w.shape[1]
    tm, tn = min(tm, m), _col_tile(n, tn)
    assert m % tm == 0 and n % tn == 0, (m, n, tm, tn)
    return pl.pallas_call(
        _norm_mm_kernel,
        grid=(m // tm, n // tn),
        in_specs=[pl.BlockSpec((tm, k), lambda i, j: (i, 0)),
                  pl.BlockSpec((1, k), lambda i, j: (0, 0)),
                  pl.BlockSpec((k, tn), lambda i, j: (0, j))],
        out_specs=pl.BlockSpec((tm, tn), lambda i, j: (i, j)),
        out_shape=jax.ShapeDtypeStruct((m, n), F32),
        compiler_params=_cparams("parallel", "parallel"),
        name="norm_matmul",
    )(h, g.reshape(1, k), w)


def _mm_kernel(a_ref, w_ref, o_ref):
    o_ref[...] = jnp.dot(a_ref[...], w_ref[...], preferred_element_type=F32)


def _in_proj(h, hn, g, w, *, tm=1024, tn=1536):
    if hn is None:
        return _norm_matmul(h, g, w, tm=tm, tn=tn)
    m, k = hn.shape
    n = w.shape[1]
    tm, tn = min(tm, m), _col_tile(n, tn)
    assert m % tm == 0
    return pl.pallas_call(
        _mm_kernel,
        grid=(m // tm, n // tn),
        in_specs=[pl.BlockSpec((tm, k), lambda i, j: (i, 0)), pl.BlockSpec((k, tn), lambda i, j: (0, j))],
        out_specs=pl.BlockSpec((tm, tn), lambda i, j: (i, j)),
        out_shape=jax.ShapeDtypeStruct((m, n), F32),
        compiler_params=_cparams("parallel", "parallel"),
        name="in_proj",
    )(hn, w)


def _out_proj_kernel(a1_ref, a2_ref, w_ref, r_ref, g_ref, o_ref, on_ref):
    k1 = a1_ref.shape[1]
    ho = (r_ref[...] + jnp.dot(a1_ref[...], w_ref[:k1, :], preferred_element_type=F32)
          + jnp.dot(a2_ref[...], w_ref[k1:, :], preferred_element_type=F32))
    o_ref[...] = ho
    on_ref[...] = _rmsnorm_rows(ho, g_ref[...]).astype(BF16)


def _out_proj_residual(a1, a2, w, res, g_next, *, tm=512):
    m, k1 = a1.shape
    k2 = a2.shape[1]
    n = w.shape[1]
    tm = min(tm, m)
    assert m % tm == 0
    rows = lambda width: pl.BlockSpec((tm, width), lambda i: (i, 0))
    const = lambda shape: pl.BlockSpec(shape, lambda i: (0, 0))
    return pl.pallas_call(
        _out_proj_kernel,
        grid=(m // tm,),
        in_specs=[rows(k1), rows(k2), const((k1 + k2, n)), rows(n), const((1, n))],
        out_specs=[rows(n), rows(n)],
        out_shape=[jax.ShapeDtypeStruct((m, n), F32), jax.ShapeDtypeStruct((m, n), BF16)],
        compiler_params=_cparams("parallel"),
        name="out_proj",
    )(a1, a2, w, res, g_next.reshape(1, n))


def _mlp_kernel(h_ref, hn_ref, w1_ref, w2_ref, o_ref, acc_ref, *, nf):
    j = pl.program_id(1)

    @pl.when(j == 0)
    def _():
        acc_ref[...] = jnp.zeros_like(acc_ref)

    mid = jnp.dot(hn_ref[...], w1_ref[...], preferred_element_type=F32)
    mid = jnp.square(jnp.maximum(mid, 0.0)).astype(BF16)
    acc_ref[...] += jnp.dot(mid, w2_ref[...], preferred_element_type=F32)

    @pl.when(j == nf - 1)
    def _():
        o_ref[...] = h_ref[...] + acc_ref[...]


def _mlp_residual(h, hn, w1, w2, *, tm=512, tf=1024):
    m, d = h.shape
    f = w1.shape[1]
    tm, tf = min(tm, m), min(tf, f)
    assert m % tm == 0 and f % tf == 0
    nf = f // tf
    return pl.pallas_call(
        functools.partial(_mlp_kernel, nf=nf),
        grid=(m // tm, nf),
        in_specs=[pl.BlockSpec((tm, d), lambda i, j: (i, 0)),
                  pl.BlockSpec((tm, d), lambda i, j: (i, 0)),
                  pl.BlockSpec((d, tf), lambda i, j: (0, j)),
                  pl.BlockSpec((tf, d), lambda i, j: (j, 0))],
        out_specs=pl.BlockSpec((tm, d), lambda i, j: (i, 0)),
        out_shape=jax.ShapeDtypeStruct((m, d), F32),
        scratch_shapes=[pltpu.VMEM((tm, d), F32)],
        compiler_params=_cparams("parallel", "arbitrary"),
        name="mlp",
    )(h, hn, w1, w2)


def _gate_kernel(h_ref, g_ref, wg_ref, p_ref, wp_ref, gn_ref, *o_refs, final):
    h = h_ref[...]
    hn = _rmsnorm_rows(h, g_ref[...]).astype(BF16)
    gate = _sigmoid(jnp.dot(hn, wg_ref[...], preferred_element_type=F32))
    pp = jnp.dot(p_ref[...].astype(BF16), wp_ref[...], preferred_element_type=F32)
    ho = h + gate * pp
    nxt = _rmsnorm_rows(ho, gn_ref[...])
    if final:
        o_refs[0][...] = nxt
    else:
        o_refs[0][...] = ho
        o_refs[1][...] = nxt.astype(BF16)


def _gated_embed(h, g, wg, p, layer, wp, g_next, *, final, tm=512):
    m, k = h.shape
    kp = p.shape[2]
    tm = min(tm, m)
    assert m % tm == 0 and wg.shape[1:] == (k, k)
    const = lambda shape: pl.BlockSpec(shape, lambda i: (0, 0))
    rows = lambda w: pl.BlockSpec((tm, w), lambda i: (i, 0))
    out_specs = rows(k) if final else [rows(k), rows(k)]
    out_shape = (jax.ShapeDtypeStruct((m, k), F32) if final
                 else [jax.ShapeDtypeStruct((m, k), F32), jax.ShapeDtypeStruct((m, k), BF16)])
    return pl.pallas_call(
        functools.partial(_gate_kernel, final=final),
        grid=(m // tm,),
        in_specs=[rows(k), const((1, k)), pl.BlockSpec((None, k, k), lambda i: (layer, 0, 0)),
                  pl.BlockSpec((None, tm, kp), lambda i: (layer, i, 0)),
                  pl.BlockSpec((None, kp, k), lambda i: (layer, 0, 0)), const((1, k))],
        out_specs=out_specs,
        out_shape=out_shape,
        compiler_params=_cparams("parallel"),
        name="gated_embed",
    )(h, g.reshape(1, k), wg, p, wp, g_next.reshape(1, k))


def _s5_prep_kernel(lr_ref, li_ref, st_ref, btr_ref, bti_ref, b8_ref, a8_re_ref, a8_im_ref):
    lr, li, st = lr_ref[...], li_ref[...], jnp.exp(st_ref[...])
    lrs, lis = lr * st, li * st

    def apow(n):
        mag = jnp.exp(n * lrs)
        return mag * jnp.cos(n * lis), mag * jnp.sin(n * lis)

    mag = jnp.exp(lrs)
    abar_re, abar_im = mag * jnp.cos(lis), mag * jnp.sin(lis)
    den = lr * lr + li * li
    nr = abar_re - 1.0
    coef_re = (nr * lr + abar_im * li) / den
    coef_im = (abar_im * lr - nr * li) / den
    btr, bti = btr_ref[...], bti_ref[...]
    bbar_re = coef_re * btr - coef_im * bti
    bbar_im = coef_re * bti + coef_im * btr
    nblk, taps_ch, lanes2 = b8_ref.shape
    lane_blk = lanes2 // 2
    ch_blk = taps_ch // V7X_SUBLANES
    rg = lax.broadcasted_iota(jnp.int32, (ch_blk, lane_blk), 0) // S5_GROUP
    cg = lax.broadcasted_iota(jnp.int32, (ch_blk, lane_blk), 1) // S5_STATE
    same = rg == cg
    for s in range(V7X_SUBLANES):
        if s == 0:
            br, bi = bbar_re, bbar_im
        else:
            pr, pi = apow(float(s))
            br, bi = pr * bbar_re - pi * bbar_im, pr * bbar_im + pi * bbar_re
        for j in range(nblk):
            ls = slice(j * lane_blk, (j + 1) * lane_blk)
            rs = slice(s * ch_blk, (s + 1) * ch_blk)
            tile = lambda m: jnp.where(same, jnp.concatenate([m[:, ls]] * (ch_blk // S5_GROUP), axis=0), 0.0)
            b8_ref[j, rs, :lane_blk] = tile(br).astype(BF16)
            b8_ref[j, rs, lane_blk:] = tile(bi).astype(BF16)
    a8r, a8i = apow(float(V7X_SUBLANES))
    a8_re_ref[...] = jnp.broadcast_to(a8r, a8_re_ref.shape)
    a8_im_ref[...] = jnp.broadcast_to(a8i, a8_im_ref.shape)


def _s5_kernel(u_ref, b8_ref, cre_ref, cim_ref, a8r_ref, a8i_ref, d_ref, gw_ref, gb_ref, o_ref,
               ue_ref, car_re, car_im):
    @pl.when(pl.program_id(1) == 0)
    def _():
        ue_ref[...] = jnp.zeros_like(ue_ref)
        car_re[...] = jnp.zeros_like(car_re)
        car_im[...] = jnp.zeros_like(car_im)

    u = u_ref[...]
    tt = u.shape[0]
    nblk, taps_ch, lanes2 = b8_ref.shape
    lane_blk = lanes2 // 2
    ch_blk = taps_ch // V7X_SUBLANES
    taps = [u] + _delayed(u, ue_ref, range(1, V7X_SUBLANES))
    ys = []
    for j in range(nblk):
        ls = slice(j * lane_blk, (j + 1) * lane_blk)
        cs = slice(j * ch_blk, (j + 1) * ch_blk)
        u8 = jnp.concatenate([t[:, cs].astype(BF16) for t in taps], axis=1)
        z = jnp.dot(u8, b8_ref[j], preferred_element_type=F32)
        pr, pi = a8r_ref[:, ls], a8i_ref[:, ls]
        cr, ci = car_re[:, ls], car_im[:, ls]
        xr, xi = [], []
        for i in range(tt // V7X_SUBLANES):
            rs = slice(i * V7X_SUBLANES, (i + 1) * V7X_SUBLANES)
            cr, ci = (z[rs, :lane_blk] + (pr * cr - pi * ci), z[rs, lane_blk:] + (pr * ci + pi * cr))
            xr.append(cr)
            xi.append(ci)
        car_re[:, ls] = cr
        car_im[:, ls] = ci
        ys.append(_bdot(jnp.concatenate(xr, axis=0), cre_ref[j]) - _bdot(jnp.concatenate(xi, axis=0), cim_ref[j]))
    y = jnp.concatenate(ys, axis=1) + d_ref[...] * u
    act = _gelu(y)
    o_ref[...] = (act * _sigmoid(_bdot(act, gw_ref[...]) + gb_ref[...])).astype(o_ref.dtype)


def _s5_mixer(proj, col_blk, bsz, seq, lam_re, lam_im, log_step, b_re, b_im, c_re, c_im,
              d_skip, glu_w, glu_b, cast_w, cast_layer, *, tt=512, lane_blk=128):
    ng, ns = lam_re.shape
    width = ng * S5_GROUP
    nstate = ng * ns
    tt = min(tt, seq)
    assert seq % tt == 0 and tt % V7X_SUBLANES == 0 and nstate % lane_blk == 0 and lane_blk % ns == 0
    nblk = nstate // lane_blk
    gpb = lane_blk // ns
    ch_blk = gpb * S5_GROUP
    flat = lambda a: a.astype(F32).reshape(1, nstate)
    step = jnp.broadcast_to(log_step.astype(F32)[:, None], (ng, ns))
    bt = lambda a: a.astype(F32).transpose(2, 0, 1).reshape(S5_GROUP, nstate)
    vm = pl.BlockSpec(memory_space=pltpu.VMEM)
    b8, a8_re, a8_im = pl.pallas_call(
        _s5_prep_kernel,
        in_specs=[vm] * 5,
        out_specs=[vm] * 3,
        out_shape=[jax.ShapeDtypeStruct((nblk, V7X_SUBLANES * ch_blk, 2 * lane_blk), BF16)]
        + [jax.ShapeDtypeStruct((V7X_SUBLANES, nstate), F32)] * 2,
        compiler_params=pltpu.CompilerParams(vmem_limit_bytes=VMEM_LIMIT_BYTES),
        name="s5_prep",
    )(flat(lam_re), flat(lam_im), flat(step), bt(b_re), bt(b_im))
    eye = jnp.eye(gpb, dtype=F32)
    cbd = lambda c: (c.astype(F32).reshape(nblk, gpb, S5_GROUP, ns).transpose(0, 1, 3, 2)[:, :, :, None, :]
                     * eye[None, :, None, :, None]).reshape(nblk, lane_blk, ch_blk).astype(BF16)
    nt = seq // tt
    cast_in, cast_out, cast_shape = _cast_slab(cast_w, cast_layer, bsz, nt)
    const = lambda shape: pl.BlockSpec(shape, lambda b, t: (0,) * len(shape))
    return pl.pallas_call(
        _with_weight_cast(_s5_kernel, 9),
        grid=(bsz, nt),
        in_specs=[pl.BlockSpec((tt, width), lambda b, t: (b * nt + t, col_blk)),
                  const((nblk, V7X_SUBLANES * ch_blk, 2 * lane_blk)),
                  const((nblk, lane_blk, ch_blk)), const((nblk, lane_blk, ch_blk)),
                  const((V7X_SUBLANES, nstate)), const((V7X_SUBLANES, nstate)),
                  const((1, width)), const((width, width)), const((1, width)), cast_in],
        out_specs=[pl.BlockSpec((tt, width), lambda b, t: (b * nt + t, 0)), cast_out],
        out_shape=[jax.ShapeDtypeStruct((bsz * seq, width), BF16), cast_shape],
        scratch_shapes=[pltpu.VMEM((V7X_SUBLANES, width), F32)]
        + [pltpu.VMEM((V7X_SUBLANES, nstate), F32)] * 2,
        compiler_params=_cparams("parallel", "arbitrary"),
        name="s5_mixer",
    )(proj, b8, cbd(c_re), cbd(c_im), a8_re, a8_im,
      d_skip.astype(F32).reshape(1, width), glu_w.astype(BF16), glu_b.astype(F32).reshape(1, width), cast_w)


def _ssd_kernel(z_ref, xs_ref, bc_ref, dt_ref, cwx_ref, cbx_ref, cwb_ref, cbb_ref, e64_ref,
                dtb_ref, alog_ref, dskx_ref, ng_ref, o_ref,
                tailx, tailb, st_ref):
    @pl.when(pl.program_id(1) == 0)
    def _():
        tailx[...] = jnp.zeros_like(tailx)
        tailb[...] = jnp.zeros_like(tailb)
        st_ref[...] = jnp.zeros_like(st_ref)

    L = SSD_CHUNK
    tt = xs_ref.shape[0]
    width = xs_ref.shape[1]
    gstate = SSD_GROUPS * SSD_STATE
    xs = _silu(_causal_conv(xs_ref[...], tailx, cwx_ref, cbx_ref))
    bc = _silu(_causal_conv(bc_ref[...], tailb, cwb_ref, cbb_ref))
    dt = _softplus(dt_ref[...] + dtb_ref[...])
    da = dt * (-jnp.exp(alog_ref[...]))
    e64 = e64_ref[...]
    xdt = xs * _dot_rhs01(dt, e64)
    ri = lax.broadcasted_iota(jnp.int32, (L, L), 0)
    ci = lax.broadcasted_iota(jnp.int32, (L, L), 1)
    tril = ri >= ci
    tri01 = tril.astype(BF16)
    lane_lo = lax.broadcasted_iota(jnp.int32, (1, V7X_LANES), 1) < SSD_HEAD_DIM
    pairs_per_group = SSD_HEADS // SSD_GROUPS // 2
    ys = []
    for c in range(tt // L):
        rs = slice(c * L, (c + 1) * L)
        acum = _dot_lhs01(tri01, da[rs, :])
        acum_t = acum.T
        acx = _dot_rhs01(acum, e64)
        ycols = []
        for g in range(SSD_GROUPS):
            bg = bc[rs, g * SSD_STATE:(g + 1) * SSD_STATE]
            cg = bc[rs, gstate + g * SSD_STATE:gstate + (g + 1) * SSD_STATE]
            scores = _bdot_nt(cg, bg).astype(BF16)
            bg_t = bg.T
            for pr in range(pairs_per_group):
                q = g * pairs_per_group + pr
                ls = slice(q * V7X_LANES, (q + 1) * V7X_LANES)
                xp = xdt[rs, ls]
                acp = acx[:, ls]
                yd = []
                for h in (2 * q, 2 * q + 1):
                    seg = acum[:, h:h + 1] - acum_t[h:h + 1, :]
                    decay = jnp.exp(jnp.where(tril, seg, -jnp.inf).astype(BF16))
                    yd.append(jnp.dot(scores * decay, xp.astype(BF16), preferred_element_type=F32))
                y = jnp.where(lane_lo, yd[0], yd[1])
                prev_t = st_ref[q]
                y = y + _bdot(cg, prev_t) * jnp.exp(acp)
                last = acp[L - 1:L, :]
                st_ref[q] = prev_t * jnp.exp(last) + _bdot(bg_t, xp * jnp.exp(last - acp))
                ycols.append(y)
        ys.append(jnp.concatenate(ycols, axis=1))
    y = jnp.concatenate(ys, axis=0) if len(ys) > 1 else ys[0]
    y = (y + xs * dskx_ref[...]) * _silu(z_ref[...])
    gw = width // SSD_GROUPS
    outs = []
    for g in range(SSD_GROUPS):
        yg = y[:, g * gw:(g + 1) * gw]
        outs.append(yg * lax.rsqrt(jnp.mean(yg * yg, axis=-1, keepdims=True) + NORM_EPS))
    o_ref[...] = (jnp.concatenate(outs, axis=1) * ng_ref[...]).astype(o_ref.dtype)


def _ssd_mixer(proj, blk_z, blk_xs, blk_bc, blk_dt, bsz, seq, conv_w, conv_b, dt_bias, a_log,
               d_skip, norm_g, cast_w, cast_layer, *, tt=512):
    width = SSD_HEADS * SSD_HEAD_DIM
    gstate2 = 2 * SSD_GROUPS * SSD_STATE
    tt = min(tt, seq)
    assert seq % tt == 0 and tt % SSD_CHUNK == 0
    nt = seq // tt
    cast_in, cast_out, cast_shape = _cast_slab(cast_w, cast_layer, bsz, nt)
    f = lambda a: a.astype(F32)
    pad_h = lambda a: jnp.pad(f(a), (0, V7X_LANES - SSD_HEADS)).reshape(1, V7X_LANES)
    exp_h = lambda a: jnp.repeat(f(a), SSD_HEAD_DIM).reshape(1, width)
    e64 = (jnp.arange(V7X_LANES)[:, None] == (jnp.arange(width)[None, :] // SSD_HEAD_DIM)).astype(BF16)
    const = lambda shape: pl.BlockSpec(shape, lambda b, t: (0,) * len(shape))
    rowblk = lambda w, cb: pl.BlockSpec((tt, w), lambda b, t: (b * nt + t, cb))
    return pl.pallas_call(
        _with_weight_cast(_ssd_kernel, 13),
        grid=(bsz, nt),
        in_specs=[rowblk(width, blk_z), rowblk(width, blk_xs), rowblk(gstate2, blk_bc),
                  rowblk(V7X_LANES, blk_dt),
                  const((CONV_K, width)), const((1, width)), const((CONV_K, gstate2)), const((1, gstate2)),
                  const((V7X_LANES, width)),
                  const((1, V7X_LANES)), const((1, V7X_LANES)),
                  const((1, width)), const((1, width)), cast_in],
        out_specs=[pl.BlockSpec((tt, width), lambda b, t: (b * nt + t, 0)), cast_out],
        out_shape=[jax.ShapeDtypeStruct((bsz * seq, width), BF16), cast_shape],
        scratch_shapes=[pltpu.VMEM((V7X_SUBLANES, width), F32),
                        pltpu.VMEM((V7X_SUBLANES, gstate2), F32),
                        pltpu.VMEM((SSD_HEADS // 2, SSD_STATE, V7X_LANES), F32)],
        compiler_params=_cparams("parallel", "arbitrary"),
        name="ssd_mixer",
    )(proj, proj, proj, proj,
      f(conv_w[:, :width]), f(conv_b[:width]).reshape(1, width),
      f(conv_w[:, width:]), f(conv_b[width:]).reshape(1, gstate2),
      e64, pad_h(dt_bias), pad_h(a_log), exp_h(d_skip),
      f(norm_g).reshape(1, width), cast_w)


def _stack_heads(x, lane_lo):
    return jnp.concatenate([jnp.where(lane_lo, x, 0.0), jnp.where(lane_lo, 0.0, x)], axis=0)


def _unit_lower_inverses(mats, same_blk, eye):
    size = eye.shape[0]
    side = lambda x, y: _bdot(x, jnp.concatenate([x, y], axis=1))
    ad = [jnp.where(same_blk, a, 0.0) for a in mats]
    ao = [a - d for a, d in zip(mats, ad)]
    s1 = [eye + d for d in ad]
    a2 = [_bdot(d, d) for d in ad]
    r = [side(x, s) for x, s in zip(a2, s1)]
    a4, s2 = [x[:, :size] for x in r], [s + x[:, size:] for s, x in zip(s1, r)]
    r = [side(x, s) for x, s in zip(a4, s2)]
    a8, s3 = [x[:, :size] for x in r], [s + x[:, size:] for s, x in zip(s2, r)]
    td = [s + _bdot(x, s) for x, s in zip(a8, s3)]
    n = [_bdot(t, o) for t, o in zip(td, ao)]
    r = [side(x, t) for x, t in zip(n, td)]
    n2, w1 = [x[:, :size] for x in r], [t + x[:, size:] for t, x in zip(td, r)]
    return [w + _bdot(x, w) for x, w in zip(n2, w1)]


def _rwkv_kernel(rkv_ref, lo_ref, mu_rkv_ref, mu_lo_ref, w0_ref, wup_ref, a0_ref, aup_ref, gup_ref,
                 kk_ref, ka_ref, rk_ref, lng_ref, lnb_ref, ones_ref, o_ref,
                 prev_rkv, prev_lo, state_ref):
    @pl.when(pl.program_id(1) == 0)
    def _():
        prev_rkv[...] = jnp.zeros_like(prev_rkv)
        prev_lo[...] = jnp.zeros_like(prev_lo)
        state_ref[...] = jnp.zeros_like(state_ref)

    L = RWKV_CHUNK
    tb = rkv_ref.shape[0]
    width = rkv_ref.shape[1] // 3

    def shift_mix(ref, prev, mu_ref):
        f = ref[...]
        fs = jnp.where(_row_iota(f.shape) == 0, prev[0:1, :], pltpu.roll(f, 1, 0))
        prev[0:1, :] = f[tb - 1:tb, :]
        return f + (fs - f) * mu_ref[...]

    rkv = shift_mix(rkv_ref, prev_rkv, mu_rkv_ref)
    lo = shift_mix(lo_ref, prev_lo, mu_lo_ref)
    r, k, v = rkv[:, :width], rkv[:, width:2 * width], rkv[:, 2 * width:]
    nl = wup_ref.shape[0]
    wl, al, gl = lo[:, :nl], lo[:, nl:2 * nl], lo[:, 2 * nl:]
    w = -_softplus(-(w0_ref[...] + _bdot(jnp.tanh(wl), wup_ref[...]))) - 0.5
    logw = -jnp.exp(w)
    a_sig = _sigmoid(a0_ref[...] + _bdot(al, aup_ref[...]))
    g = _bdot(_sigmoid(gl), gup_ref[...])
    kk = k * kk_ref[...]
    k = k * (1.0 + (a_sig - 1.0) * ka_ref[...])
    ones_bd = ones_ref[...]

    def head_sum(x):
        return jnp.dot(x.astype(BF16), ones_bd, preferred_element_type=F32)

    P2 = 2 * L
    ri = lax.broadcasted_iota(jnp.int32, (P2, P2), 0)
    ci = lax.broadcasted_iota(jnp.int32, (P2, P2), 1)
    same_head = (ri // L) == (ci // L)
    strict = same_head & (ri > ci)
    incl = same_head & (ri >= ci)
    same_blk = (ri // RWKV_INV_BLOCK) == (ci // RWKV_INV_BLOCK)
    eye = (ri == ci).astype(F32)
    tri01 = (lax.broadcasted_iota(jnp.int32, (L, L), 0) >= lax.broadcasted_iota(jnp.int32, (L, L), 1)).astype(BF16)
    lane_lo = lax.broadcasted_iota(jnp.int32, (1, V7X_LANES), 1) < RWKV_HEAD_DIM

    nq = width // V7X_LANES
    lss = [slice(q * V7X_LANES, (q + 1) * V7X_LANES) for q in range(nq)]
    kks = [kk[:, ls] for ls in lss]
    kkn = [x * lax.rsqrt(jnp.maximum(head_sum(x * x), 1e-24)) for x in kks]
    nc = tb // L
    inst = [(slice(c * L, (c + 1) * L), q, ls) for c in range(nc) for q, ls in enumerate(lss)]
    cums = [_dot_lhs01(tri01, logw[c * L:(c + 1) * L, :]) for c in range(nc)]
    g_in = jnp.concatenate([jnp.exp(cm) for cm in cums], axis=0)
    g_prev = jnp.concatenate([jnp.exp(cm - logw[c * L:(c + 1) * L, :]) for c, cm in enumerate(cums)], axis=0)
    g_inv = jnp.concatenate([jnp.exp(-cm) for cm in cums], axis=0)
    ats = [_stack_heads(-kkn[q][rs] * g_prev[rs, ls], lane_lo) for rs, q, ls in inst]
    rts = [_stack_heads(r[rs, ls] * g_in[rs, ls], lane_lo) for rs, q, ls in inst]
    bts = [_stack_heads(kkn[q][rs] * a_sig[rs, ls] * g_inv[rs, ls], lane_lo) for rs, q, ls in inst]
    kts = [_stack_heads(k[rs, ls] * g_inv[rs, ls], lane_lo) for rs, q, ls in inst]
    vss = [_stack_heads(v[rs, ls], lane_lo).astype(BF16) for rs, q, ls in inst]
    ars = [jnp.concatenate([a, b], axis=0).astype(BF16) for a, b in zip(ats, rts)]
    ps = [_bdot_nt(ar, jnp.concatenate([b, kq], axis=0)) for ar, b, kq in zip(ars, bts, kts)]
    tinv = _unit_lower_inverses([jnp.where(strict, p[:P2, :P2], 0.0) for p in ps], same_blk, eye)
    akv = [_bdot(jnp.where(strict, p[:P2, P2:], 0.0), vs) for p, vs in zip(ps, vss)]
    a_r = [jnp.concatenate([jnp.where(incl, p[P2:, :P2], 0.0), jnp.where(incl, p[P2:, P2:], 0.0)],
                           axis=1).astype(BF16) for p in ps]
    g_last = [g_in[rs, ls][L - 1:L, :] for rs, q, ls in inst]
    bkl = [(jnp.concatenate([b, kq], axis=0) * gl_).astype(BF16) for b, kq, gl_ in zip(bts, kts, g_last)]
    hts = [state_ref[q] for q in range(nq)]
    yrows = []
    for c in range(nc):
        sel = lambda xs: xs[c * nq:(c + 1) * nq]
        arh = [_bdot_nt(ar, ht) for ar, ht in zip(sel(ars), hts)]
        us = [_bdot(t, x[:P2] + y) for t, x, y in zip(sel(tinv), arh, sel(akv))]
        uv = [jnp.concatenate([u.astype(BF16), vs], axis=0) for u, vs in zip(us, sel(vss))]
        hts = [ht * gl_ + _bdot_tn(w, b) for ht, gl_, w, b in zip(hts, sel(g_last), uv, sel(bkl))]
        ys = [x[P2:] + jnp.dot(m, w, preferred_element_type=F32) for x, m, w in zip(arh, sel(a_r), uv)]
        yrows.append(jnp.concatenate([y[:L] + y[L:] for y in ys], axis=1))
    for q in range(nq):
        state_ref[q] = hts[q]
    y = jnp.concatenate(yrows, axis=0) if len(yrows) > 1 else yrows[0]
    inv_n = 1.0 / RWKV_HEAD_DIM
    outs = []
    for q, ls in enumerate(lss):
        yq = y[:, ls]
        mean = head_sum(yq) * inv_n
        yc = yq - mean
        var = head_sum(yc * yc) * inv_n
        yn = yc * lax.rsqrt(var + RWKV_GN_EPS) * lng_ref[:, ls] + lnb_ref[:, ls]
        bonus = head_sum(r[:, ls] * k[:, ls] * rk_ref[:, ls]) * v[:, ls]
        outs.append((yn + bonus) * g[:, ls])
    o_ref[...] = jnp.concatenate(outs, axis=1).astype(o_ref.dtype)


def _rwkv_mixer(proj, blk_rkv, blk_lo, bsz, seq, mu_rkv, mu_lo, w0, w_up, a0, a_up, g_up,
                k_k, k_a, r_k, ln_g, ln_b, cast_w, cast_layer, *, tb=256):
    width = w0.shape[0]
    nl = w_up.shape[0]
    tb = min(tb, seq)
    assert seq % tb == 0 and tb % RWKV_CHUNK == 0
    nt = seq // tb
    cast_in, cast_out, cast_shape = _cast_slab(cast_w, cast_layer, bsz, nt)
    f = lambda a: a.astype(F32).reshape(1, -1)
    lane = jnp.arange(V7X_LANES) // RWKV_HEAD_DIM
    ones_bd = (lane[:, None] == lane[None, :]).astype(BF16)
    const = lambda shape: pl.BlockSpec(shape, lambda b, t: (0,) * len(shape))
    return pl.pallas_call(
        _with_weight_cast(_rwkv_kernel, 15),
        grid=(bsz, nt),
        in_specs=[pl.BlockSpec((tb, 3 * width), lambda b, t: (b * nt + t, blk_rkv)),
                  pl.BlockSpec((tb, 4 * nl), lambda b, t: (b * nt + t, blk_lo)),
                  const((1, 3 * width)), const((1, 4 * nl)),
                  const((1, width)), const((nl, width)), const((1, width)), const((nl, width)),
                  const((2 * nl, width)),
                  const((1, width)), const((1, width)), const((1, width)), const((1, width)),
                  const((1, width)), const((V7X_LANES, V7X_LANES)), cast_in],
        out_specs=[pl.BlockSpec((tb, width), lambda b, t: (b * nt + t, 0)), cast_out],
        out_shape=[jax.ShapeDtypeStruct((bsz * seq, width), BF16), cast_shape],
        scratch_shapes=[pltpu.VMEM((V7X_SUBLANES, 3 * width), F32),
                        pltpu.VMEM((V7X_SUBLANES, 4 * nl), F32),
                        pltpu.VMEM((width // V7X_LANES, V7X_LANES, V7X_LANES), F32)],
        compiler_params=_cparams("parallel", "arbitrary"),
        name="rwkv_mixer",
    )(proj, proj, mu_rkv, mu_lo, f(w0), w_up.astype(BF16), f(a0), a_up.astype(BF16),
      g_up.astype(BF16), f(k_k), f(k_a), f(r_k), f(ln_g), f(ln_b), ones_bd, cast_w)


def _lru_kernel(xl_ref, gl_ref, cw_ref, cb_ref, wa_ref, ba_ref, wx_ref, bx_ref, lam_ref, o_ref,
                tail, carry):
    @pl.when(pl.program_id(1) == 0)
    def _():
        tail[...] = jnp.zeros_like(tail)
        carry[...] = jnp.zeros_like(carry)

    tt, width = xl_ref.shape
    row = _row_iota((tt, width))
    xc = _causal_conv(xl_ref[...], tail, cw_ref, cb_ref)
    gr, gi = [], []
    for j in range(width // V7X_LANES):
        xj = xc[:, j * V7X_LANES:(j + 1) * V7X_LANES]
        gr.append(_bdot(xj, wa_ref[j]))
        gi.append(_bdot(xj, wx_ref[j]))
    gate_r = _sigmoid(jnp.concatenate(gr, axis=1) + ba_ref[...])
    gate_i = _sigmoid(jnp.concatenate(gi, axis=1) + bx_ref[...])
    log_a = -LRU_C * gate_r * _softplus(-lam_ref[...])
    a = jnp.exp(log_a)
    mult = jnp.sqrt(jnp.maximum(-(jnp.tanh(log_a) * (a * a + 1.0)), 0.0))
    mult = jnp.where(row + pl.program_id(1) * tt == 0, 1.0, mult)
    b = xc * gate_i * mult
    nt = tt // V7X_SUBLANES
    a, b = a.reshape(nt, V7X_SUBLANES, width), b.reshape(nt, V7X_SUBLANES, width)
    sub = lax.broadcasted_iota(jnp.int32, a.shape, 1)
    for s in (1, 2, 4):
        keep = sub >= s
        sa, sb = pltpu.roll(a, s, 1), pltpu.roll(b, s, 1)
        b = b + jnp.where(keep, a * sb, 0.0)
        a = jnp.where(keep, a * sa, a)
    a, b = a.reshape(tt, width), b.reshape(tt, width)
    c = carry[0:1, :]
    hs = []
    for i in range(tt // V7X_SUBLANES):
        rs = slice(i * V7X_SUBLANES, (i + 1) * V7X_SUBLANES)
        hi = b[rs, :] + a[rs, :] * c
        hs.append(hi)
        c = hi[V7X_SUBLANES - 1:, :]
    carry[0:1, :] = c
    h = jnp.concatenate(hs, axis=0)
    o_ref[...] = (h * _gelu(gl_ref[...])).astype(o_ref.dtype)


def _lru_mixer(proj, blk_xl, blk_gl, bsz, seq, conv_w, conv_b, w_a, b_a, w_x, b_x, lam, cast_w, cast_layer,
               *, tt=512):
    nb, blk, _ = w_a.shape
    width = nb * blk
    tt = min(tt, seq)
    assert seq % tt == 0 and tt % V7X_SUBLANES == 0
    nt = seq // tt
    cast_in, cast_out, cast_shape = _cast_slab(cast_w, cast_layer, bsz, nt)
    per = V7X_LANES // blk
    eye = jnp.eye(per, dtype=F32)
    bd = lambda w: (w.astype(F32).reshape(nb // per, per, blk, 1, blk) * eye[None, :, None, :, None]
                    ).reshape(nb // per, V7X_LANES, V7X_LANES).astype(BF16)
    f = lambda a: a.astype(F32).reshape(1, width)
    const = lambda shape: pl.BlockSpec(shape, lambda b, t: (0,) * len(shape))
    return pl.pallas_call(
        _with_weight_cast(_lru_kernel, 9),
        grid=(bsz, nt),
        in_specs=[pl.BlockSpec((tt, width), lambda b, t: (b * nt + t, blk_xl)),
                  pl.BlockSpec((tt, width), lambda b, t: (b * nt + t, blk_gl)),
                  const((CONV_K, width)), const((1, width)),
                  const((nb // per, V7X_LANES, V7X_LANES)), const((1, width)),
                  const((nb // per, V7X_LANES, V7X_LANES)), const((1, width)), const((1, width)), cast_in],
        out_specs=[pl.BlockSpec((tt, width), lambda b, t: (b * nt + t, 0)), cast_out],
        out_shape=[jax.ShapeDtypeStruct((bsz * seq, width), BF16), cast_shape],
        scratch_shapes=[pltpu.VMEM((V7X_SUBLANES, width), F32), pltpu.VMEM((V7X_SUBLANES, width), F32)],
        compiler_params=_cparams("parallel", "arbitrary"),
        name="lru_mixer",
    )(proj, proj, conv_w.astype(F32), f(conv_b), bd(w_a), f(b_a), bd(w_x), f(b_x), f(lam), cast_w)


def _even_mixer(h, hn, bsz, seq, layer, group, mlp_w1, mlp_w2, norm_g, norm_out, in_proj, out_proj,
                lam_re, lam_im, log_step, b_re, b_im, c_re, c_im,
                s5_d, glu_w, glu_b, conv_w, conv_b, dt_bias, a_log, ssd_d, ssd_norm):
    s5w = lam_re.shape[0] * S5_GROUP
    ssdw = SSD_HEADS * SSD_HEAD_DIM
    gstate2 = 2 * SSD_GROUPS * SSD_STATE
    off_bc, off_u, off_dt = 2 * ssdw, 2 * ssdw + gstate2, 2 * ssdw + gstate2 + s5w
    n_dt = in_proj.shape[2] - (s5w + 2 * ssdw + gstate2)
    segments = [(s5w, ssdw, 0), (s5w + ssdw, ssdw + gstate2, ssdw), (0, s5w, off_u),
                (s5w + 2 * ssdw + gstate2, n_dt, off_dt)]
    w_all = _regroup_columns(in_proj, group, segments, -(-(off_dt + n_dt) // 512) * 512)
    proj = _in_proj(h, hn, norm_g, w_all)
    y_a, w1 = _s5_mixer(proj, off_u // s5w, bsz, seq, lam_re, lam_im, log_step, b_re, b_im, c_re, c_im,
                        s5_d, glu_w, glu_b, mlp_w1, layer)
    y_b, w2 = _ssd_mixer(proj, 0, 1, off_bc // gstate2, off_dt // V7X_LANES, bsz, seq,
                         conv_w, conv_b, dt_bias, a_log, ssd_d, ssd_norm, mlp_w2, layer)
    return [*_out_proj_residual(y_a, y_b, out_proj.astype(BF16), h, norm_out), w1, w2]


def _odd_mixer(h, hn, bsz, seq, layer, group, mlp_w1, mlp_w2, norm_g, norm_out, in_proj, out_proj,
               mu, w0, w_up, a0, a_up, g_up, k_k, k_a, r_k,
               ln_g, ln_b, conv_w, conv_b, w_a, b_a, w_x, b_x, lam):
    rw = w0.shape[0]
    nl = w_up.shape[0]
    ngl = g_up.shape[0]
    lw = lam.shape[0] * lam.shape[1]
    nlp = V7X_LANES
    assert nl <= nlp and ngl == 2 * nlp
    off_xl, off_gl, off_lo = 3 * rw, 3 * rw + lw, 3 * rw + 2 * lw
    src_lo = 3 * rw + 2 * nl + ngl
    segments = [(0, 3 * rw, 0), (src_lo, lw, off_xl), (src_lo + lw, lw, off_gl),
                (3 * rw, nl, off_lo), (3 * rw + nl, nl, off_lo + nlp), (3 * rw + 2 * nl, ngl, off_lo + 2 * nlp)]
    w_all = _regroup_columns(in_proj, group, segments, off_lo + 4 * nlp)
    mu = mu.astype(F32)
    mu_rkv = mu[:3 * rw].reshape(1, -1)
    padv = lambda v: jnp.pad(v, (0, nlp - v.shape[0]))
    mu_lo = jnp.concatenate([padv(mu[3 * rw:3 * rw + nl]), padv(mu[3 * rw + nl:3 * rw + 2 * nl]),
                             mu[3 * rw + 2 * nl:]]).reshape(1, -1)
    padr = lambda w: jnp.pad(w, ((0, nlp - w.shape[0]), (0, 0)))
    proj = _in_proj(h, hn, norm_g, w_all)
    y_c, w1 = _rwkv_mixer(proj, 0, off_lo // (4 * nlp), bsz, seq, mu_rkv, mu_lo, w0, padr(w_up), a0, padr(a_up),
                          g_up, k_k, k_a, r_k.reshape(-1), ln_g, ln_b, mlp_w1, layer)
    y_d, w2 = _lru_mixer(proj, off_xl // lw, off_gl // lw, bsz, seq, conv_w, conv_b, w_a, b_a, w_x, b_x, lam,
                         mlp_w2, layer)
    return [*_out_proj_residual(y_c, y_d, out_proj.astype(BF16), h, norm_out), w1, w2]


def kernel(x, p, norm_mix, norm_ffn, norm_pl, mlp_w1, mlp_w2, pl_proj, pl_gate, e_in_proj, e_out_proj, s5_lam_re, s5_lam_im, s5_log_step, s5_b_re, s5_b_im, s5_c_re, s5_c_im, s5_d, s5_glu_w, s5_glu_b, ssd_conv_w, ssd_conv_b, ssd_dt_bias, ssd_a_log, ssd_d, ssd_norm, o_in_proj, o_out_proj, rwkv_mu, rwkv_w0, rwkv_w_up, rwkv_a0, rwkv_a_up, rwkv_g_up, rwkv_k_k, rwkv_k_a, rwkv_r_k, rwkv_ln_g, rwkv_ln_b, lru_conv_w, lru_conv_b, lru_w_a, lru_b_a, lru_w_x, lru_b_x, lru_lam, norm_final):
    bsz, seq, d = x.shape
    depth = p.shape[0]
    h = x.astype(F32).reshape(bsz * seq, d)
    hn = None
    wg_all, wp_all = pl_gate.astype(BF16), pl_proj.astype(BF16)
    p_all = p.reshape(depth, bsz * seq, -1)
    mlp_w1, mlp_w2 = mlp_w1.astype(F32), mlp_w2.astype(F32)
    for i in range(depth):
        j = i // 2
        common = (h, hn, bsz, seq, i, j, mlp_w1, mlp_w2, norm_mix[i], norm_ffn[i])
        if i % 2 == 0:
            h, hn, w1, w2 = _even_mixer(*common, e_in_proj, e_out_proj[j], s5_lam_re[j], s5_lam_im[j],
                                        s5_log_step[j], s5_b_re[j], s5_b_im[j], s5_c_re[j], s5_c_im[j], s5_d[j],
                                        s5_glu_w[j], s5_glu_b[j], ssd_conv_w[j], ssd_conv_b[j], ssd_dt_bias[j],
                                        ssd_a_log[j], ssd_d[j], ssd_norm[j])
        else:
            h, hn, w1, w2 = _odd_mixer(*common, o_in_proj, o_out_proj[j], rwkv_mu[j], rwkv_w0[j],
                                       rwkv_w_up[j], rwkv_a0[j], rwkv_a_up[j], rwkv_g_up[j], rwkv_k_k[j], rwkv_k_a[j],
                                       rwkv_r_k[j], rwkv_ln_g[j], rwkv_ln_b[j], lru_conv_w[j], lru_conv_b[j],
                                       lru_w_a[j], lru_b_a[j], lru_w_x[j], lru_b_x[j], lru_lam[j])
        h = _mlp_residual(h, hn, w1, w2)
        gate_args = (h, norm_pl[i], wg_all, p_all, i, wp_all)
        if i + 1 < depth:
            h, hn = _gated_embed(*gate_args, norm_mix[i + 1], final=False)
        else:
            h = _gated_embed(*gate_args, norm_final, final=True)
    return h.reshape(bsz, seq, d)
```

```python
import functools

import jax
import jax.numpy as jnp
from jax import lax
from jax.experimental import pallas as pl
from jax.experimental.pallas import tpu as pltpu

F32 = jnp.float32
BF16 = jnp.bfloat16

V7X_LANES = 128
V7X_SUBLANES = 8
V7X_BF16_SUBLANES = 16
V7X_VMEM_BYTES = 64 * 1024 * 1024
VMEM_LIMIT_BYTES = V7X_VMEM_BYTES - 8 * 1024 * 1024

NORM_EPS = 1e-6
S5_GROUP = 16
S5_STATE = 64
SSD_HEAD_DIM = 64
SSD_HEADS = 24
SSD_GROUPS = 4
SSD_STATE = 128
SSD_CHUNK = 128
CONV_K = 4
RWKV_HEAD_DIM = 64
RWKV_GN_EPS = 64e-5
RWKV_CHUNK = 64
RWKV_INV_BLOCK = 16
LRU_BLOCK = 64
LRU_C = 8.0


def _cparams(*sem):
    return pltpu.CompilerParams(dimension_semantics=sem, vmem_limit_bytes=VMEM_LIMIT_BYTES)


def _bdot(a, b):
    return jnp.dot(a.astype(BF16), b.astype(BF16), preferred_element_type=F32)


def _bdot_nt(a, b):
    return lax.dot_general(a.astype(BF16), b.astype(BF16), (((1,), (1,)), ((), ())),
                           preferred_element_type=F32)


def _bdot_tn(a, b):
    return lax.dot_general(a.astype(BF16), b.astype(BF16), (((0,), (0,)), ((), ())),
                           preferred_element_type=F32)


def _split3(x):
    x1 = x.astype(BF16)
    r1 = x - x1.astype(F32)
    x2 = r1.astype(BF16)
    x3 = (r1 - x2.astype(F32)).astype(BF16)
    return x1, x2, x3


def _dot_lhs01(m01, x):
    m = m01.astype(BF16)
    x1, x2, x3 = _split3(x)
    d = lambda v: jnp.dot(m, v, preferred_element_type=F32)
    return d(x1) + d(x2) + d(x3)


def _dot_rhs01(x, m01):
    m = m01.astype(BF16)
    x1, x2, x3 = _split3(x)
    d = lambda v: jnp.dot(v, m, preferred_element_type=F32)
    return d(x1) + d(x2) + d(x3)


def _softplus(x):
    return jnp.maximum(x, 0.0) + jnp.log1p(jnp.exp(-jnp.abs(x)))


def _sigmoid(x):
    return jax.nn.sigmoid(x)


def _gelu(x):
    return jax.nn.gelu(x, approximate=True)


def _silu(x):
    return x * _sigmoid(x)


def _row_iota(shape):
    return lax.broadcasted_iota(jnp.int32, shape, 0)


def _delayed(x, tail_ref, delays):
    rows, c = x.shape
    nt = rows // V7X_SUBLANES
    x3 = x.reshape(nt, V7X_SUBLANES, c)
    tail = tail_ref[...].reshape(1, V7X_SUBLANES, c)
    sub = lax.broadcasted_iota(jnp.int32, (nt, V7X_SUBLANES, c), 1)
    out = []
    for d in delays:
        cur = pltpu.roll(x3, d, 1)
        prev = jnp.concatenate([pltpu.roll(tail, d, 1), cur[:nt - 1]], axis=0)
        out.append(jnp.where(sub >= d, cur, prev).reshape(rows, c))
    tail_ref[...] = x[rows - V7X_SUBLANES:, :]
    return out


def _causal_conv(x, ext_ref, w_ref, b_ref):
    acc = b_ref[...] + w_ref[CONV_K - 1:CONV_K, :] * x
    for k, xd in enumerate(_delayed(x, ext_ref, range(CONV_K - 1, 0, -1))):
        acc = acc + w_ref[k:k + 1, :] * xd
    return acc


def _with_weight_cast(body, n_in):
    def kernel(*refs):
        ins, wi_ref, o_ref, wo_ref, scratch = refs[:n_in], refs[n_in], refs[n_in + 1], refs[n_in + 2], refs[n_in + 3:]
        wo_ref[...] = wi_ref[...].astype(BF16)
        body(*ins, o_ref, *scratch)
    return kernel


def _cast_slab(w_all, layer, bsz, nt):
    _, rows, cols = w_all.shape
    blk = rows // (bsz * nt)
    assert blk * bsz * nt == rows and blk % V7X_BF16_SUBLANES == 0, (rows, bsz, nt)
    return (pl.BlockSpec((None, blk, cols), lambda b, t: (layer, b * nt + t, 0)),
            pl.BlockSpec((blk, cols), lambda b, t: (b * nt + t, 0)),
            jax.ShapeDtypeStruct((rows, cols), BF16))


def _regroup_kernel(wt_ref, o_ref, *, segments):
    o_ref[...] = jnp.zeros_like(o_ref)
    kb = o_ref.shape[0]
    lane = lax.broadcasted_iota(jnp.int32, (kb, V7X_LANES), 1)
    for src, width, dst in segments:
        full = width // V7X_LANES * V7X_LANES
        if full:
            o_ref[:, dst:dst + full] = wt_ref[src:src + full, :].T.astype(BF16)
        rem = width - full
        if rem:
            end = src + width
            win = pltpu.roll(wt_ref[end - V7X_LANES:end, :].T, rem, 1)
            o_ref[:, dst + full:dst + full + V7X_LANES] = jnp.where(lane < rem, win, 0.0).astype(BF16)


def _regroup_columns(w_all, index, segments, n_out, *, rows_blk=256):
    _, k, n_in = w_all.shape
    rows_blk = min(rows_blk, k)
    assert k % rows_blk == 0 and rows_blk % V7X_LANES == 0
    for src, width, dst in segments:
        assert dst % V7X_LANES == 0 and src % V7X_SUBLANES == 0 and width % V7X_SUBLANES == 0
        assert width % V7X_LANES == 0 or src + width >= V7X_LANES
    return pl.pallas_call(
        functools.partial(_regroup_kernel, segments=tuple(segments)),
        grid=(k // rows_blk,),
        in_specs=[pl.BlockSpec((None, n_in, rows_blk), lambda i: (index, 0, i))],
        out_specs=pl.BlockSpec((rows_blk, n_out), lambda i: (i, 0)),
        out_shape=jax.ShapeDtypeStruct((k, n_out), BF16),
        compiler_params=_cparams("parallel"),
        name="regroup_columns",
    )(jnp.swapaxes(w_all.astype(F32), 1, 2))


def _rmsnorm_rows(h, g):
    ms = jnp.mean(h * h, axis=-1, keepdims=True)
    return h * lax.rsqrt(ms + NORM_EPS) * g


def _norm_mm_kernel(h_ref, g_ref, w_ref, o_ref):
    hn = _rmsnorm_rows(h_ref[...], g_ref[...]).astype(BF16)
    o_ref[...] = jnp.dot(hn, w_ref[...], preferred_element_type=F32)


def _col_tile(n, target):
    return max(t for t in range(V7X_LANES, max(target, V7X_LANES) + 1, V7X_LANES) if n % t == 0)


def _norm_matmul(h, g, w, *, tm, tn):
    m, k = h.shape
    n = w.shape[1]
    tm, tn = min(tm, m), _col_tile(n, tn)
    assert m % tm == 0 and n % tn == 0, (m, n, tm, tn)
    return pl.pallas_call(
        _norm_mm_kernel,
        grid=(m // tm, n // tn),
        in_specs=[pl.BlockSpec((tm, k), lambda i, j: (i, 0)),
                  pl.BlockSpec((1, k), lambda i, j: (0, 0)),
                  pl.BlockSpec((k, tn), lambda i, j: (0, j))],
        out_specs=pl.BlockSpec((tm, tn), lambda i, j: (i, j)),
        out_shape=jax.ShapeDtypeStruct((m, n), F32),
        compiler_params=_cparams("parallel", "parallel"),
        name="norm_matmul",
    )(h, g.reshape(1, k), w)


def _mm_kernel(a_ref, w_ref, o_ref):
    o_ref[...] = jnp.dot(a_ref[...], w_ref[...], preferred_element_type=F32)


def _in_proj(h, hn, g, w, *, tm=1024, tn=1536):
    if hn is None:
        return _norm_matmul(h, g, w, tm=tm, tn=tn)
    m, k = hn.shape
    n = w.shape[1]
    tm, tn = min(2 * tm, m), _col_tile(n, tn // 2)
    assert m % tm == 0
    return pl.pallas_call(
        _mm_kernel,
        grid=(m // tm, n // tn),
        in_specs=[pl.BlockSpec((tm, k), lambda i, j: (i, 0)), pl.BlockSpec((k, tn), lambda i, j: (0, j))],
        out_specs=pl.BlockSpec((tm, tn), lambda i, j: (i, j)),
        out_shape=jax.ShapeDtypeStruct((m, n), F32),
        compiler_params=_cparams("parallel", "parallel"),
        name="in_proj",
    )(hn, w)


def _out_proj_kernel(a1_ref, a2_ref, w_ref, r_ref, g_ref, o_ref, on_ref):
    k1 = a1_ref.shape[1]
    ho = (r_ref[...] + jnp.dot(a1_ref[...], w_ref[:k1, :], preferred_element_type=F32)
          + jnp.dot(a2_ref[...], w_ref[k1:, :], preferred_element_type=F32))
    o_ref[...] = ho
    on_ref[...] = _rmsnorm_rows(ho, g_ref[...]).astype(BF16)


def _out_proj_residual(a1, a2, w, res, g_next, *, tm=512):
    m, k1 = a1.shape
    k2 = a2.shape[1]
    n = w.shape[1]
    tm = min(tm, m)
    assert m % tm == 0
    rows = lambda width: pl.BlockSpec((tm, width), lambda i: (i, 0))
    const = lambda shape: pl.BlockSpec(shape, lambda i: (0, 0))
    return pl.pallas_call(
        _out_proj_kernel,
        grid=(m // tm,),
        in_specs=[rows(k1), rows(k2), const((k1 + k2, n)), rows(n), const((1, n))],
        out_specs=[rows(n), rows(n)],
        out_shape=[jax.ShapeDtypeStruct((m, n), F32), jax.ShapeDtypeStruct((m, n), BF16)],
        compiler_params=_cparams("parallel"),
        name="out_proj",
    )(a1, a2, w, res, g_next.reshape(1, n))


def _mlp_kernel(h_ref, hn_ref, w1_ref, w2_ref, o_ref, acc_ref, *, nf):
    j = pl.program_id(1)

    @pl.when(j == 0)
    def _():
        acc_ref[...] = jnp.zeros_like(acc_ref)

    mid = jnp.dot(hn_ref[...], w1_ref[...], preferred_element_type=F32)
    mid = jnp.square(jnp.maximum(mid, 0.0)).astype(BF16)
    acc_ref[...] += jnp.dot(mid, w2_ref[...], preferred_element_type=F32)

    @pl.when(j == nf - 1)
    def _():
        o_ref[...] = h_ref[...] + acc_ref[...]


def _mlp_residual(h, hn, w1, w2, *, tm=512, tf=1024):
    m, d = h.shape
    f = w1.shape[1]
    tm, tf = min(tm, m), min(tf, f)
    assert m % tm == 0 and f % tf == 0
    nf = f // tf
    return pl.pallas_call(
        functools.partial(_mlp_kernel, nf=nf),
        grid=(m // tm, nf),
        in_specs=[pl.BlockSpec((tm, d), lambda i, j: (i, 0)),
                  pl.BlockSpec((tm, d), lambda i, j: (i, 0)),
                  pl.BlockSpec((d, tf), lambda i, j: (0, j)),
                  pl.BlockSpec((tf, d), lambda i, j: (j, 0))],
        out_specs=pl.BlockSpec((tm, d), lambda i, j: (i, 0)),
        out_shape=jax.ShapeDtypeStruct((m, d), F32),
        scratch_shapes=[pltpu.VMEM((tm, d), F32)],
        compiler_params=_cparams("parallel", "arbitrary"),
        name="mlp",
    )(h, hn, w1, w2)


def _gate_kernel(h_ref, g_ref, wg_ref, p_ref, wp_ref, gn_ref, *o_refs, final):
    h = h_ref[...]
    hn = _rmsnorm_rows(h, g_ref[...]).astype(BF16)
    gate = _sigmoid(jnp.dot(hn, wg_ref[...], preferred_element_type=F32))
    pp = jnp.dot(p_ref[...].astype(BF16), wp_ref[...], preferred_element_type=F32)
    ho = h + gate * pp
    nxt = _rmsnorm_rows(ho, gn_ref[...])
    if final:
        o_refs[0][...] = nxt
    else:
        o_refs[0][...] = ho
        o_refs[1][...] = nxt.astype(BF16)


def _gated_embed(h, g, wg, p, layer, wp, g_next, *, final, tm=512):
    m, k = h.shape
    kp = p.shape[2]
    tm = min(tm, m)
    assert m % tm == 0 and wg.shape[1:] == (k, k)
    const = lambda shape: pl.BlockSpec(shape, lambda i: (0, 0))
    rows = lambda w: pl.BlockSpec((tm, w), lambda i: (i, 0))
    out_specs = rows(k) if final else [rows(k), rows(k)]
    out_shape = (jax.ShapeDtypeStruct((m, k), F32) if final
                 else [jax.ShapeDtypeStruct((m, k), F32), jax.ShapeDtypeStruct((m, k), BF16)])
    return pl.pallas_call(
        functools.partial(_gate_kernel, final=final),
        grid=(m // tm,),
        in_specs=[rows(k), const((1, k)), pl.BlockSpec((None, k, k), lambda i: (layer, 0, 0)),
                  pl.BlockSpec((None, tm, kp), lambda i: (layer, i, 0)),
                  pl.BlockSpec((None, kp, k), lambda i: (layer, 0, 0)), const((1, k))],
        out_specs=out_specs,
        out_shape=out_shape,
        compiler_params=_cparams("parallel"),
        name="gated_embed",
    )(h, g.reshape(1, k), wg, p, wp, g_next.reshape(1, k))


def _s5_prep_kernel(lr_ref, li_ref, st_ref, btr_ref, bti_ref, b8_ref, a8_re_ref, a8_im_ref):
    lr, li, st = lr_ref[...], li_ref[...], jnp.exp(st_ref[...])
    lrs, lis = lr * st, li * st

    def apow(n):
        mag = jnp.exp(n * lrs)
        return mag * jnp.cos(n * lis), mag * jnp.sin(n * lis)

    mag = jnp.exp(lrs)
    abar_re, abar_im = mag * jnp.cos(lis), mag * jnp.sin(lis)
    den = lr * lr + li * li
    nr = abar_re - 1.0
    coef_re = (nr * lr + abar_im * li) / den
    coef_im = (abar_im * lr - nr * li) / den
    btr, bti = btr_ref[...], bti_ref[...]
    bbar_re = coef_re * btr - coef_im * bti
    bbar_im = coef_re * bti + coef_im * btr
    nblk, taps_ch, lanes2 = b8_ref.shape
    lane_blk = lanes2 // 2
    ch_blk = taps_ch // V7X_SUBLANES
    rg = lax.broadcasted_iota(jnp.int32, (ch_blk, lane_blk), 0) // S5_GROUP
    cg = lax.broadcasted_iota(jnp.int32, (ch_blk, lane_blk), 1) // S5_STATE
    same = rg == cg
    for s in range(V7X_SUBLANES):
        if s == 0:
            br, bi = bbar_re, bbar_im
        else:
            pr, pi = apow(float(s))
            br, bi = pr * bbar_re - pi * bbar_im, pr * bbar_im + pi * bbar_re
        for j in range(nblk):
            ls = slice(j * lane_blk, (j + 1) * lane_blk)
            rs = slice(s * ch_blk, (s + 1) * ch_blk)
            tile = lambda m: jnp.where(same, jnp.concatenate([m[:, ls]] * (ch_blk // S5_GROUP), axis=0), 0.0)
            b8_ref[j, rs, :lane_blk] = tile(br).astype(BF16)
            b8_ref[j, rs, lane_blk:] = tile(bi).astype(BF16)
    a8r, a8i = apow(float(V7X_SUBLANES))
    a8_re_ref[...] = jnp.broadcast_to(a8r, a8_re_ref.shape)
    a8_im_ref[...] = jnp.broadcast_to(a8i, a8_im_ref.shape)


def _s5_kernel(u_ref, b8_ref, cre_ref, cim_ref, a8r_ref, a8i_ref, d_ref, gw_ref, gb_ref, o_ref,
               ue_ref, car_re, car_im):
    @pl.when(pl.program_id(1) == 0)
    def _():
        ue_ref[...] = jnp.zeros_like(ue_ref)
        car_re[...] = jnp.zeros_like(car_re)
        car_im[...] = jnp.zeros_like(car_im)

    u = u_ref[...]
    tt = u.shape[0]
    nblk, taps_ch, lanes2 = b8_ref.shape
    lane_blk = lanes2 // 2
    ch_blk = taps_ch // V7X_SUBLANES
    taps = [u] + _delayed(u, ue_ref, range(1, V7X_SUBLANES))
    ys = []
    for j in range(nblk):
        ls = slice(j * lane_blk, (j + 1) * lane_blk)
        cs = slice(j * ch_blk, (j + 1) * ch_blk)
        u8 = jnp.concatenate([t[:, cs].astype(BF16) for t in taps], axis=1)
        z = jnp.dot(u8, b8_ref[j], preferred_element_type=F32)
        pr, pi = a8r_ref[:, ls], a8i_ref[:, ls]
        cr, ci = car_re[:, ls], car_im[:, ls]
        xr, xi = [], []
        for i in range(tt // V7X_SUBLANES):
            rs = slice(i * V7X_SUBLANES, (i + 1) * V7X_SUBLANES)
            cr, ci = (z[rs, :lane_blk] + (pr * cr - pi * ci), z[rs, lane_blk:] + (pr * ci + pi * cr))
            xr.append(cr)
            xi.append(ci)
        car_re[:, ls] = cr
        car_im[:, ls] = ci
        ys.append(_bdot(jnp.concatenate(xr, axis=0), cre_ref[j]) - _bdot(jnp.concatenate(xi, axis=0), cim_ref[j]))
    y = jnp.concatenate(ys, axis=1) + d_ref[...] * u
    act = _gelu(y)
    o_ref[...] = (act * _sigmoid(_bdot(act, gw_ref[...]) + gb_ref[...])).astype(o_ref.dtype)


def _s5_mixer(proj, col_blk, bsz, seq, lam_re, lam_im, log_step, b_re, b_im, c_re, c_im,
              d_skip, glu_w, glu_b, cast_w, cast_layer, *, tt=512, lane_blk=128):
    ng, ns = lam_re.shape
    width = ng * S5_GROUP
    nstate = ng * ns
    tt = min(tt, seq)
    assert seq % tt == 0 and tt % V7X_SUBLANES == 0 and nstate % lane_blk == 0 and lane_blk % ns == 0
    nblk = nstate // lane_blk
    gpb = lane_blk // ns
    ch_blk = gpb * S5_GROUP
    flat = lambda a: a.astype(F32).reshape(1, nstate)
    step = jnp.broadcast_to(log_step.astype(F32)[:, None], (ng, ns))
    bt = lambda a: a.astype(F32).transpose(2, 0, 1).reshape(S5_GROUP, nstate)
    vm = pl.BlockSpec(memory_space=pltpu.VMEM)
    b8, a8_re, a8_im = pl.pallas_call(
        _s5_prep_kernel,
        in_specs=[vm] * 5,
        out_specs=[vm] * 3,
        out_shape=[jax.ShapeDtypeStruct((nblk, V7X_SUBLANES * ch_blk, 2 * lane_blk), BF16)]
        + [jax.ShapeDtypeStruct((V7X_SUBLANES, nstate), F32)] * 2,
        compiler_params=pltpu.CompilerParams(vmem_limit_bytes=VMEM_LIMIT_BYTES),
        name="s5_prep",
    )(flat(lam_re), flat(lam_im), flat(step), bt(b_re), bt(b_im))
    eye = jnp.eye(gpb, dtype=F32)
    cbd = lambda c: (c.astype(F32).reshape(nblk, gpb, S5_GROUP, ns).transpose(0, 1, 3, 2)[:, :, :, None, :]
                     * eye[None, :, None, :, None]).reshape(nblk, lane_blk, ch_blk).astype(BF16)
    nt = seq // tt
    cast_in, cast_out, cast_shape = _cast_slab(cast_w, cast_layer, bsz, nt)
    const = lambda shape: pl.BlockSpec(shape, lambda b, t: (0,) * len(shape))
    return pl.pallas_call(
        _with_weight_cast(_s5_kernel, 9),
        grid=(bsz, nt),
        in_specs=[pl.BlockSpec((tt, width), lambda b, t: (b * nt + t, col_blk)),
                  const((nblk, V7X_SUBLANES * ch_blk, 2 * lane_blk)),
                  const((nblk, lane_blk, ch_blk)), const((nblk, lane_blk, ch_blk)),
                  const((V7X_SUBLANES, nstate)), const((V7X_SUBLANES, nstate)),
                  const((1, width)), const((width, width)), const((1, width)), cast_in],
        out_specs=[pl.BlockSpec((tt, width), lambda b, t: (b * nt + t, 0)), cast_out],
        out_shape=[jax.ShapeDtypeStruct((bsz * seq, width), BF16), cast_shape],
        scratch_shapes=[pltpu.VMEM((V7X_SUBLANES, width), F32)]
        + [pltpu.VMEM((V7X_SUBLANES, nstate), F32)] * 2,
        compiler_params=_cparams("parallel", "arbitrary"),
        name="s5_mixer",
    )(proj, b8, cbd(c_re), cbd(c_im), a8_re, a8_im,
      d_skip.astype(F32).reshape(1, width), glu_w.astype(BF16), glu_b.astype(F32).reshape(1, width), cast_w)


def _ssd_kernel(z_ref, xs_ref, bc_ref, dt_ref, cwx_ref, cbx_ref, cwb_ref, cbb_ref, e64_ref,
                dtb_ref, alog_ref, dskx_ref, ng_ref, o_ref,
                tailx, tailb, st_ref):
    @pl.when(pl.program_id(1) == 0)
    def _():
        tailx[...] = jnp.zeros_like(tailx)
        tailb[...] = jnp.zeros_like(tailb)
        st_ref[...] = jnp.zeros_like(st_ref)

    L = SSD_CHUNK
    tt = xs_ref.shape[0]
    width = xs_ref.shape[1]
    gstate = SSD_GROUPS * SSD_STATE
    xs = _silu(_causal_conv(xs_ref[...], tailx, cwx_ref, cbx_ref))
    bc = _silu(_causal_conv(bc_ref[...], tailb, cwb_ref, cbb_ref))
    dt = _softplus(dt_ref[...] + dtb_ref[...])
    da = dt * (-jnp.exp(alog_ref[...]))
    e64 = e64_ref[...]
    xdt = xs * _dot_rhs01(dt, e64)
    ri = lax.broadcasted_iota(jnp.int32, (L, L), 0)
    ci = lax.broadcasted_iota(jnp.int32, (L, L), 1)
    tril = ri >= ci
    tri01 = tril.astype(BF16)
    lane_lo = lax.broadcasted_iota(jnp.int32, (1, V7X_LANES), 1) < SSD_HEAD_DIM
    pairs_per_group = SSD_HEADS // SSD_GROUPS // 2
    ys = []
    for c in range(tt // L):
        rs = slice(c * L, (c + 1) * L)
        acum = _dot_lhs01(tri01, da[rs, :])
        acum_t = acum.T
        acx = _dot_rhs01(acum, e64)
        ycols = []
        for g in range(SSD_GROUPS):
            bg = bc[rs, g * SSD_STATE:(g + 1) * SSD_STATE]
            cg = bc[rs, gstate + g * SSD_STATE:gstate + (g + 1) * SSD_STATE]
            scores = _bdot_nt(cg, bg).astype(BF16)
            bg_t = bg.T
            for pr in range(pairs_per_group):
                q = g * pairs_per_group + pr
                ls = slice(q * V7X_LANES, (q + 1) * V7X_LANES)
                xp = xdt[rs, ls]
                acp = acx[:, ls]
                yd = []
                for h in (2 * q, 2 * q + 1):
                    seg = acum[:, h:h + 1] - acum_t[h:h + 1, :]
                    decay = jnp.exp(jnp.where(tril, seg, -jnp.inf).astype(BF16))
                    yd.append(jnp.dot(scores * decay, xp.astype(BF16), preferred_element_type=F32))
                y = jnp.where(lane_lo, yd[0], yd[1])
                prev_t = st_ref[q]
                y = y + _bdot(cg, prev_t) * jnp.exp(acp)
                last = acp[L - 1:L, :]
                st_ref[q] = prev_t * jnp.exp(last) + _bdot(bg_t, xp * jnp.exp(last - acp))
                ycols.append(y)
        ys.append(jnp.concatenate(ycols, axis=1))
    y = jnp.concatenate(ys, axis=0) if len(ys) > 1 else ys[0]
    y = (y + xs * dskx_ref[...]) * _silu(z_ref[...])
    gw = width // SSD_GROUPS
    outs = []
    for g in range(SSD_GROUPS):
        yg = y[:, g * gw:(g + 1) * gw]
        outs.append(yg * lax.rsqrt(jnp.mean(yg * yg, axis=-1, keepdims=True) + NORM_EPS))
    o_ref[...] = (jnp.concatenate(outs, axis=1) * ng_ref[...]).astype(o_ref.dtype)


def _ssd_mixer(proj, blk_z, blk_xs, blk_bc, blk_dt, bsz, seq, conv_w, conv_b, dt_bias, a_log,
               d_skip, norm_g, cast_w, cast_layer, *, tt=512):
    width = SSD_HEADS * SSD_HEAD_DIM
    gstate2 = 2 * SSD_GROUPS * SSD_STATE
    tt = min(tt, seq)
    assert seq % tt == 0 and tt % SSD_CHUNK == 0
    nt = seq // tt
    cast_in, cast_out, cast_shape = _cast_slab(cast_w, cast_layer, bsz, nt)
    f = lambda a: a.astype(F32)
    pad_h = lambda a: jnp.pad(f(a), (0, V7X_LANES - SSD_HEADS)).reshape(1, V7X_LANES)
    exp_h = lambda a: jnp.repeat(f(a), SSD_HEAD_DIM).reshape(1, width)
    e64 = (jnp.arange(V7X_LANES)[:, None] == (jnp.arange(width)[None, :] // SSD_HEAD_DIM)).astype(BF16)
    const = lambda shape: pl.BlockSpec(shape, lambda b, t: (0,) * len(shape))
    rowblk = lambda w, cb: pl.BlockSpec((tt, w), lambda b, t: (b * nt + t, cb))
    return pl.pallas_call(
        _with_weight_cast(_ssd_kernel, 13),
        grid=(bsz, nt),
        in_specs=[rowblk(width, blk_z), rowblk(width, blk_xs), rowblk(gstate2, blk_bc),
                  rowblk(V7X_LANES, blk_dt),
                  const((CONV_K, width)), const((1, width)), const((CONV_K, gstate2)), const((1, gstate2)),
                  const((V7X_LANES, width)),
                  const((1, V7X_LANES)), const((1, V7X_LANES)),
                  const((1, width)), const((1, width)), cast_in],
        out_specs=[pl.BlockSpec((tt, width), lambda b, t: (b * nt + t, 0)), cast_out],
        out_shape=[jax.ShapeDtypeStruct((bsz * seq, width), BF16), cast_shape],
        scratch_shapes=[pltpu.VMEM((V7X_SUBLANES, width), F32),
                        pltpu.VMEM((V7X_SUBLANES, gstate2), F32),
                        pltpu.VMEM((SSD_HEADS // 2, SSD_STATE, V7X_LANES), F32)],
        compiler_params=_cparams("parallel", "arbitrary"),
        name="ssd_mixer",
    )(proj, proj, proj, proj,
      f(conv_w[:, :width]), f(conv_b[:width]).reshape(1, width),
      f(conv_w[:, width:]), f(conv_b[width:]).reshape(1, gstate2),
      e64, pad_h(dt_bias), pad_h(a_log), exp_h(d_skip),
      f(norm_g).reshape(1, width), cast_w)


def _stack_heads(x, lane_lo):
    return jnp.concatenate([jnp.where(lane_lo, x, 0.0), jnp.where(lane_lo, 0.0, x)], axis=0)


def _unit_lower_inverses(mats, same_blk, eye):
    size = eye.shape[0]
    side = lambda x, y: _bdot(x, jnp.concatenate([x, y], axis=1))
    ad = [jnp.where(same_blk, a, 0.0) for a in mats]
    ao = [a - d for a, d in zip(mats, ad)]
    s1 = [eye + d for d in ad]
    a2 = [_bdot(d, d) for d in ad]
    r = [side(x, s) for x, s in zip(a2, s1)]
    a4, s2 = [x[:, :size] for x in r], [s + x[:, size:] for s, x in zip(s1, r)]
    r = [side(x, s) for x, s in zip(a4, s2)]
    a8, s3 = [x[:, :size] for x in r], [s + x[:, size:] for s, x in zip(s2, r)]
    td = [s + _bdot(x, s) for x, s in zip(a8, s3)]
    n = [_bdot(t, o) for t, o in zip(td, ao)]
    r = [side(x, t) for x, t in zip(n, td)]
    n2, w1 = [x[:, :size] for x in r], [t + x[:, size:] for t, x in zip(td, r)]
    return [w + _bdot(x, w) for x, w in zip(n2, w1)]


def _rwkv_kernel(rkv_ref, lo_ref, mu_rkv_ref, mu_lo_ref, w0_ref, wup_ref, a0_ref, aup_ref, gup_ref,
                 kk_ref, ka_ref, rk_ref, lng_ref, lnb_ref, ones_ref, o_ref,
                 prev_rkv, prev_lo, state_ref):
    @pl.when(pl.program_id(1) == 0)
    def _():
        prev_rkv[...] = jnp.zeros_like(prev_rkv)
        prev_lo[...] = jnp.zeros_like(prev_lo)
        state_ref[...] = jnp.zeros_like(state_ref)

    L = RWKV_CHUNK
    tb = rkv_ref.shape[0]
    width = rkv_ref.shape[1] // 3

    def shift_mix(ref, prev, mu_ref):
        f = ref[...]
        fs = jnp.where(_row_iota(f.shape) == 0, prev[0:1, :], pltpu.roll(f, 1, 0))
        prev[0:1, :] = f[tb - 1:tb, :]
        return f + (fs - f) * mu_ref[...]

    rkv = shift_mix(rkv_ref, prev_rkv, mu_rkv_ref)
    lo = shift_mix(lo_ref, prev_lo, mu_lo_ref)
    r, k, v = rkv[:, :width], rkv[:, width:2 * width], rkv[:, 2 * width:]
    nl = wup_ref.shape[0]
    wl, al, gl = lo[:, :nl], lo[:, nl:2 * nl], lo[:, 2 * nl:]
    w = -_softplus(-(w0_ref[...] + _bdot(jnp.tanh(wl), wup_ref[...]))) - 0.5
    logw = -jnp.exp(w)
    a_sig = _sigmoid(a0_ref[...] + _bdot(al, aup_ref[...]))
    g = _bdot(_sigmoid(gl), gup_ref[...])
    kk = k * kk_ref[...]
    k = k * (1.0 + (a_sig - 1.0) * ka_ref[...])
    ones_bd = ones_ref[...]

    def head_sum(x):
        return jnp.dot(x.astype(BF16), ones_bd, preferred_element_type=F32)

    P2 = 2 * L
    ri = lax.broadcasted_iota(jnp.int32, (P2, P2), 0)
    ci = lax.broadcasted_iota(jnp.int32, (P2, P2), 1)
    same_head = (ri // L) == (ci // L)
    strict = same_head & (ri > ci)
    incl = same_head & (ri >= ci)
    same_blk = (ri // RWKV_INV_BLOCK) == (ci // RWKV_INV_BLOCK)
    eye = (ri == ci).astype(F32)
    tri01 = (lax.broadcasted_iota(jnp.int32, (L, L), 0) >= lax.broadcasted_iota(jnp.int32, (L, L), 1)).astype(BF16)
    lane_lo = lax.broadcasted_iota(jnp.int32, (1, V7X_LANES), 1) < RWKV_HEAD_DIM

    nq = width // V7X_LANES
    lss = [slice(q * V7X_LANES, (q + 1) * V7X_LANES) for q in range(nq)]
    kks = [kk[:, ls] for ls in lss]
    kkn = [x * lax.rsqrt(jnp.maximum(head_sum(x * x), 1e-24)) for x in kks]
    nc = tb // L
    inst = [(slice(c * L, (c + 1) * L), q, ls) for c in range(nc) for q, ls in enumerate(lss)]
    cums = [_dot_lhs01(tri01, logw[c * L:(c + 1) * L, :]) for c in range(nc)]
    g_in = jnp.concatenate([jnp.exp(cm) for cm in cums], axis=0)
    g_prev = jnp.concatenate([jnp.exp(cm - logw[c * L:(c + 1) * L, :]) for c, cm in enumerate(cums)], axis=0)
    g_inv = jnp.concatenate([jnp.exp(-cm) for cm in cums], axis=0)
    ats = [_stack_heads(-kkn[q][rs] * g_prev[rs, ls], lane_lo) for rs, q, ls in inst]
    rts = [_stack_heads(r[rs, ls] * g_in[rs, ls], lane_lo) for rs, q, ls in inst]
    bts = [_stack_heads(kkn[q][rs] * a_sig[rs, ls] * g_inv[rs, ls], lane_lo) for rs, q, ls in inst]
    kts = [_stack_heads(k[rs, ls] * g_inv[rs, ls], lane_lo) for rs, q, ls in inst]
    vss = [_stack_heads(v[rs, ls], lane_lo).astype(BF16) for rs, q, ls in inst]
    ars = [jnp.concatenate([a, b], axis=0).astype(BF16) for a, b in zip(ats, rts)]
    ps = [_bdot_nt(ar, jnp.concatenate([b, kq], axis=0)) for ar, b, kq in zip(ars, bts, kts)]
    tinv = _unit_lower_inverses([jnp.where(strict, p[:P2, :P2], 0.0) for p in ps], same_blk, eye)
    akv = [_bdot(jnp.where(strict, p[:P2, P2:], 0.0), vs) for p, vs in zip(ps, vss)]
    a_r = [jnp.concatenate([jnp.where(incl, p[P2:, :P2], 0.0), jnp.where(incl, p[P2:, P2:], 0.0)],
                           axis=1).astype(BF16) for p in ps]
    g_last = [g_in[rs, ls][L - 1:L, :] for rs, q, ls in inst]
    bkl = [(jnp.concatenate([b, kq], axis=0) * gl_).astype(BF16) for b, kq, gl_ in zip(bts, kts, g_last)]
    hts = [state_ref[q] for q in range(nq)]
    yrows = []
    for c in range(nc):
        sel = lambda xs: xs[c * nq:(c + 1) * nq]
        arh = [_bdot_nt(ar, ht) for ar, ht in zip(sel(ars), hts)]
        us = [_bdot(t, x[:P2] + y) for t, x, y in zip(sel(tinv), arh, sel(akv))]
        uv = [jnp.concatenate([u.astype(BF16), vs], axis=0) for u, vs in zip(us, sel(vss))]
        hts = [ht * gl_ + _bdot_tn(w, b) for ht, gl_, w, b in zip(hts, sel(g_last), uv, sel(bkl))]
        ys = [x[P2:] + jnp.dot(m, w, preferred_element_type=F32) for x, m, w in zip(arh, sel(a_r), uv)]
        yrows.append(jnp.concatenate([y[:L] + y[L:] for y in ys], axis=1))
    for q in range(nq):
        state_ref[q] = hts[q]
    y = jnp.concatenate(yrows, axis=0) if len(yrows) > 1 else yrows[0]
    inv_n = 1.0 / RWKV_HEAD_DIM
    outs = []
    for q, ls in enumerate(lss):
        yq = y[:, ls]
        mean = head_sum(yq) * inv_n
        yc = yq - mean
        var = head_sum(yc * yc) * inv_n
        yn = yc * lax.rsqrt(var + RWKV_GN_EPS) * lng_ref[:, ls] + lnb_ref[:, ls]
        bonus = head_sum(r[:, ls] * k[:, ls] * rk_ref[:, ls]) * v[:, ls]
        outs.append((yn + bonus) * g[:, ls])
    o_ref[...] = jnp.concatenate(outs, axis=1).astype(o_ref.dtype)


def _rwkv_mixer(proj, blk_rkv, blk_lo, bsz, seq, mu_rkv, mu_lo, w0, w_up, a0, a_up, g_up,
                k_k, k_a, r_k, ln_g, ln_b, cast_w, cast_layer, *, tb=256):
    width = w0.shape[0]
    nl = w_up.shape[0]
    tb = min(tb, seq)
    assert seq % tb == 0 and tb % RWKV_CHUNK == 0
    nt = seq // tb
    cast_in, cast_out, cast_shape = _cast_slab(cast_w, cast_layer, bsz, nt)
    f = lambda a: a.astype(F32).reshape(1, -1)
    lane = jnp.arange(V7X_LANES) // RWKV_HEAD_DIM
    ones_bd = (lane[:, None] == lane[None, :]).astype(BF16)
    const = lambda shape: pl.BlockSpec(shape, lambda b, t: (0,) * len(shape))
    return pl.pallas_call(
        _with_weight_cast(_rwkv_kernel, 15),
        grid=(bsz, nt),
        in_specs=[pl.BlockSpec((tb, 3 * width), lambda b, t: (b * nt + t, blk_rkv)),
                  pl.BlockSpec((tb, 4 * nl), lambda b, t: (b * nt + t, blk_lo)),
                  const((1, 3 * width)), const((1, 4 * nl)),
                  const((1, width)), const((nl, width)), const((1, width)), const((nl, width)),
                  const((2 * nl, width)),
                  const((1, width)), const((1, width)), const((1, width)), const((1, width)),
                  const((1, width)), const((V7X_LANES, V7X_LANES)), cast_in],
        out_specs=[pl.BlockSpec((tb, width), lambda b, t: (b * nt + t, 0)), cast_out],
        out_shape=[jax.ShapeDtypeStruct((bsz * seq, width), BF16), cast_shape],
        scratch_shapes=[pltpu.VMEM((V7X_SUBLANES, 3 * width), F32),
                        pltpu.VMEM((V7X_SUBLANES, 4 * nl), F32),
                        pltpu.VMEM((width // V7X_LANES, V7X_LANES, V7X_LANES), F32)],
        compiler_params=_cparams("parallel", "arbitrary"),
        name="rwkv_mixer",
    )(proj, proj, mu_rkv, mu_lo, f(w0), w_up.astype(BF16), f(a0), a_up.astype(BF16),
      g_up.astype(BF16), f(k_k), f(k_a), f(r_k), f(ln_g), f(ln_b), ones_bd, cast_w)


def _lru_kernel(xl_ref, gl_ref, cw_ref, cb_ref, wa_ref, ba_ref, wx_ref, bx_ref, lam_ref, o_ref,
                tail, carry):
    @pl.when(pl.program_id(1) == 0)
    def _():
        tail[...] = jnp.zeros_like(tail)
        carry[...] = jnp.zeros_like(carry)

    tt, width = xl_ref.shape
    row = _row_iota((tt, width))
    xc = _causal_conv(xl_ref[...], tail, cw_ref, cb_ref)
    gr, gi = [], []
    for j in range(width // V7X_LANES):
        xj = xc[:, j * V7X_LANES:(j + 1) * V7X_LANES]
        gr.append(_bdot(xj, wa_ref[j]))
        gi.append(_bdot(xj, wx_ref[j]))
    gate_r = _sigmoid(jnp.concatenate(gr, axis=1) + ba_ref[...])
    gate_i = _sigmoid(jnp.concatenate(gi, axis=1) + bx_ref[...])
    log_a = -LRU_C * gate_r * _softplus(-lam_ref[...])
    a = jnp.exp(log_a)
    mult = jnp.sqrt(jnp.maximum(-(jnp.tanh(log_a) * (a * a + 1.0)), 0.0))
    mult = jnp.where(row + pl.program_id(1) * tt == 0, 1.0, mult)
    b = xc * gate_i * mult
    nt = tt // V7X_SUBLANES
    a, b = a.reshape(nt, V7X_SUBLANES, width), b.reshape(nt, V7X_SUBLANES, width)
    sub = lax.broadcasted_iota(jnp.int32, a.shape, 1)
    for s in (1, 2, 4):
        keep = sub >= s
        sa, sb = pltpu.roll(a, s, 1), pltpu.roll(b, s, 1)
        b = b + jnp.where(keep, a * sb, 0.0)
        a = jnp.where(keep, a * sa, a)
    a, b = a.reshape(tt, width), b.reshape(tt, width)
    c = carry[0:1, :]
    hs = []
    for i in range(tt // V7X_SUBLANES):
        rs = slice(i * V7X_SUBLANES, (i + 1) * V7X_SUBLANES)
        hi = b[rs, :] + a[rs, :] * c
        hs.append(hi)
        c = hi[V7X_SUBLANES - 1:, :]
    carry[0:1, :] = c
    h = jnp.concatenate(hs, axis=0)
    o_ref[...] = (h * _gelu(gl_ref[...])).astype(o_ref.dtype)


def _lru_mixer(proj, blk_xl, blk_gl, bsz, seq, conv_w, conv_b, w_a, b_a, w_x, b_x, lam, cast_w, cast_layer,
               *, tt=512):
    nb, blk, _ = w_a.shape
    width = nb * blk
    tt = min(tt, seq)
    assert seq % tt == 0 and tt % V7X_SUBLANES == 0
    nt = seq // tt
    cast_in, cast_out, cast_shape = _cast_slab(cast_w, cast_layer, bsz, nt)
    per = V7X_LANES // blk
    eye = jnp.eye(per, dtype=F32)
    bd = lambda w: (w.astype(F32).reshape(nb // per, per, blk, 1, blk) * eye[None, :, None, :, None]
                    ).reshape(nb // per, V7X_LANES, V7X_LANES).astype(BF16)
    f = lambda a: a.astype(F32).reshape(1, width)
    const = lambda shape: pl.BlockSpec(shape, lambda b, t: (0,) * len(shape))
    return pl.pallas_call(
        _with_weight_cast(_lru_kernel, 9),
        grid=(bsz, nt),
        in_specs=[pl.BlockSpec((tt, width), lambda b, t: (b * nt + t, blk_xl)),
                  pl.BlockSpec((tt, width), lambda b, t: (b * nt + t, blk_gl)),
                  const((CONV_K, width)), const((1, width)),
                  const((nb // per, V7X_LANES, V7X_LANES)), const((1, width)),
                  const((nb // per, V7X_LANES, V7X_LANES)), const((1, width)), const((1, width)), cast_in],
        out_specs=[pl.BlockSpec((tt, width), lambda b, t: (b * nt + t, 0)), cast_out],
        out_shape=[jax.ShapeDtypeStruct((bsz * seq, width), BF16), cast_shape],
        scratch_shapes=[pltpu.VMEM((V7X_SUBLANES, width), F32), pltpu.VMEM((V7X_SUBLANES, width), F32)],
        compiler_params=_cparams("parallel", "arbitrary"),
        name="lru_mixer",
    )(proj, proj, conv_w.astype(F32), f(conv_b), bd(w_a), f(b_a), bd(w_x), f(b_x), f(lam), cast_w)


def _even_mixer(h, hn, bsz, seq, layer, group, mlp_w1, mlp_w2, norm_g, norm_out, in_proj, out_proj,
                lam_re, lam_im, log_step, b_re, b_im, c_re, c_im,
                s5_d, glu_w, glu_b, conv_w, conv_b, dt_bias, a_log, ssd_d, ssd_norm):
    s5w = lam_re.shape[0] * S5_GROUP
    ssdw = SSD_HEADS * SSD_HEAD_DIM
    gstate2 = 2 * SSD_GROUPS * SSD_STATE
    off_bc, off_u, off_dt = 2 * ssdw, 2 * ssdw + gstate2, 2 * ssdw + gstate2 + s5w
    n_dt = in_proj.shape[2] - (s5w + 2 * ssdw + gstate2)
    segments = [(s5w, ssdw, 0), (s5w + ssdw, ssdw + gstate2, ssdw), (0, s5w, off_u),
                (s5w + 2 * ssdw + gstate2, n_dt, off_dt)]
    w_all = _regroup_columns(in_proj, group, segments, -(-(off_dt + n_dt) // 512) * 512)
    proj = _in_proj(h, hn, norm_g, w_all)
    y_a, w1 = _s5_mixer(proj, off_u // s5w, bsz, seq, lam_re, lam_im, log_step, b_re, b_im, c_re, c_im,
                        s5_d, glu_w, glu_b, mlp_w1, layer)
    y_b, w2 = _ssd_mixer(proj, 0, 1, off_bc // gstate2, off_dt // V7X_LANES, bsz, seq,
                         conv_w, conv_b, dt_bias, a_log, ssd_d, ssd_norm, mlp_w2, layer)
    return [*_out_proj_residual(y_a, y_b, out_proj.astype(BF16), h, norm_out), w1, w2]


def _odd_mixer(h, hn, bsz, seq, layer, group, mlp_w1, mlp_w2, norm_g, norm_out, in_proj, out_proj,
               mu, w0, w_up, a0, a_up, g_up, k_k, k_a, r_k,
               ln_g, ln_b, conv_w, conv_b, w_a, b_a, w_x, b_x, lam):
    rw = w0.shape[0]
    nl = w_up.shape[0]
    ngl = g_up.shape[0]
    lw = lam.shape[0] * lam.shape[1]
    nlp = V7X_LANES
    assert nl <= nlp and ngl == 2 * nlp
    off_xl, off_gl, off_lo = 3 * rw, 3 * rw + lw, 3 * rw + 2 * lw
    src_lo = 3 * rw + 2 * nl + ngl
    segments = [(0, 3 * rw, 0), (src_lo, lw, off_xl), (src_lo + lw, lw, off_gl),
                (3 * rw, nl, off_lo), (3 * rw + nl, nl, off_lo + nlp), (3 * rw + 2 * nl, ngl, off_lo + 2 * nlp)]
    w_all = _regroup_columns(in_proj, group, segments, off_lo + 4 * nlp)
    mu = mu.astype(F32)
    mu_rkv = mu[:3 * rw].reshape(1, -1)
    padv = lambda v: jnp.pad(v, (0, nlp - v.shape[0]))
    mu_lo = jnp.concatenate([padv(mu[3 * rw:3 * rw + nl]), padv(mu[3 * rw + nl:3 * rw + 2 * nl]),
                             mu[3 * rw + 2 * nl:]]).reshape(1, -1)
    padr = lambda w: jnp.pad(w, ((0, nlp - w.shape[0]), (0, 0)))
    proj = _in_proj(h, hn, norm_g, w_all)
    y_c, w1 = _rwkv_mixer(proj, 0, off_lo // (4 * nlp), bsz, seq, mu_rkv, mu_lo, w0, padr(w_up), a0, padr(a_up),
                          g_up, k_k, k_a, r_k.reshape(-1), ln_g, ln_b, mlp_w1, layer)
    y_d, w2 = _lru_mixer(proj, off_xl // lw, off_gl // lw, bsz, seq, conv_w, conv_b, w_a, b_a, w_x, b_x, lam,
                         mlp_w2, layer)
    return [*_out_proj_residual(y_c, y_d, out_proj.astype(BF16), h, norm_out), w1, w2]


def kernel(x, p, norm_mix, norm_ffn, norm_pl, mlp_w1, mlp_w2, pl_proj, pl_gate, e_in_proj, e_out_proj, s5_lam_re, s5_lam_im, s5_log_step, s5_b_re, s5_b_im, s5_c_re, s5_c_im, s5_d, s5_glu_w, s5_glu_b, ssd_conv_w, ssd_conv_b, ssd_dt_bias, ssd_a_log, ssd_d, ssd_norm, o_in_proj, o_out_proj, rwkv_mu, rwkv_w0, rwkv_w_up, rwkv_a0, rwkv_a_up, rwkv_g_up, rwkv_k_k, rwkv_k_a, rwkv_r_k, rwkv_ln_g, rwkv_ln_b, lru_conv_w, lru_conv_b, lru_w_a, lru_b_a, lru_w_x, lru_b_x, lru_lam, norm_final):
    bsz, seq, d = x.shape
    depth = p.shape[0]
    h = x.astype(F32).reshape(bsz * seq, d)
    hn = None
    wg_all, wp_all = pl_gate.astype(BF16), pl_proj.astype(BF16)
    p_all = p.reshape(depth, bsz * seq, -1)
    mlp_w1, mlp_w2 = mlp_w1.astype(F32), mlp_w2.astype(F32)
    for i in range(depth):
        j = i // 2
        common = (h, hn, bsz, seq, i, j, mlp_w1, mlp_w2, norm_mix[i], norm_ffn[i])
        if i % 2 == 0:
            h, hn, w1, w2 = _even_mixer(*common, e_in_proj, e_out_proj[j], s5_lam_re[j], s5_lam_im[j],
                                        s5_log_step[j], s5_b_re[j], s5_b_im[j], s5_c_re[j], s5_c_im[j], s5_d[j],
                                        s5_glu_w[j], s5_glu_b[j], ssd_conv_w[j], ssd_conv_b[j], ssd_dt_bias[j],
                                        ssd_a_log[j], ssd_d[j], ssd_norm[j])
        else:
            h, hn, w1, w2 = _odd_mixer(*common, o_in_proj, o_out_proj[j], rwkv_mu[j], rwkv_w0[j],
                                       rwkv_w_up[j], rwkv_a0[j], rwkv_a_up[j], rwkv_g_up[j], rwkv_k_k[j], rwkv_k_a[j],
                                       rwkv_r_k[j], rwkv_ln_g[j], rwkv_ln_b[j], lru_conv_w[j], lru_conv_b[j],
                                       lru_w_a[j], lru_b_a[j], lru_w_x[j], lru_b_x[j], lru_lam[j])
        h = _mlp_residual(h, hn, w1, w2)
        gate_args = (h, norm_pl[i], wg_all, p_all, i, wp_all)
        if i + 1 < depth:
            h, hn = _gated_embed(*gate_args, norm_mix[i + 1], final=False)
        else:
            h = _gated_embed(*gate_args, norm_final, final=True)
    return h.reshape(bsz, seq, d)
```
